```python
import jax, jax.numpy as jnp
from jax import lax
import numpy as np

D_MODEL = 1024
BATCH = 16
SEQ = 256
DEPTH = 1
DEC_BATCH = 8
DEC_SEQ = 4096
PAST_LEN = 512

GRID_W = 64
CONV_WIDTH = D_MODEL // 2
DN_HEADS = 4
DN_DK = (D_MODEL // 2) // DN_HEADS
DN_DV = (D_MODEL // 2) // DN_HEADS
DN_KEY = DN_HEADS * DN_DK
DN_VAL = DN_HEADS * DN_DV
CHUNK = 64
D_FF = 2816
N_MOD = 9
EPS = 1e-6
PROJ_DIM = 3 * CONV_WIDTH + 2 * DN_KEY + 2 * DN_VAL + 4 * DN_HEADS
SPLITS = [CONV_WIDTH, 2 * CONV_WIDTH, 3 * CONV_WIDTH,
          3 * CONV_WIDTH + 2 * DN_KEY + DN_VAL,
          3 * CONV_WIDTH + 2 * DN_KEY + 2 * DN_VAL]

kernel_name = "hybrid_conv_gdn_macaron_diffusion_step"


def _rmsnorm(x, g):
    xf = x.astype(jnp.float32)
    y = xf * lax.rsqrt(jnp.mean(xf * xf, axis=-1, keepdims=True) + EPS)
    return (y * g.astype(jnp.float32)).astype(x.dtype)


def _l2norm(x):
    return x * lax.rsqrt(jnp.sum(x * x, axis=-1, keepdims=True) + EPS)


def _conv3(x, w):
    zeros = jnp.zeros_like(x[:, :1])
    prev = jnp.concatenate([zeros, x[:, :-1]], axis=1)
    nxt = jnp.concatenate([x[:, 1:], zeros], axis=1)
    return prev * w[0] + x * w[1] + nxt * w[2]


def _swiglu(h, w_in, w_out):
    a, b = jnp.split(h @ w_in, 2, axis=-1)
    return (jax.nn.silu(a) * b) @ w_out


def _gated_delta(q, k, v, beta, g, s0):
    B, L, H, DK = q.shape
    n = L // CHUNK
    to_c = lambda t: t.reshape(B, n, CHUNK, H, -1).transpose(1, 0, 3, 2, 4)
    to_s = lambda t: t.reshape(B, n, CHUNK, H).transpose(1, 0, 3, 2)
    qc, kc, vc = to_c(q * (DK ** -0.5)), to_c(k), to_c(v)
    bc = to_s(beta)
    gc = jnp.cumsum(to_s(g), axis=-1)
    tril = jnp.tril(jnp.ones((CHUNK, CHUNK), bool))
    strict = jnp.tril(jnp.ones((CHUNK, CHUNK), bool), -1)
    diff = gc[..., :, None] - gc[..., None, :]
    decay = jnp.where(tril, jnp.exp(jnp.where(tril, diff, 0.0)), 0.0)
    kb = kc * bc[..., None]
    lmat = jnp.where(strict, jnp.einsum('nbhcd,nbhed->nbhce', kb, kc) * decay, 0.0)
    eye = jnp.eye(CHUNK, dtype=jnp.float32)
    tmat = lax.linalg.triangular_solve(eye + lmat, jnp.broadcast_to(eye, lmat.shape),
                                       left_side=True, lower=True, unit_diagonal=True)
    u = tmat @ (vc * bc[..., None])
    wk = tmat @ (kb * jnp.exp(gc)[..., None])
    qk = jnp.where(tril, jnp.einsum('nbhcd,nbhed->nbhce', qc, kc) * decay, 0.0)
    qg = qc * jnp.exp(gc)[..., None]
    kg = kc * jnp.exp(gc[..., -1:] - gc)[..., None]
    g_last = jnp.exp(gc[..., -1])

    def step(s, inp):
        qg_i, kg_i, u_i, w_i, qk_i, gl_i = inp
        v_new = u_i - w_i @ s
        o = qg_i @ s + qk_i @ v_new
        s = s * gl_i[..., None, None] + jnp.einsum('bhcd,bhce->bhde', kg_i, v_new)
        return s, o

    s_fin, o = lax.scan(step, s0, (qg, kg, u, wk, qk, g_last))
    o = o.transpose(1, 0, 3, 2, 4).reshape(B, L, H, -1)
    return o, s_fin


def _mixer(h, w_in, conv_w, dn_conv_w, a_log, dt_bias, dn_norm, w_out, s0_fwd, s0_bwd, on_grid):
    B, L, _ = h.shape
    f32 = jnp.float32
    proj = h @ w_in
    cb, cc, ch, qkv, z, ab = jnp.split(proj, SPLITS, axis=-1)
    u = cc * ch
    if on_grid:
        rows = L // GRID_W
        u = _conv3(u.reshape(B, rows, GRID_W, CONV_WIDTH), conv_w).reshape(B, L, CONV_WIDTH)
    else:
        u = _conv3(u, conv_w)
    y_conv = cb * u
    qkv = jax.nn.silu(_conv3(qkv, dn_conv_w))
    q, k, v = jnp.split(qkv, [DN_KEY, 2 * DN_KEY], axis=-1)
    q = _l2norm(q.astype(f32).reshape(B, L, DN_HEADS, DN_DK))
    k = _l2norm(k.astype(f32).reshape(B, L, DN_HEADS, DN_DK))
    v = v.astype(f32).reshape(B, L, DN_HEADS, DN_DV)
    ab = ab.astype(f32).reshape(B, L, 4, DN_HEADS)
    A = jnp.exp(a_log.astype(f32))
    dtb = dt_bias.astype(f32)
    g_f = -A[0] * jax.nn.softplus(ab[:, :, 0] + dtb[0])
    beta_f = jax.nn.sigmoid(ab[:, :, 1])
    g_b = -A[1] * jax.nn.softplus(ab[:, :, 2] + dtb[1])
    beta_b = jax.nn.sigmoid(ab[:, :, 3])
    o_f, s_f = _gated_delta(q, k, v, beta_f, g_f, s0_fwd.astype(f32))
    rev = lambda t: t[:, ::-1]
    o_b, s_b = _gated_delta(rev(q), rev(k), rev(v), rev(beta_b), rev(g_b), s0_bwd.astype(f32))
    o = o_f + rev(o_b)
    o = _rmsnorm(o, dn_norm) * jax.nn.silu(z.astype(f32)).reshape(B, L, DN_HEADS, DN_DV)
    y_dn = o.reshape(B, L, DN_VAL).astype(h.dtype)
    out = jnp.concatenate([y_conv, y_dn], axis=-1) @ w_out
    return out, s_f.astype(h.dtype), s_b.astype(h.dtype)


def _layer(x, mod, s0_fwd, s0_bwd, on_grid, norm_ffn1, w_ffn1_in, w_ffn1_out, norm_mix, w_mix_in, conv_w,
           dn_conv_w, dn_a_log, dn_dt_bias, dn_norm, w_mix_out, norm_ffn2, w_ffn2_in, w_ffn2_out):
    sh1, sc1, gt1, sh2, sc2, gt2, sh3, sc3, gt3 = jnp.split(mod, N_MOD, axis=-1)
    h = _rmsnorm(x, norm_ffn1) * (1 + sc1) + sh1
    x = x + 0.5 * gt1 * _swiglu(h, w_ffn1_in, w_ffn1_out)
    h = _rmsnorm(x, norm_mix) * (1 + sc2) + sh2
    m, s_f, s_b = _mixer(h, w_mix_in, conv_w, dn_conv_w, dn_a_log, dn_dt_bias, dn_norm, w_mix_out,
                         s0_fwd, s0_bwd, on_grid)
    x = x + gt2 * m
    h = _rmsnorm(x, norm_ffn2) * (1 + sc3) + sh3
    x = x + 0.5 * gt3 * _swiglu(h, w_ffn2_in, w_ffn2_out)
    return x, s_f, s_b


def setup_inputs(seed: int = 0) -> dict:
    key = jax.random.key(seed)
    ks = jax.random.split(key, 24)
    nrm = lambda k, shape, s: jax.random.normal(k, shape, jnp.float32) * s
    gain = lambda k, shape: 1.0 + 0.02 * jax.random.normal(k, shape, jnp.float32)
    dt = jnp.exp(jax.random.uniform(ks[15], (DEPTH, 2, DN_HEADS), jnp.float32,
                                    np.log(1e-3), np.log(1e-1)))
    return {
        "x_prompt": nrm(ks[0], (BATCH, SEQ, D_MODEL), 1.0),
        "x_sample": nrm(ks[1], (DEC_BATCH, DEC_SEQ, D_MODEL), 1.0),
        "state_dn_fwd": nrm(ks[2], (DEC_BATCH, DEPTH, DN_HEADS, DN_DK, DN_DV), 1.0),
        "state_dn_bwd": nrm(ks[3], (DEC_BATCH, DEPTH, DN_HEADS, DN_DK, DN_DV), 1.0),
        "c": nrm(ks[4], (DEC_BATCH, D_MODEL), 1.0),
        "c_ctx": nrm(ks[5], (D_MODEL,), 1.0),
        "w_ada": nrm(ks[6], (DEPTH, D_MODEL, N_MOD * D_MODEL), 0.5 * D_MODEL ** -0.5),
        "b_ada": nrm(ks[7], (DEPTH, N_MOD * D_MODEL), 0.02),
        "norm_ffn1": gain(ks[8], (DEPTH, D_MODEL)),
        "w_ffn1_in": nrm(ks[9], (DEPTH, D_MODEL, 2 * D_FF), D_MODEL ** -0.5),
        "w_ffn1_out": nrm(ks[10], (DEPTH, D_FF, D_MODEL), D_FF ** -0.5),
        "norm_mix": gain(ks[11], (DEPTH, D_MODEL)),
        "w_mix_in": nrm(ks[12], (DEPTH, D_MODEL, PROJ_DIM), D_MODEL ** -0.5),
        "conv_w": nrm(ks[13], (DEPTH, 3, CONV_WIDTH), 3 ** -0.5),
        "dn_conv_w": nrm(ks[14], (DEPTH, 3, 2 * DN_KEY + DN_VAL), 3 ** -0.5),
        "dn_a_log": jnp.log(jax.random.uniform(ks[16], (DEPTH, 2, DN_HEADS), jnp.float32, 1.0, 16.0)),
        "dn_dt_bias": dt + jnp.log(-jnp.expm1(-dt)),
        "dn_norm": gain(ks[17], (DEPTH, DN_DV)),
        "w_mix_out": nrm(ks[18], (DEPTH, CONV_WIDTH + DN_VAL, D_MODEL), (CONV_WIDTH + DN_VAL) ** -0.5),
        "norm_ffn2": gain(ks[19], (DEPTH, D_MODEL)),
        "w_ffn2_in": nrm(ks[20], (DEPTH, D_MODEL, 2 * D_FF), D_MODEL ** -0.5),
        "w_ffn2_out": nrm(ks[21], (DEPTH, D_FF, D_MODEL), D_FF ** -0.5),
        "norm_final": gain(ks[22], (D_MODEL,)),
    }


def reference(x_prompt, x_sample, state_dn_fwd, state_dn_bwd, c, c_ctx, w_ada, b_ada, norm_ffn1, w_ffn1_in,
              w_ffn1_out, norm_mix, w_mix_in, conv_w, dn_conv_w, dn_a_log, dn_dt_bias, dn_norm, w_mix_out,
              norm_ffn2, w_ffn2_in, w_ffn2_out, norm_final):
    xp, xs = x_prompt, x_sample
    nb = xp.shape[0]
    zero_state = jnp.zeros((nb, DN_HEADS, DN_DK, DN_DV), jnp.float32)
    new_f, new_b = [], []
    for l in range(DEPTH):
        lw = (norm_ffn1[l], w_ffn1_in[l], w_ffn1_out[l], norm_mix[l], w_mix_in[l], conv_w[l], dn_conv_w[l],
              dn_a_log[l], dn_dt_bias[l], dn_norm[l], w_mix_out[l], norm_ffn2[l], w_ffn2_in[l], w_ffn2_out[l])
        mod_ctx = (jax.nn.silu(c_ctx) @ w_ada[l] + b_ada[l])[None, None, :]
        mod_lat = (jax.nn.silu(c) @ w_ada[l] + b_ada[l])[:, None, :]
        xp, sf, sb = _layer(xp, mod_ctx, zero_state, zero_state, False, *lw)
        new_f.append(sf)
        new_b.append(sb)
        xs, _, _ = _layer(xs, mod_lat, state_dn_fwd[:, l], state_dn_bwd[:, l], True, *lw)
    y_prompt = _rmsnorm(xp, norm_final)
    y_sample = _rmsnorm(xs, norm_final)
    new_state_dn_fwd = jnp.stack(new_f, axis=1)
    new_state_dn_bwd = jnp.stack(new_b, axis=1)
    return (y_prompt, y_sample, new_state_dn_fwd, new_state_dn_bwd)
```

```python
import functools

import jax
import jax.numpy as jnp
from jax import lax
from jax.experimental import pallas as pl
from jax.experimental.pallas import tpu as pltpu

F32 = jnp.float32
BF16 = jnp.bfloat16

EPS = 1e-6
CHUNK = 64
GRID_W = 64
N_HEADS = 4
HEAD_DIM = 128
N_MOD = 9
FF_CHUNK = 256
VMEM_LIMIT = 56 * 1024 * 1024


def _cparams(*sem):
    return pltpu.CompilerParams(dimension_semantics=sem, vmem_limit_bytes=VMEM_LIMIT)


def _bdot(a, b):
    return jnp.dot(a.astype(BF16), b.astype(BF16), preferred_element_type=F32)


def _split_bf16(a):
    hi = a.astype(BF16)
    lo = (a - hi.astype(F32)).astype(BF16)
    return hi, lo


def _dot_x3(a, b):
    a_hi, a_lo = _split_bf16(a)
    b_hi, b_lo = _split_bf16(b)
    d = lambda x, y: jnp.dot(x, y, preferred_element_type=F32)
    return d(a_hi, b_hi) + (d(a_hi, b_lo) + d(a_lo, b_hi))


def _sigmoid(x):
    return 1.0 / (1.0 + jnp.exp(-x))


def _silu(x):
    return x * _sigmoid(x)


def _rms(x):
    return x * lax.rsqrt(jnp.mean(x * x, axis=-1, keepdims=True) + EPS)


def _mod_kernel(c_ref, w_ref, b_ref, o_ref):
    s = _silu(c_ref[...])
    o_ref[...] = _dot_x3(s, w_ref[...]) + b_ref[...]


def _modulation(cvec, w_ada, b_ada):
    rows, d = cvec.shape
    n = w_ada.shape[1]
    tn = 1536
    return pl.pallas_call(
        _mod_kernel,
        grid=(n // tn,),
        in_specs=[pl.BlockSpec((rows, d), lambda i: (0, 0)),
                  pl.BlockSpec((d, tn), lambda i: (0, i)),
                  pl.BlockSpec((1, tn), lambda i: (0, i))],
        out_specs=pl.BlockSpec((rows, tn), lambda i: (0, i)),
        out_shape=jax.ShapeDtypeStruct((rows, n), F32),
        compiler_params=_cparams("arbitrary"),
        name="modulation",
    )(cvec, w_ada, b_ada.reshape(1, n))


def _ffn_kernel(x_ref, mod_ref, g_ref, wa_ref, wb_ref, wo_ref, gfin_ref, o_ref, h_ref, acc_ref,
                *, mod_row, final_norm):
    x = x_ref[0]
    sh = mod_ref[0, mod_row:mod_row + 1, :]
    sc = mod_ref[0, mod_row + 1:mod_row + 2, :]
    gt = mod_ref[0, mod_row + 2:mod_row + 3, :]
    h_ref[...] = (_rms(x) * g_ref[...] * (1.0 + sc) + sh).astype(BF16)
    acc_ref[...] = jnp.zeros_like(acc_ref)

    def body(c, carry):
        h = h_ref[...]
        a = jnp.dot(h, wa_ref[c], preferred_element_type=F32)
        b = jnp.dot(h, wb_ref[c], preferred_element_type=F32)
        act = (_silu(a) * b).astype(BF16)
        acc_ref[...] += jnp.dot(act, wo_ref[c], preferred_element_type=F32)
        return carry

    lax.fori_loop(0, wa_ref.shape[0], body, 0)
    y = x + 0.5 * gt * acc_ref[...]
    if final_norm:
        y = _rms(y) * gfin_ref[...]
    o_ref[0] = y


def _ffn(x, mod, per_batch_mod, mod_row, g, wa, wb, wo, gfin, final_norm, tm):
    bsz, seq, d = x.shape
    nck, _, ck = wa.shape
    mod_map = (lambda b, i: (b, 0, 0)) if per_batch_mod else (lambda b, i: (0, 0, 0))
    const3 = lambda b, i: (0, 0, 0)
    const2 = lambda b, i: (0, 0)
    return pl.pallas_call(
        functools.partial(_ffn_kernel, mod_row=mod_row, final_norm=final_norm),
        grid=(bsz, seq // tm),
        in_specs=[pl.BlockSpec((1, tm, d), lambda b, i: (b, i, 0)),
                  pl.BlockSpec((1, N_MOD, d), mod_map),
                  pl.BlockSpec((1, d), const2),
                  pl.BlockSpec((nck, d, ck), const3, pipeline_mode=pl.Buffered(1)),
                  pl.BlockSpec((nck, d, ck), const3, pipeline_mode=pl.Buffered(1)),
                  pl.BlockSpec((nck, ck, d), const3, pipeline_mode=pl.Buffered(1)),
                  pl.BlockSpec((1, d), const2)],
        out_specs=pl.BlockSpec((1, tm, d), lambda b, i: (b, i, 0)),
        out_shape=jax.ShapeDtypeStruct(x.shape, F32),
        scratch_shapes=[pltpu.VMEM((tm, d), BF16), pltpu.VMEM((tm, d), F32)],
        compiler_params=_cparams("parallel", "arbitrary"),
        name="ffn_final" if final_norm else "ffn",
    )(x, mod, g, wa, wb, wo, gfin)


def _proj_kernel(x_ref, mod_ref, g_ref, wc_ref, wq_ref, wz_ref, wab_ref, pc_ref, pq_ref, pz_ref, pab_ref):
    x = x_ref[0]
    sh = mod_ref[0, 3:4, :]
    sc = mod_ref[0, 4:5, :]
    h = (_rms(x) * g_ref[...] * (1.0 + sc) + sh).astype(BF16)
    pc_ref[0] = jnp.dot(h, wc_ref[...], preferred_element_type=F32)
    pq_ref[0] = jnp.dot(h, wq_ref[...], preferred_element_type=F32)
    pz_ref[0] = jnp.dot(h, wz_ref[...], preferred_element_type=F32)
    pab_ref[0] = jnp.dot(h, wab_ref[...], preferred_element_type=F32)


def _proj(x, mod, per_batch_mod, g, wc, wq, wz, wab, tm):
    bsz, seq, d = x.shape
    mod_map = (lambda b, i: (b, 0, 0)) if per_batch_mod else (lambda b, i: (0, 0, 0))
    const2 = lambda b, i: (0, 0)
    tok = lambda b, i: (b, i, 0)
    widths = (wc.shape[1], wq.shape[1], wz.shape[1], wab.shape[1])
    return pl.pallas_call(
        _proj_kernel,
        grid=(bsz, seq // tm),
        in_specs=[pl.BlockSpec((1, tm, d), tok),
                  pl.BlockSpec((1, N_MOD, d), mod_map),
                  pl.BlockSpec((1, d), const2)]
                 + [pl.BlockSpec((d, w), const2, pipeline_mode=pl.Buffered(1)) for w in widths],
        out_specs=[pl.BlockSpec((1, tm, w), tok) for w in widths],
        out_shape=[jax.ShapeDtypeStruct((bsz, seq, w), F32) for w in widths],
        compiler_params=_cparams("parallel", "arbitrary"),
        name="mix_in_proj",
    )(x, mod, g, wc, wq, wz, wab)


def _shift_rows(cur, prev_halo, next_halo, dist, first, last):
    tm = cur.shape[0]
    hp = jnp.where(first, 0.0, prev_halo)
    hn = jnp.where(last, 0.0, next_halo)
    if dist % 8 == 0:
        prev = jnp.concatenate([hp[hp.shape[0] - dist:], cur[:tm - dist]], axis=0)
        nxt = jnp.concatenate([cur[dist:], hn[:dist]], axis=0)
        return prev, nxt
    assert dist == 1
    row = lax.broadcasted_iota(jnp.int32, cur.shape, 0)
    prev = jnp.where(row == 0, hp[hp.shape[0] - 1:], pltpu.roll(cur, 1, axis=0))
    nxt = jnp.where(row == tm - 1, hn[:1], pltpu.roll(cur, tm - 1, axis=0))
    return prev, nxt


def _prep_kernel(pc_ref, pcp_ref, pcn_ref, pq_ref, pqp_ref, pqn_ref, pab_ref,
                 cw_ref, dw_ref, alog_ref, dtb_ref,
                 yc_ref, q_ref, k_ref, v_ref, gb_ref, *, conv_dist, n_tiles):
    i = pl.program_id(1)
    first = i == 0
    last = i == n_tiles - 1
    cwid = yc_ref.shape[-1]
    kwid = q_ref.shape[-1]

    pc = pc_ref[0]
    u = pc[:, cwid:2 * cwid] * pc[:, 2 * cwid:]
    pcp = pcp_ref[0]
    pcn = pcn_ref[0]
    up, un = _shift_rows(u, pcp[:, cwid:2 * cwid] * pcp[:, 2 * cwid:],
                         pcn[:, cwid:2 * cwid] * pcn[:, 2 * cwid:], conv_dist, first, last)
    cw = cw_ref[...]
    yc_ref[0] = pc[:, :cwid] * (up * cw[0:1] + u * cw[1:2] + un * cw[2:3])

    pq = pq_ref[0]
    qp, qn = _shift_rows(pq, pqp_ref[0], pqn_ref[0], 1, first, last)
    dw = dw_ref[...]
    qkv = _silu(qp * dw[0:1] + pq * dw[1:2] + qn * dw[2:3])
    for h in range(N_HEADS):
        sl = slice(h * HEAD_DIM, (h + 1) * HEAD_DIM)
        qh = qkv[:, sl]
        kh = qkv[:, kwid + h * HEAD_DIM: kwid + (h + 1) * HEAD_DIM]
        q_ref[0, :, sl] = qh * (lax.rsqrt(jnp.sum(qh * qh, axis=-1, keepdims=True) + EPS) * HEAD_DIM ** -0.5)
        k_ref[0, :, sl] = kh * lax.rsqrt(jnp.sum(kh * kh, axis=-1, keepdims=True) + EPS)
    v_ref[0] = qkv[:, 2 * kwid:]

    ab = pab_ref[0]
    lane = lax.broadcasted_iota(jnp.int32, ab.shape, 1)
    is_gf = lane < N_HEADS
    is_gb = (lane >= 2 * N_HEADS) & (lane < 3 * N_HEADS)
    xg = ab + dtb_ref[...]
    softplus = jnp.maximum(xg, 0.0) + jnp.log1p(jnp.exp(-jnp.abs(xg)))
    gates = jnp.where(is_gf | is_gb, -jnp.exp(alog_ref[...]) * softplus, _sigmoid(ab))
    r = lax.broadcasted_iota(jnp.int32, (CHUNK, CHUNK), 0)
    c = lax.broadcasted_iota(jnp.int32, (CHUNK, CHUNK), 1)
    tril = (r >= c).astype(BF16)
    triu = (r <= c).astype(BF16)
    g_hi = gates.astype(BF16)
    rem = gates - g_hi.astype(F32)
    g_mid = rem.astype(BF16)
    g_lo = (rem - g_mid.astype(F32)).astype(BF16)
    d = lambda m, x: jnp.dot(m, x, preferred_element_type=F32)
    lane_c = lax.broadcasted_iota(jnp.int32, (CHUNK, ab.shape[1]), 1)
    is_gf_c = lane_c < N_HEADS
    is_gb_c = (lane_c >= 2 * N_HEADS) & (lane_c < 3 * N_HEADS)
    for t in range(ab.shape[0] // CHUNK):
        rs = slice(t * CHUNK, (t + 1) * CHUNK)
        parts = (g_hi[rs], g_mid[rs], g_lo[rs])
        pre = d(tril, parts[0]) + (d(tril, parts[1]) + d(tril, parts[2]))
        suf = d(triu, parts[0]) + (d(triu, parts[1]) + d(triu, parts[2]))
        gb_ref[0, rs, :] = jnp.where(is_gf_c, pre, jnp.where(is_gb_c, suf, gates[rs]))


def _prep(pc, pq, pab, conv_w, dn_conv_w, alog_row, dtb_row, conv_dist, tm):
    bsz, seq, wc3 = pc.shape
    wq3 = pq.shape[-1]
    cwid, kwid = wc3 // 3, wq3 // 3
    n_tiles = seq // tm
    hc = max(conv_dist, 8)
    hq = 8
    tok = lambda b, i: (b, i, 0)
    const2 = lambda b, i: (0, 0)

    def prev_map(hrows):
        return lambda b, i: (b, jnp.maximum(i * (tm // hrows) - 1, 0), 0)

    def next_map(hrows):
        return lambda b, i: (b, jnp.minimum((i + 1) * (tm // hrows), seq // hrows - 1), 0)

    outs = [jax.ShapeDtypeStruct((bsz, seq, w), F32) for w in (cwid, kwid, kwid, kwid, pab.shape[-1])]
    return pl.pallas_call(
        functools.partial(_prep_kernel, conv_dist=conv_dist, n_tiles=n_tiles),
        grid=(bsz, n_tiles),
        in_specs=[pl.BlockSpec((1, tm, wc3), tok),
                  pl.BlockSpec((1, hc, wc3), prev_map(hc)),
                  pl.BlockSpec((1, hc, wc3), next_map(hc)),
                  pl.BlockSpec((1, tm, wq3), tok),
                  pl.BlockSpec((1, hq, wq3), prev_map(hq)),
                  pl.BlockSpec((1, hq, wq3), next_map(hq)),
                  pl.BlockSpec((1, tm, pab.shape[-1]), tok),
                  pl.BlockSpec(conv_w.shape, const2),
                  pl.BlockSpec(dn_conv_w.shape, const2),
                  pl.BlockSpec(alog_row.shape, const2),
                  pl.BlockSpec(dtb_row.shape, const2)],
        out_specs=[pl.BlockSpec((1, tm, o.shape[-1]), tok) for o in outs],
        out_shape=outs,
        compiler_params=_cparams("parallel", "arbitrary"),
        name="mixer_prep",
    )(pc, pc, pc, pq, pq, pq, pab, conv_w, dn_conv_w, alog_row, dtb_row)


def _unit_tri_inverse(a):
    r = lax.broadcasted_iota(jnp.int32, a.shape, 0)
    c = lax.broadcasted_iota(jnp.int32, a.shape, 1)
    p = -a
    t = jnp.where(r == c, 1.0, 0.0) + p
    steps = CHUNK.bit_length() - 2
    for _ in range(steps):
        p = _dot_x3(p, p)
        t = t + _dot_x3(t, p)
    return t


def _dn_kernel(q_ref, k_ref, v_ref, gb_ref, s0_ref, o_ref, sfin_ref, s_ref, *, reverse, has_s0, n_chunks):
    j = pl.program_id(1)

    @pl.when(j == 0)
    def _():
        if has_s0:
            s_ref[...] = s0_ref[0]
        else:
            s_ref[...] = jnp.zeros_like(s_ref)

    gbt = gb_ref[0]
    gbt_t = gbt.T
    r = lax.broadcasted_iota(jnp.int32, (CHUNK, CHUNK), 0)
    c = lax.broadcasted_iota(jnp.int32, (CHUNK, CHUNK), 1)
    if reverse:
        incl, strict, last, gcol, bcol = r <= c, r < c, 0, 2 * N_HEADS, 3 * N_HEADS
    else:
        incl, strict, last, gcol, bcol = r >= c, r > c, CHUNK - 1, 0, N_HEADS
    tdot = lambda a, b: lax.dot_general(a.astype(BF16), b.astype(BF16), (((1,), (1,)), ((), ())),
                                        preferred_element_type=F32)
    for h in range(N_HEADS):
        sl = slice(h * HEAD_DIM, (h + 1) * HEAD_DIM)
        q = q_ref[0, :, sl]
        k = k_ref[0, :, sl]
        v = v_ref[0, :, sl]
        gc = gbt[:, gcol + h:gcol + h + 1]
        gr = gbt_t[gcol + h:gcol + h + 1, :]
        beta = gbt[:, bcol + h:bcol + h + 1]
        decay = jnp.where(incl, jnp.exp(jnp.where(incl, gc - gr, 0.0)), 0.0)
        kb = k * beta
        a = jnp.where(strict, tdot(kb, k) * decay, 0.0)
        t = _unit_tri_inverse(a)
        eg = jnp.exp(gc)
        uw = _bdot(t, jnp.concatenate([v * beta, kb * eg], axis=1))
        u, w = uw[:, :HEAD_DIM], uw[:, HEAD_DIM:]
        qk = jnp.where(incl, tdot(q, k) * decay, 0.0)
        g_last = gc[last:last + 1, :]
        s = s_ref[h]
        v_new = u - _bdot(w, s)
        o_ref[0, :, sl] = _bdot(q * eg, s) + _bdot(qk, v_new)
        kg = k * jnp.exp(g_last - gc)
        s_ref[h] = s * jnp.exp(g_last) + lax.dot_general(
            kg.astype(BF16), v_new.astype(BF16), (((0,), (0,)), ((), ())), preferred_element_type=F32)

    @pl.when(j == n_chunks - 1)
    def _():
        sfin_ref[0] = s_ref[...]


def _delta_scan(q, k, v, gb, s0, reverse):
    bsz, seq, kwid = q.shape
    n_chunks = seq // CHUNK
    has_s0 = s0 is not None
    if not has_s0:
        s0 = jnp.zeros((1, N_HEADS, HEAD_DIM, HEAD_DIM), F32)
    cmap = (lambda b, j: (b, n_chunks - 1 - j, 0)) if reverse else (lambda b, j: (b, j, 0))
    smap = (lambda b, j: (b, 0, 0, 0)) if has_s0 else (lambda b, j: (0, 0, 0, 0))
    sblk = (1, N_HEADS, HEAD_DIM, HEAD_DIM)
    return pl.pallas_call(
        functools.partial(_dn_kernel, reverse=reverse, has_s0=has_s0, n_chunks=n_chunks),
        grid=(bsz, n_chunks),
        in_specs=[pl.BlockSpec((1, CHUNK, kwid), cmap)] * 3
                 + [pl.BlockSpec((1, CHUNK, gb.shape[-1]), cmap),
                    pl.BlockSpec(sblk, smap)],
        out_specs=[pl.BlockSpec((1, CHUNK, kwid), cmap),
                   pl.BlockSpec(sblk, lambda b, j: (b, 0, 0, 0))],
        out_shape=[jax.ShapeDtypeStruct(q.shape, F32),
                   jax.ShapeDtypeStruct((bsz,) + sblk[1:], F32)],
        scratch_shapes=[pltpu.VMEM(sblk[1:], F32)],
        compiler_params=_cparams("parallel", "arbitrary"),
        name="delta_scan_bwd" if reverse else "delta_scan_fwd",
    )(q, k, v, gb, s0)


def _mixout_kernel(x_ref, mod_ref, yc_ref, of_ref, ob_ref, z_ref, gn_ref, wt_ref, wb_ref, o_ref):
    o = of_ref[0] + ob_ref[0]
    gate = _silu(z_ref[0])
    gn = gn_ref[...]
    ys = []
    for h in range(N_HEADS):
        sl = slice(h * HEAD_DIM, (h + 1) * HEAD_DIM)
        ys.append(_rms(o[:, sl]) * gn * gate[:, sl])
    y_dn = jnp.concatenate(ys, axis=1)
    m = _bdot(yc_ref[0], wt_ref[...]) + _bdot(y_dn, wb_ref[...])
    o_ref[0] = x_ref[0] + mod_ref[0, 5:6, :] * m


def _mixout(x, mod, per_batch_mod, yc, o_f, o_b, z, gn, wt, wb, tm):
    bsz, seq, d = x.shape
    half = yc.shape[-1]
    mod_map = (lambda b, i: (b, 0, 0)) if per_batch_mod else (lambda b, i: (0, 0, 0))
    const2 = lambda b, i: (0, 0)
    tok = lambda b, i: (b, i, 0)
    return pl.pallas_call(
        _mixout_kernel,
        grid=(bsz, seq // tm),
        in_specs=[pl.BlockSpec((1, tm, d), tok),
                  pl.BlockSpec((1, N_MOD, d), mod_map)]
                 + [pl.BlockSpec((1, tm, half), tok)] * 4
                 + [pl.BlockSpec((1, HEAD_DIM), const2),
                    pl.BlockSpec((half, d), const2),
                    pl.BlockSpec((half, d), const2)],
        out_specs=pl.BlockSpec((1, tm, d), tok),
        out_shape=jax.ShapeDtypeStruct(x.shape, F32),
        compiler_params=_cparams("parallel", "arbitrary"),
        name="mix_out_proj",
    )(x, mod, yc, o_f, o_b, z, gn, wt, wb)


def _layer(x, mod, per_batch_mod, s0_f, s0_b, on_grid, lw, norm_final, final_norm, tm):
    (g1, wa1, wb1, wo1, gm, wc, wq, wz, wab, conv_w, dn_conv_w, alog_row, dtb_row, gn, wt, wbo,
     g2, wa2, wb2, wo2) = lw
    x = _ffn(x, mod, per_batch_mod, 0, g1, wa1, wb1, wo1, norm_final, False, tm)
    pc, pq, pz, pab = _proj(x, mod, per_batch_mod, gm, wc, wq, wz, wab, tm)
    yc, q, k, v, gb = _prep(pc, pq, pab, conv_w, dn_conv_w, alog_row, dtb_row,
                            GRID_W if on_grid else 1, tm)
    o_f, s_f = _delta_scan(q, k, v, gb, s0_f, False)
    o_b, s_b = _delta_scan(q, k, v, gb, s0_b, True)
    x = _mixout(x, mod, per_batch_mod, yc, o_f, o_b, pz, gn, wt, wbo, tm)
    y = _ffn(x, mod, per_batch_mod, 6, g2, wa2, wb2, wo2, norm_final, final_norm, tm)
    return y, s_f, s_b


def _ffn_weights(w_in, w_out):
    d, two_ff = w_in.shape
    ff = two_ff // 2
    nck = ff // FF_CHUNK
    w = w_in.astype(BF16).reshape(d, 2, nck, FF_CHUNK).transpose(1, 2, 0, 3)
    return w[0], w[1], w_out.astype(BF16).reshape(nck, FF_CHUNK, d)


def kernel(x_prompt, x_sample, state_dn_fwd, state_dn_bwd, c, c_ctx, w_ada, b_ada, norm_ffn1, w_ffn1_in,
           w_ffn1_out, norm_mix, w_mix_in, conv_w, dn_conv_w, dn_a_log, dn_dt_bias, dn_norm, w_mix_out,
           norm_ffn2, w_ffn2_in, w_ffn2_out, norm_final):
    depth = w_ada.shape[0]
    d = x_prompt.shape[-1]
    n_lat = c.shape[0]
    cwid = conv_w.shape[-1]
    kwid = N_HEADS * HEAD_DIM
    n_gate = 4 * N_HEADS
    row = lambda a: a.reshape(1, -1)

    cvec = jnp.concatenate([c_ctx[None, :], c, jnp.zeros((16 - 1 - n_lat, d), F32)], axis=0)

    xp, xs = x_prompt, x_sample
    new_f, new_b = [], []
    for l in range(depth):
        mod = _modulation(cvec, w_ada[l], b_ada[l]).reshape(16, N_MOD, d)
        mod_ctx, mod_lat = mod[0:1], mod[1:1 + n_lat]

        wa1, wb1, wo1 = _ffn_weights(w_ffn1_in[l], w_ffn1_out[l])
        wa2, wb2, wo2 = _ffn_weights(w_ffn2_in[l], w_ffn2_out[l])
        wm = w_mix_in[l].astype(BF16)
        wc = wm[:, :3 * cwid]
        wq = wm[:, 3 * cwid:3 * cwid + 3 * kwid]
        wz = wm[:, 3 * cwid + 3 * kwid:3 * cwid + 4 * kwid]
        wab = jnp.pad(wm[:, 3 * cwid + 4 * kwid:], ((0, 0), (0, HEAD_DIM - n_gate)))
        zpad = jnp.zeros((N_HEADS,), F32)
        alog_row = jnp.concatenate([dn_a_log[l, 0], zpad, dn_a_log[l, 1], zpad,
                                    jnp.zeros((HEAD_DIM - n_gate,), F32)])[None, :]
        dtb_row = jnp.concatenate([dn_dt_bias[l, 0], zpad, dn_dt_bias[l, 1], zpad,
                                   jnp.zeros((HEAD_DIM - n_gate,), F32)])[None, :]
        wo = w_mix_out[l].astype(BF16)
        lw = (row(norm_ffn1[l]), wa1, wb1, wo1, row(norm_mix[l]), wc, wq, wz, wab, conv_w[l], dn_conv_w[l],
              alog_row, dtb_row, row(dn_norm[l]), wo[:cwid], wo[cwid:], row(norm_ffn2[l]), wa2, wb2, wo2)
        last = l == depth - 1
        xp, sf, sb = _layer(xp, mod_ctx, False, None, None, False, lw, row(norm_final), last, 256)
        new_f.append(sf)
        new_b.append(sb)
        xs, _, _ = _layer(xs, mod_lat, True, state_dn_fwd[:, l], state_dn_bwd[:, l], True, lw,
                          row(norm_final), last, 512)
    return xp, xs, jnp.stack(new_f, axis=1), jnp.stack(new_b, axis=1)
```

```python
import functools

import jax
import jax.numpy as jnp
from jax import lax
from jax.experimental import pallas as pl
from jax.experimental.pallas import tpu as pltpu

F32 = jnp.float32
BF16 = jnp.bfloat16

EPS = 1e-6
CHUNK = 64
GRID_W = 64
N_HEADS = 4
HEAD_DIM = 128
N_MOD = 9
FF_CHUNK = 256
VMEM_LIMIT = 56 * 1024 * 1024


def _cparams(*sem):
    return pltpu.CompilerParams(dimension_semantics=sem, vmem_limit_bytes=VMEM_LIMIT)


def _bdot(a, b):
    return jnp.dot(a.astype(BF16), b.astype(BF16), preferred_element_type=F32)


def _split_bf16(a):
    hi = a.astype(BF16)
    lo = (a - hi.astype(F32)).astype(BF16)
    return hi, lo


def _dot_x3(a, b):
    a_hi, a_lo = _split_bf16(a)
    b_hi, b_lo = _split_bf16(b)
    d = lambda x, y: jnp.dot(x, y, preferred_element_type=F32)
    return d(a_hi, b_hi) + (d(a_hi, b_lo) + d(a_lo, b_hi))


def _sigmoid(x):
    return 1.0 / (1.0 + jnp.exp(-x))


def _silu(x):
    return x * _sigmoid(x)


def _rms(x):
    return x * lax.rsqrt(jnp.mean(x * x, axis=-1, keepdims=True) + EPS)


def _mod_kernel(c_ref, w_ref, b_ref, o_ref):
    s = _silu(c_ref[...])
    o_ref[...] = _dot_x3(s, w_ref[...]) + b_ref[...]


def _modulation(cvec, w_ada, b_ada):
    rows, d = cvec.shape
    n = w_ada.shape[1]
    tn = 1536
    return pl.pallas_call(
        _mod_kernel,
        grid=(n // tn,),
        in_specs=[pl.BlockSpec((rows, d), lambda i: (0, 0)),
                  pl.BlockSpec((d, tn), lambda i: (0, i)),
                  pl.BlockSpec((1, tn), lambda i: (0, i))],
        out_specs=pl.BlockSpec((rows, tn), lambda i: (0, i)),
        out_shape=jax.ShapeDtypeStruct((rows, n), F32),
        compiler_params=_cparams("arbitrary"),
        name="modulation",
    )(cvec, w_ada, b_ada.reshape(1, n))


def _ffn_kernel(x_ref, mod_ref, g_ref, wa_ref, wb_ref, wo_ref, gfin_ref, o_ref, h_ref, acc_ref,
                *, mod_row, final_norm):
    x = x_ref[0]
    sh = mod_ref[0, mod_row:mod_row + 1, :]
    sc = mod_ref[0, mod_row + 1:mod_row + 2, :]
    gt = mod_ref[0, mod_row + 2:mod_row + 3, :]
    h_ref[...] = (_rms(x) * g_ref[...] * (1.0 + sc) + sh).astype(BF16)
    acc_ref[...] = jnp.zeros_like(acc_ref)

    def body(c, carry):
        h = h_ref[...]
        a = jnp.dot(h, wa_ref[c], preferred_element_type=F32)
        b = jnp.dot(h, wb_ref[c], preferred_element_type=F32)
        act = (_silu(a) * b).astype(BF16)
        acc_ref[...] += jnp.dot(act, wo_ref[c], preferred_element_type=F32)
        return carry

    lax.fori_loop(0, wa_ref.shape[0], body, 0)
    y = x + 0.5 * gt * acc_ref[...]
    if final_norm:
        y = _rms(y) * gfin_ref[...]
    o_ref[0] = y


def _ffn(x, mod, per_batch_mod, mod_row, g, wa, wb, wo, gfin, final_norm, tm):
    bsz, seq, d = x.shape
    nck, _, ck = wa.shape
    mod_map = (lambda b, i: (b, 0, 0)) if per_batch_mod else (lambda b, i: (0, 0, 0))
    const3 = lambda b, i: (0, 0, 0)
    const2 = lambda b, i: (0, 0)
    return pl.pallas_call(
        functools.partial(_ffn_kernel, mod_row=mod_row, final_norm=final_norm),
        grid=(bsz, seq // tm),
        in_specs=[pl.BlockSpec((1, tm, d), lambda b, i: (b, i, 0)),
                  pl.BlockSpec((1, N_MOD, d), mod_map),
                  pl.BlockSpec((1, d), const2),
                  pl.BlockSpec((nck, d, ck), const3, pipeline_mode=pl.Buffered(1)),
                  pl.BlockSpec((nck, d, ck), const3, pipeline_mode=pl.Buffered(1)),
                  pl.BlockSpec((nck, ck, d), const3, pipeline_mode=pl.Buffered(1)),
                  pl.BlockSpec((1, d), const2)],
        out_specs=pl.BlockSpec((1, tm, d), lambda b, i: (b, i, 0)),
        out_shape=jax.ShapeDtypeStruct(x.shape, F32),
        scratch_shapes=[pltpu.VMEM((tm, d), BF16), pltpu.VMEM((tm, d), F32)],
        compiler_params=_cparams("parallel", "arbitrary"),
        name="ffn_final" if final_norm else "ffn",
    )(x, mod, g, wa, wb, wo, gfin)


def _proj_kernel(x_ref, mod_ref, g_ref, wc_ref, wq_ref, wz_ref, wab_ref, pc_ref, pq_ref, pz_ref, pab_ref):
    x = x_ref[0]
    sh = mod_ref[0, 3:4, :]
    sc = mod_ref[0, 4:5, :]
    h = (_rms(x) * g_ref[...] * (1.0 + sc) + sh).astype(BF16)
    pc_ref[0] = jnp.dot(h, wc_ref[...], preferred_element_type=F32)
    pq_ref[0] = jnp.dot(h, wq_ref[...], preferred_element_type=F32)
    pz_ref[0] = jnp.dot(h, wz_ref[...], preferred_element_type=F32)
    pab_ref[0] = jnp.dot(h, wab_ref[...], preferred_element_type=F32)


def _proj(x, mod, per_batch_mod, g, wc, wq, wz, wab, tm):
    bsz, seq, d = x.shape
    mod_map = (lambda b, i: (b, 0, 0)) if per_batch_mod else (lambda b, i: (0, 0, 0))
    const2 = lambda b, i: (0, 0)
    tok = lambda b, i: (b, i, 0)
    widths = (wc.shape[1], wq.shape[1], wz.shape[1], wab.shape[1])
    return pl.pallas_call(
        _proj_kernel,
        grid=(bsz, seq // tm),
        in_specs=[pl.BlockSpec((1, tm, d), tok),
                  pl.BlockSpec((1, N_MOD, d), mod_map),
                  pl.BlockSpec((1, d), const2)]
                 + [pl.BlockSpec((d, w), const2, pipeline_mode=pl.Buffered(1)) for w in widths],
        out_specs=[pl.BlockSpec((1, tm, w), tok) for w in widths],
        out_shape=[jax.ShapeDtypeStruct((bsz, seq, w), F32) for w in widths],
        compiler_params=_cparams("parallel", "arbitrary"),
        name="mix_in_proj",
    )(x, mod, g, wc, wq, wz, wab)


def _shift_rows(cur, prev_halo, next_halo, dist, first, last):
    tm = cur.shape[0]
    hp = jnp.where(first, 0.0, prev_halo)
    hn = jnp.where(last, 0.0, next_halo)
    if dist % 8 == 0:
        prev = jnp.concatenate([hp[hp.shape[0] - dist:], cur[:tm - dist]], axis=0)
        nxt = jnp.concatenate([cur[dist:], hn[:dist]], axis=0)
        return prev, nxt
    assert dist == 1
    row = lax.broadcasted_iota(jnp.int32, cur.shape, 0)
    prev = jnp.where(row == 0, hp[hp.shape[0] - 1:], pltpu.roll(cur, 1, axis=0))
    nxt = jnp.where(row == tm - 1, hn[:1], pltpu.roll(cur, tm - 1, axis=0))
    return prev, nxt


def _prep_kernel(pc_ref, pcp_ref, pcn_ref, pq_ref, pqp_ref, pqn_ref, pab_ref,
                 cw_ref, dw_ref, alog_ref, dtb_ref,
                 yc_ref, q_ref, k_ref, v_ref, gb_ref, *, conv_dist, n_tiles):
    i = pl.program_id(1)
    first = i == 0
    last = i == n_tiles - 1
    cwid = yc_ref.shape[-1]
    kwid = q_ref.shape[-1]

    pc = pc_ref[0]
    u = pc[:, cwid:2 * cwid] * pc[:, 2 * cwid:]
    pcp = pcp_ref[0]
    pcn = pcn_ref[0]
    up, un = _shift_rows(u, pcp[:, cwid:2 * cwid] * pcp[:, 2 * cwid:],
                         pcn[:, cwid:2 * cwid] * pcn[:, 2 * cwid:], conv_dist, first, last)
    cw = cw_ref[...]
    yc_ref[0] = pc[:, :cwid] * (up * cw[0:1] + u * cw[1:2] + un * cw[2:3])

    pq = pq_ref[0]
    qp, qn = _shift_rows(pq, pqp_ref[0], pqn_ref[0], 1, first, last)
    dw = dw_ref[...]
    qkv = _silu(qp * dw[0:1] + pq * dw[1:2] + qn * dw[2:3])
    for h in range(N_HEADS):
        sl = slice(h * HEAD_DIM, (h + 1) * HEAD_DIM)
        qh = qkv[:, sl]
        kh = qkv[:, kwid + h * HEAD_DIM: kwid + (h + 1) * HEAD_DIM]
        q_ref[0, :, sl] = qh * (lax.rsqrt(jnp.sum(qh * qh, axis=-1, keepdims=True) + EPS) * HEAD_DIM ** -0.5)
        k_ref[0, :, sl] = kh * lax.rsqrt(jnp.sum(kh * kh, axis=-1, keepdims=True) + EPS)
    v_ref[0] = qkv[:, 2 * kwid:]

    ab = pab_ref[0]
    lane = lax.broadcasted_iota(jnp.int32, ab.shape, 1)
    is_gf = lane < N_HEADS
    is_gb = (lane >= 2 * N_HEADS) & (lane < 3 * N_HEADS)
    xg = ab + dtb_ref[...]
    softplus = jnp.maximum(xg, 0.0) + jnp.log1p(jnp.exp(-jnp.abs(xg)))
    gates = jnp.where(is_gf | is_gb, -jnp.exp(alog_ref[...]) * softplus, _sigmoid(ab))
    r = lax.broadcasted_iota(jnp.int32, (CHUNK, CHUNK), 0)
    c = lax.broadcasted_iota(jnp.int32, (CHUNK, CHUNK), 1)
    tril = (r >= c).astype(BF16)
    triu = (r <= c).astype(BF16)
    g_hi = gates.astype(BF16)
    rem = gates - g_hi.astype(F32)
    g_mid = rem.astype(BF16)
    g_lo = (rem - g_mid.astype(F32)).astype(BF16)
    d = lambda m, x: jnp.dot(m, x, preferred_element_type=F32)
    lane_c = lax.broadcasted_iota(jnp.int32, (CHUNK, ab.shape[1]), 1)
    is_gf_c = lane_c < N_HEADS
    is_gb_c = (lane_c >= 2 * N_HEADS) & (lane_c < 3 * N_HEADS)
    for t in range(ab.shape[0] // CHUNK):
        rs = slice(t * CHUNK, (t + 1) * CHUNK)
        parts = (g_hi[rs], g_mid[rs], g_lo[rs])
        pre = d(tril, parts[0]) + (d(tril, parts[1]) + d(tril, parts[2]))
        suf = d(triu, parts[0]) + (d(triu, parts[1]) + d(triu, parts[2]))
        gb_ref[0, rs, :] = jnp.where(is_gf_c, pre, jnp.where(is_gb_c, suf, gates[rs]))


def _prep(pc, pq, pab, conv_w, dn_conv_w, alog_row, dtb_row, conv_dist, tm):
    bsz, seq, wc3 = pc.shape
    wq3 = pq.shape[-1]
    cwid, kwid = wc3 // 3, wq3 // 3
    n_tiles = seq // tm
    hc = max(conv_dist, 8)
    hq = 8
    tok = lambda b, i: (b, i, 0)
    const2 = lambda b, i: (0, 0)

    def prev_map(hrows):
        return lambda b, i: (b, jnp.maximum(i * (tm // hrows) - 1, 0), 0)

    def next_map(hrows):
        return lambda b, i: (b, jnp.minimum((i + 1) * (tm // hrows), seq // hrows - 1), 0)

    outs = [jax.ShapeDtypeStruct((bsz, seq, w), F32) for w in (cwid, kwid, kwid, kwid, pab.shape[-1])]
    return pl.pallas_call(
        functools.partial(_prep_kernel, conv_dist=conv_dist, n_tiles=n_tiles),
        grid=(bsz, n_tiles),
        in_specs=[pl.BlockSpec((1, tm, wc3), tok),
                  pl.BlockSpec((1, hc, wc3), prev_map(hc)),
                  pl.BlockSpec((1, hc, wc3), next_map(hc)),
                  pl.BlockSpec((1, tm, wq3), tok),
                  pl.BlockSpec((1, hq, wq3), prev_map(hq)),
                  pl.BlockSpec((1, hq, wq3), next_map(hq)),
                  pl.BlockSpec((1, tm, pab.shape[-1]), tok),
                  pl.BlockSpec(conv_w.shape, const2),
                  pl.BlockSpec(dn_conv_w.shape, const2),
                  pl.BlockSpec(alog_row.shape, const2),
                  pl.BlockSpec(dtb_row.shape, const2)],
        out_specs=[pl.BlockSpec((1, tm, o.shape[-1]), tok) for o in outs],
        out_shape=outs,
        compiler_params=_cparams("parallel", "arbitrary"),
        name="mixer_prep",
    )(pc, pc, pc, pq, pq, pq, pab, conv_w, dn_conv_w, alog_row, dtb_row)


N_PAIRS = N_HEADS // 2
N_DIRS = 2


def _pair_block_diag(y16, keep_left, keep_right):
    return jnp.concatenate([y16 * keep_left, y16 * keep_right], axis=0)


def _pair_inverse(a, eye_pair, keep_left, keep_right):
    d = lambda x, y: jnp.dot(x, y, preferred_element_type=F32)
    p = -a
    t = eye_pair + p
    n_sq = CHUNK.bit_length() - 2
    for i in range(n_sq + 1):
        p_hi, p_lo = _split_bf16(p)
        r_hi = _pair_block_diag(p_hi, keep_left, keep_right)
        r_lo = _pair_block_diag(p_lo, keep_left, keep_right)
        r_hh = jnp.concatenate([r_hi, r_hi], axis=0)
        if i == 0:
            p = d(jnp.concatenate([p_hi, p_lo], axis=1), r_hh) + d(p_hi, r_lo)
            continue
        t_hi, t_lo = _split_bf16(t)
        if i == n_sq:
            return t + (d(jnp.concatenate([t_hi, t_lo], axis=1), r_hh) + d(t_hi, r_lo))
        l1 = jnp.concatenate([jnp.concatenate([p_hi, p_lo], axis=1),
                              jnp.concatenate([t_hi, t_lo], axis=1)], axis=0)
        l2 = jnp.concatenate([p_hi, t_hi], axis=0)
        res = d(l1, r_hh) + d(l2, r_lo)
        p = res[:CHUNK]
        t = t + res[CHUNK:]


def _delta_kernel(qf_ref, kf_ref, vf_ref, gf_ref, qb_ref, kb_ref, vb_ref, gbk_ref, s0f_ref, s0b_ref,
                  of_ref, ob_ref, sff_ref, sfb_ref,
                  s_ref, u_s, wq_s, kgt_s, qk_s, gl_s, *, n_chunks, has_s0):
    j = pl.program_id(1)
    slot_w = lax.rem(j, 2)
    slot_r = 1 - slot_w

    @pl.when(j == 0)
    def _():
        if has_s0:
            s_ref[0:N_HEADS] = s0f_ref[0]
            s_ref[N_HEADS:] = s0b_ref[0]
        else:
            s_ref[...] = jnp.zeros_like(s_ref)
        u_s[1] = jnp.zeros(u_s.shape[1:], u_s.dtype)
        wq_s[1] = jnp.zeros(wq_s.shape[1:], wq_s.dtype)
        kgt_s[1] = jnp.zeros(kgt_s.shape[1:], kgt_s.dtype)
        qk_s[1] = jnp.zeros(qk_s.shape[1:], qk_s.dtype)
        gl_s[1] = jnp.ones(gl_s.shape[1:], gl_s.dtype)

    dot = lambda x, y: jnp.dot(x, y, preferred_element_type=F32)
    row = lax.broadcasted_iota(jnp.int32, (CHUNK, 2 * CHUNK), 0)
    lane = lax.broadcasted_iota(jnp.int32, (CHUNK, 2 * CHUNK), 1)
    col = lane & (CHUNK - 1)
    left = lane < CHUNK
    keep_left = jnp.where(left, 1.0, 0.0).astype(BF16)
    keep_right = jnp.where(left, 0.0, 1.0).astype(BF16)
    eye_pair = jnp.where(row == col, 1.0, 0.0)
    zeros16 = jnp.zeros((CHUNK, HEAD_DIM), BF16)
    in_refs = ((qf_ref, kf_ref, vf_ref, gf_ref), (qb_ref, kb_ref, vb_ref, gbk_ref))
    out_refs = (of_ref, ob_ref)
    chains = [(d, p) for d in range(N_DIRS) for p in range(N_PAIRS)]

    hds = [(d, h) for d in range(N_DIRS) for h in range(N_HEADS)]
    s_old, ws_qs = [], []
    for d, h in hds:
        hd = d * N_HEADS + h
        s = s_ref[hd]
        s_old.append(s)
        ws_qs.append(dot(wq_s[slot_r, hd], s.astype(BF16)))

    prep = []
    for d, p in chains:
        q_ref, k_ref, v_ref, g_ref = in_refs[d]
        sl = slice(2 * p * HEAD_DIM, 2 * (p + 1) * HEAD_DIM)
        q2, k2, v2 = q_ref[0, :, sl], k_ref[0, :, sl], v_ref[0, :, sl]
        gates = g_ref[0]
        gcol, bcol = 2 * d * N_HEADS, (2 * d + 1) * N_HEADS
        gc = [gates[:, gcol + 2 * p + e:gcol + 2 * p + e + 1] for e in range(2)]
        beta = [gates[:, bcol + 2 * p + e:bcol + 2 * p + e + 1] for e in range(2)]
        gc_pair = jnp.where(left, gc[0], gc[1])
        beta_pair = jnp.where(left, beta[0], beta[1])
        gr_pair = jnp.sum(jnp.where(row == col, gc_pair, 0.0), axis=0, keepdims=True)
        if d == 0:
            incl, strict, last = row >= col, row > col, CHUNK - 1
        else:
            incl, strict, last = row <= col, row < col, 0
        decay = jnp.where(incl, jnp.exp(jnp.where(incl, gc_pair - gr_pair, 0.0)), 0.0)
        k16 = k2.astype(BF16)
        k_bd = jnp.concatenate([jnp.concatenate([k16[:, :HEAD_DIM], zeros16], axis=1),
                                jnp.concatenate([zeros16, k16[:, HEAD_DIM:]], axis=1)], axis=0)
        kq = jnp.concatenate([k16, q2.astype(BF16)], axis=0)
        gram = lax.dot_general(kq, k_bd, (((1,), (1,)), ((), ())), preferred_element_type=F32)
        a = jnp.where(strict, beta_pair * gram[:CHUNK] * decay, 0.0)
        qk_s[slot_w, d * N_PAIRS + p] = jnp.where(incl, gram[CHUNK:] * decay, 0.0).astype(BF16)
        prep.append((q2, k2, v2, gc, beta, last, a))

    v_new = []
    for i, (d, h) in enumerate(hds):
        hd = d * N_HEADS + h
        vn = (u_s[slot_r, hd] - ws_qs[i][:CHUNK]).astype(BF16)
        v_new.append(vn)
        v_pad = jnp.concatenate([vn, zeros16] if h % 2 == 0 else [zeros16, vn], axis=0)
        o = ws_qs[i][CHUNK:] + dot(qk_s[slot_r, d * N_PAIRS + h // 2], v_pad)
        out_refs[d][0, :, h * HEAD_DIM:(h + 1) * HEAD_DIM] = o

    t_inv = [_pair_inverse(pr[-1], eye_pair, keep_left, keep_right) for pr in prep]

    for i, (d, h) in enumerate(hds):
        hd = d * N_HEADS + h
        s_ref[hd] = s_old[i] * gl_s[slot_r, hd:hd + 1, :] + dot(kgt_s[slot_r, hd], v_new[i])

    for (d, p), (q2, k2, v2, gc, beta, last, _), t in zip(chains, prep, t_inv):
        t16 = t.astype(BF16)
        for e in range(2):
            hd = d * N_HEADS + 2 * p + e
            sl = slice(e * HEAD_DIM, (e + 1) * HEAD_DIM)
            eg = jnp.exp(gc[e])
            g_last = gc[e][last:last + 1, :]
            rhs = jnp.concatenate([v2[:, sl] * beta[e], k2[:, sl] * (beta[e] * eg)], axis=1).astype(BF16)
            zpad = jnp.zeros_like(rhs)
            uw = dot(t16, jnp.concatenate([rhs, zpad] if e == 0 else [zpad, rhs], axis=0))
            u_s[slot_w, hd] = uw[:, :HEAD_DIM]
            wq_s[slot_w, hd, 0:CHUNK, :] = uw[:, HEAD_DIM:].astype(BF16)
            wq_s[slot_w, hd, CHUNK:, :] = (q2[:, sl] * eg).astype(BF16)
            kgt_s[slot_w, hd] = (k2[:, sl] * jnp.exp(g_last - gc[e])).T.astype(BF16)
            gl_s[slot_w, hd:hd + 1, :] = jnp.broadcast_to(jnp.exp(g_last), (1, HEAD_DIM))

    @pl.when(j == n_chunks)
    def _():
        sff_ref[0] = s_ref[0:N_HEADS]
        sfb_ref[0] = s_ref[N_HEADS:]


def _delta_scan(q, k, v, gb, s0_f, s0_b):
    bsz, seq, kwid = q.shape
    n = seq // CHUNK
    has_s0 = s0_f is not None
    sblk = (1, N_HEADS, HEAD_DIM, HEAD_DIM)
    if not has_s0:
        s0_f = s0_b = jnp.zeros(sblk, F32)
    fwd_in = lambda b, j: (b, jnp.minimum(j, n - 1), 0)
    bwd_in = lambda b, j: (b, jnp.maximum(n - 1 - j, 0), 0)
    fwd_out = lambda b, j: (b, jnp.maximum(j - 1, 0), 0)
    bwd_out = lambda b, j: (b, jnp.minimum(n - j, n - 1), 0)
    smap = (lambda b, j: (b, 0, 0, 0)) if has_s0 else (lambda b, j: (0, 0, 0, 0))
    gw = gb.shape[-1]
    n_hd = N_DIRS * N_HEADS
    return pl.pallas_call(
        functools.partial(_delta_kernel, n_chunks=n, has_s0=has_s0),
        grid=(bsz, n + 1),
        in_specs=[pl.BlockSpec((1, CHUNK, kwid), fwd_in)] * 3 + [pl.BlockSpec((1, CHUNK, gw), fwd_in)]
                 + [pl.BlockSpec((1, CHUNK, kwid), bwd_in)] * 3 + [pl.BlockSpec((1, CHUNK, gw), bwd_in)]
                 + [pl.BlockSpec(sblk, smap)] * 2,
        out_specs=[pl.BlockSpec((1, CHUNK, kwid), fwd_out),
                   pl.BlockSpec((1, CHUNK, kwid), bwd_out),
                   pl.BlockSpec(sblk, lambda b, j: (b, 0, 0, 0)),
                   pl.BlockSpec(sblk, lambda b, j: (b, 0, 0, 0))],
        out_shape=[jax.ShapeDtypeStruct(q.shape, F32)] * 2
                  + [jax.ShapeDtypeStruct((bsz,) + sblk[1:], F32)] * 2,
        scratch_shapes=[pltpu.VMEM((n_hd, HEAD_DIM, HEAD_DIM), F32),
                        pltpu.VMEM((2, n_hd, CHUNK, HEAD_DIM), F32),
                        pltpu.VMEM((2, n_hd, 2 * CHUNK, HEAD_DIM), BF16),
                        pltpu.VMEM((2, n_hd, HEAD_DIM, CHUNK), BF16),
                        pltpu.VMEM((2, N_DIRS * N_PAIRS, CHUNK, 2 * CHUNK), BF16),
                        pltpu.VMEM((2, n_hd, HEAD_DIM), F32)],
        compiler_params=_cparams("parallel", "arbitrary"),
        name="delta_scan",
    )(q, k, v, gb, q, k, v, gb, s0_f, s0_b)


def _mixout_kernel(x_ref, mod_ref, yc_ref, of_ref, ob_ref, z_ref, gn_ref, wt_ref, wb_ref, o_ref):
    o = of_ref[0] + ob_ref[0]
    gate = _silu(z_ref[0])
    gn = gn_ref[...]
    ys = []
    for h in range(N_HEADS):
        sl = slice(h * HEAD_DIM, (h + 1) * HEAD_DIM)
        ys.append(_rms(o[:, sl]) * gn * gate[:, sl])
    y_dn = jnp.concatenate(ys, axis=1)
    m = _bdot(yc_ref[0], wt_ref[...]) + _bdot(y_dn, wb_ref[...])
    o_ref[0] = x_ref[0] + mod_ref[0, 5:6, :] * m


def _mixout(x, mod, per_batch_mod, yc, o_f, o_b, z, gn, wt, wb, tm):
    bsz, seq, d = x.shape
    half = yc.shape[-1]
    mod_map = (lambda b, i: (b, 0, 0)) if per_batch_mod else (lambda b, i: (0, 0, 0))
    const2 = lambda b, i: (0, 0)
    tok = lambda b, i: (b, i, 0)
    return pl.pallas_call(
        _mixout_kernel,
        grid=(bsz, seq // tm),
        in_specs=[pl.BlockSpec((1, tm, d), tok),
                  pl.BlockSpec((1, N_MOD, d), mod_map)]
                 + [pl.BlockSpec((1, tm, half), tok)] * 4
                 + [pl.BlockSpec((1, HEAD_DIM), const2),
                    pl.BlockSpec((half, d), const2),
                    pl.BlockSpec((half, d), const2)],
        out_specs=pl.BlockSpec((1, tm, d), tok),
        out_shape=jax.ShapeDtypeStruct(x.shape, F32),
        compiler_params=_cparams("parallel", "arbitrary"),
        name="mix_out_proj",
    )(x, mod, yc, o_f, o_b, z, gn, wt, wb)


def _layer(x, mod, per_batch_mod, s0_f, s0_b, on_grid, lw, norm_final, final_norm, tm):
    (g1, wa1, wb1, wo1, gm, wc, wq, wz, wab, conv_w, dn_conv_w, alog_row, dtb_row, gn, wt, wbo,
     g2, wa2, wb2, wo2) = lw
    x = _ffn(x, mod, per_batch_mod, 0, g1, wa1, wb1, wo1, norm_final, False, tm)
    pc, pq, pz, pab = _proj(x, mod, per_batch_mod, gm, wc, wq, wz, wab, tm)
    yc, q, k, v, gb = _prep(pc, pq, pab, conv_w, dn_conv_w, alog_row, dtb_row,
                            GRID_W if on_grid else 1, tm)
    o_f, o_b, s_f, s_b = _delta_scan(q, k, v, gb, s0_f, s0_b)
    x = _mixout(x, mod, per_batch_mod, yc, o_f, o_b, pz, gn, wt, wbo, tm)
    y = _ffn(x, mod, per_batch_mod, 6, g2, wa2, wb2, wo2, norm_final, final_norm, tm)
    return y, s_f, s_b


def _ffn_weights(w_in, w_out):
    d, two_ff = w_in.shape
    ff = two_ff // 2
    nck = ff // FF_CHUNK
    w = w_in.astype(BF16).reshape(d, 2, nck, FF_CHUNK).transpose(1, 2, 0, 3)
    return w[0], w[1], w_out.astype(BF16).reshape(nck, FF_CHUNK, d)


def kernel(x_prompt, x_sample, state_dn_fwd, state_dn_bwd, c, c_ctx, w_ada, b_ada, norm_ffn1, w_ffn1_in,
           w_ffn1_out, norm_mix, w_mix_in, conv_w, dn_conv_w, dn_a_log, dn_dt_bias, dn_norm, w_mix_out,
           norm_ffn2, w_ffn2_in, w_ffn2_out, norm_final):
    depth = w_ada.shape[0]
    d = x_prompt.shape[-1]
    n_lat = c.shape[0]
    cwid = conv_w.shape[-1]
    kwid = N_HEADS * HEAD_DIM
    n_gate = 4 * N_HEADS
    row = lambda a: a.reshape(1, -1)

    cvec = jnp.concatenate([c_ctx[None, :], c, jnp.zeros((16 - 1 - n_lat, d), F32)], axis=0)

    xp, xs = x_prompt, x_sample
    new_f, new_b = [], []
    for l in range(depth):
        mod = _modulation(cvec, w_ada[l], b_ada[l]).reshape(16, N_MOD, d)
        mod_ctx, mod_lat = mod[0:1], mod[1:1 + n_lat]

        wa1, wb1, wo1 = _ffn_weights(w_ffn1_in[l], w_ffn1_out[l])
        wa2, wb2, wo2 = _ffn_weights(w_ffn2_in[l], w_ffn2_out[l])
        wm = w_mix_in[l].astype(BF16)
        wc = wm[:, :3 * cwid]
        wq = wm[:, 3 * cwid:3 * cwid + 3 * kwid]
        wz = wm[:, 3 * cwid + 3 * kwid:3 * cwid + 4 * kwid]
        wab = jnp.pad(wm[:, 3 * cwid + 4 * kwid:], ((0, 0), (0, HEAD_DIM - n_gate)))
        zpad = jnp.zeros((N_HEADS,), F32)
        alog_row = jnp.concatenate([dn_a_log[l, 0], zpad, dn_a_log[l, 1], zpad,
                                    jnp.zeros((HEAD_DIM - n_gate,), F32)])[None, :]
        dtb_row = jnp.concatenate([dn_dt_bias[l, 0], zpad, dn_dt_bias[l, 1], zpad,
                                   jnp.zeros((HEAD_DIM - n_gate,), F32)])[None, :]
        wo = w_mix_out[l].astype(BF16)
        lw = (row(norm_ffn1[l]), wa1, wb1, wo1, row(norm_mix[l]), wc, wq, wz, wab, conv_w[l], dn_conv_w[l],
              alog_row, dtb_row, row(dn_norm[l]), wo[:cwid], wo[cwid:], row(norm_ffn2[l]), wa2, wb2, wo2)
        last = l == depth - 1
        xp, sf, sb = _layer(xp, mod_ctx, False, None, None, False, lw, row(norm_final), last, 256)
        new_f.append(sf)
        new_b.append(sb)
        xs, _, _ = _layer(xs, mod_lat, True, state_dn_fwd[:, l], state_dn_bwd[:, l], True, lw,
                          row(norm_final), last, 512)
    return xp, xs, jnp.stack(new_f, axis=1), jnp.stack(new_b, axis=1)
```

```python
import functools

import jax
import jax.numpy as jnp
from jax import lax
from jax.experimental import pallas as pl
from jax.experimental.pallas import tpu as pltpu

F32 = jnp.float32
BF16 = jnp.bfloat16

EPS = 1e-6
CHUNK = 64
GRID_W = 64
N_HEADS = 4
HEAD_DIM = 128
N_PAIRS = N_HEADS // 2
N_DIRS = 2
N_MOD = 9
FF_CHUNK = 256
PREP_CHUNKS = 2
VMEM_LIMIT = 56 * 1024 * 1024


def _cparams(*sem):
    return pltpu.CompilerParams(dimension_semantics=sem, vmem_limit_bytes=VMEM_LIMIT)


def _bdot(a, b):
    return jnp.dot(a.astype(BF16), b.astype(BF16), preferred_element_type=F32)


def _split_bf16(a):
    hi = a.astype(BF16)
    lo = (a - hi.astype(F32)).astype(BF16)
    return hi, lo


def _dot_x3(a, b):
    a_hi, a_lo = _split_bf16(a)
    b_hi, b_lo = _split_bf16(b)
    d = lambda x, y: jnp.dot(x, y, preferred_element_type=F32)
    return d(a_hi, b_hi) + (d(a_hi, b_lo) + d(a_lo, b_hi))


def _sigmoid(x):
    return 1.0 / (1.0 + jnp.exp(-x))


def _silu(x):
    return x * _sigmoid(x)


def _rms(x):
    return x * lax.rsqrt(jnp.mean(x * x, axis=-1, keepdims=True) + EPS)


def _mod_kernel(c_ref, w_ref, b_ref, o_ref):
    s = _silu(c_ref[...])
    o_ref[...] = _dot_x3(s, w_ref[...]) + b_ref[...]


def _modulation(cvec, w_ada, b_ada):
    rows, d = cvec.shape
    n = w_ada.shape[1]
    tn = 1536
    return pl.pallas_call(
        _mod_kernel,
        grid=(n // tn,),
        in_specs=[pl.BlockSpec((rows, d), lambda i: (0, 0)),
                  pl.BlockSpec((d, tn), lambda i: (0, i)),
                  pl.BlockSpec((1, tn), lambda i: (0, i))],
        out_specs=pl.BlockSpec((rows, tn), lambda i: (0, i)),
        out_shape=jax.ShapeDtypeStruct((rows, n), F32),
        compiler_params=_cparams("arbitrary"),
        name="modulation",
    )(cvec, w_ada, b_ada.reshape(1, n))


def _ffn_kernel(x_ref, mod_ref, g_ref, wa_ref, wb_ref, wo_ref, gfin_ref, o_ref, h_ref, acc_ref,
                *, mod_row, final_norm):
    x = x_ref[0]
    sh = mod_ref[0, mod_row:mod_row + 1, :]
    sc = mod_ref[0, mod_row + 1:mod_row + 2, :]
    gt = mod_ref[0, mod_row + 2:mod_row + 3, :]
    h_ref[...] = (_rms(x) * g_ref[...] * (1.0 + sc) + sh).astype(BF16)
    acc_ref[...] = jnp.zeros_like(acc_ref)

    def body(c, carry):
        h = h_ref[...]
        a = jnp.dot(h, wa_ref[c], preferred_element_type=F32)
        b = jnp.dot(h, wb_ref[c], preferred_element_type=F32)
        act = (_silu(a) * b).astype(BF16)
        acc_ref[...] += jnp.dot(act, wo_ref[c], preferred_element_type=F32)
        return carry

    lax.fori_loop(0, wa_ref.shape[0], body, 0)
    y = x + 0.5 * gt * acc_ref[...]
    if final_norm:
        y = _rms(y) * gfin_ref[...]
    o_ref[0] = y


def _ffn(x, mod, per_batch_mod, mod_row, g, wa, wb, wo, gfin, final_norm, tm):
    bsz, seq, d = x.shape
    nck, _, ck = wa.shape
    mod_map = (lambda b, i: (b, 0, 0)) if per_batch_mod else (lambda b, i: (0, 0, 0))
    const3 = lambda b, i: (0, 0, 0)
    const2 = lambda b, i: (0, 0)
    return pl.pallas_call(
        functools.partial(_ffn_kernel, mod_row=mod_row, final_norm=final_norm),
        grid=(bsz, seq // tm),
        in_specs=[pl.BlockSpec((1, tm, d), lambda b, i: (b, i, 0)),
                  pl.BlockSpec((1, N_MOD, d), mod_map),
                  pl.BlockSpec((1, d), const2),
                  pl.BlockSpec((nck, d, ck), const3, pipeline_mode=pl.Buffered(1)),
                  pl.BlockSpec((nck, d, ck), const3, pipeline_mode=pl.Buffered(1)),
                  pl.BlockSpec((nck, ck, d), const3, pipeline_mode=pl.Buffered(1)),
                  pl.BlockSpec((1, d), const2)],
        out_specs=pl.BlockSpec((1, tm, d), lambda b, i: (b, i, 0)),
        out_shape=jax.ShapeDtypeStruct(x.shape, F32),
        scratch_shapes=[pltpu.VMEM((tm, d), BF16), pltpu.VMEM((tm, d), F32)],
        compiler_params=_cparams("parallel", "arbitrary"),
        name="ffn_final" if final_norm else "ffn",
    )(x, mod, g, wa, wb, wo, gfin)


def _proj_kernel(x_ref, mod_ref, g_ref, wc_ref, wq_ref, wz_ref, wab_ref, pc_ref, pq_ref, pz_ref, pab_ref):
    x = x_ref[0]
    sh = mod_ref[0, 3:4, :]
    sc = mod_ref[0, 4:5, :]
    h = (_rms(x) * g_ref[...] * (1.0 + sc) + sh).astype(BF16)
    pc_ref[0] = jnp.dot(h, wc_ref[...], preferred_element_type=F32)
    pq_ref[0] = jnp.dot(h, wq_ref[...], preferred_element_type=F32)
    pz_ref[0] = jnp.dot(h, wz_ref[...], preferred_element_type=F32)
    pab_ref[0] = jnp.dot(h, wab_ref[...], preferred_element_type=F32)


def _proj(x, mod, per_batch_mod, g, wc, wq, wz, wab, tm):
    bsz, seq, d = x.shape
    mod_map = (lambda b, i: (b, 0, 0)) if per_batch_mod else (lambda b, i: (0, 0, 0))
    const2 = lambda b, i: (0, 0)
    tok = lambda b, i: (b, i, 0)
    widths = (wc.shape[1], wq.shape[1], wz.shape[1], wab.shape[1])
    return pl.pallas_call(
        _proj_kernel,
        grid=(bsz, seq // tm),
        in_specs=[pl.BlockSpec((1, tm, d), tok),
                  pl.BlockSpec((1, N_MOD, d), mod_map),
                  pl.BlockSpec((1, d), const2)]
                 + [pl.BlockSpec((d, w), const2, pipeline_mode=pl.Buffered(1)) for w in widths],
        out_specs=[pl.BlockSpec((1, tm, w), tok) for w in widths],
        out_shape=[jax.ShapeDtypeStruct((bsz, seq, w), F32) for w in widths],
        compiler_params=_cparams("parallel", "arbitrary"),
        name="mix_in_proj",
    )(x, mod, g, wc, wq, wz, wab)


def _shift_rows(cur, prev_halo, next_halo, dist, first, last):
    tm = cur.shape[0]
    hp = jnp.where(first, 0.0, prev_halo)
    hn = jnp.where(last, 0.0, next_halo)
    if dist % 8 == 0:
        prev = jnp.concatenate([hp[hp.shape[0] - dist:], cur[:tm - dist]], axis=0)
        nxt = jnp.concatenate([cur[dist:], hn[:dist]], axis=0)
        return prev, nxt
    assert dist == 1
    row = lax.broadcasted_iota(jnp.int32, cur.shape, 0)
    prev = jnp.where(row == 0, hp[hp.shape[0] - 1:], pltpu.roll(cur, 1, axis=0))
    nxt = jnp.where(row == tm - 1, hn[:1], pltpu.roll(cur, tm - 1, axis=0))
    return prev, nxt


def _prep_kernel(pc_ref, pcp_ref, pcn_ref, pq_ref, pqp_ref, pqn_ref, pab_ref,
                 cw_ref, dw_ref, alog_ref, dtb_ref,
                 yc_ref, q_ref, k_ref, v_ref, gb_ref, *, conv_dist, n_tiles):
    i = pl.program_id(1)
    first = i == 0
    last = i == n_tiles - 1
    cwid = yc_ref.shape[-1]
    kwid = q_ref.shape[-1]

    pc = pc_ref[0]
    u = pc[:, cwid:2 * cwid] * pc[:, 2 * cwid:]
    pcp = pcp_ref[0]
    pcn = pcn_ref[0]
    up, un = _shift_rows(u, pcp[:, cwid:2 * cwid] * pcp[:, 2 * cwid:],
                         pcn[:, cwid:2 * cwid] * pcn[:, 2 * cwid:], conv_dist, first, last)
    cw = cw_ref[...]
    yc_ref[0] = pc[:, :cwid] * (up * cw[0:1] + u * cw[1:2] + un * cw[2:3])

    pq = pq_ref[0]
    qp, qn = _shift_rows(pq, pqp_ref[0], pqn_ref[0], 1, first, last)
    dw = dw_ref[...]
    qkv = _silu(qp * dw[0:1] + pq * dw[1:2] + qn * dw[2:3])
    for h in range(N_HEADS):
        sl = slice(h * HEAD_DIM, (h + 1) * HEAD_DIM)
        qh = qkv[:, sl]
        kh = qkv[:, kwid + h * HEAD_DIM: kwid + (h + 1) * HEAD_DIM]
        q_ref[0, :, sl] = qh * (lax.rsqrt(jnp.sum(qh * qh, axis=-1, keepdims=True) + EPS) * HEAD_DIM ** -0.5)
        k_ref[0, :, sl] = kh * lax.rsqrt(jnp.sum(kh * kh, axis=-1, keepdims=True) + EPS)
    v_ref[0] = qkv[:, 2 * kwid:]

    ab = pab_ref[0]
    lane = lax.broadcasted_iota(jnp.int32, ab.shape, 1)
    is_gf = lane < N_HEADS
    is_gb = (lane >= 2 * N_HEADS) & (lane < 3 * N_HEADS)
    xg = ab + dtb_ref[...]
    softplus = jnp.maximum(xg, 0.0) + jnp.log1p(jnp.exp(-jnp.abs(xg)))
    gates = jnp.where(is_gf | is_gb, -jnp.exp(alog_ref[...]) * softplus, _sigmoid(ab))
    r = lax.broadcasted_iota(jnp.int32, (CHUNK, CHUNK), 0)
    c = lax.broadcasted_iota(jnp.int32, (CHUNK, CHUNK), 1)
    tril = (r >= c).astype(BF16)
    triu = (r <= c).astype(BF16)
    g_hi = gates.astype(BF16)
    rem = gates - g_hi.astype(F32)
    g_mid = rem.astype(BF16)
    g_lo = (rem - g_mid.astype(F32)).astype(BF16)
    d = lambda m, x: jnp.dot(m, x, preferred_element_type=F32)
    lane_c = lax.broadcasted_iota(jnp.int32, (CHUNK, ab.shape[1]), 1)
    is_gf_c = lane_c < N_HEADS
    is_gb_c = (lane_c >= 2 * N_HEADS) & (lane_c < 3 * N_HEADS)
    for t in range(ab.shape[0] // CHUNK):
        rs = slice(t * CHUNK, (t + 1) * CHUNK)
        parts = (g_hi[rs], g_mid[rs], g_lo[rs])
        pre = d(tril, parts[0]) + (d(tril, parts[1]) + d(tril, parts[2]))
        suf = d(triu, parts[0]) + (d(triu, parts[1]) + d(triu, parts[2]))
        gb_ref[0, rs, :] = jnp.where(is_gf_c, pre, jnp.where(is_gb_c, suf, gates[rs]))


def _prep(pc, pq, pab, conv_w, dn_conv_w, alog_row, dtb_row, conv_dist, tm):
    bsz, seq, wc3 = pc.shape
    wq3 = pq.shape[-1]
    cwid, kwid = wc3 // 3, wq3 // 3
    n_tiles = seq // tm
    hc = max(conv_dist, 8)
    hq = 8
    tok = lambda b, i: (b, i, 0)
    const2 = lambda b, i: (0, 0)

    def prev_map(hrows):
        return lambda b, i: (b, jnp.maximum(i * (tm // hrows) - 1, 0), 0)

    def next_map(hrows):
        return lambda b, i: (b, jnp.minimum((i + 1) * (tm // hrows), seq // hrows - 1), 0)

    outs = [jax.ShapeDtypeStruct((bsz, seq, w), F32) for w in (cwid, kwid, kwid, kwid, pab.shape[-1])]
    return pl.pallas_call(
        functools.partial(_prep_kernel, conv_dist=conv_dist, n_tiles=n_tiles),
        grid=(bsz, n_tiles),
        in_specs=[pl.BlockSpec((1, tm, wc3), tok),
                  pl.BlockSpec((1, hc, wc3), prev_map(hc)),
                  pl.BlockSpec((1, hc, wc3), next_map(hc)),
                  pl.BlockSpec((1, tm, wq3), tok),
                  pl.BlockSpec((1, hq, wq3), prev_map(hq)),
                  pl.BlockSpec((1, hq, wq3), next_map(hq)),
                  pl.BlockSpec((1, tm, pab.shape[-1]), tok),
                  pl.BlockSpec(conv_w.shape, const2),
                  pl.BlockSpec(dn_conv_w.shape, const2),
                  pl.BlockSpec(alog_row.shape, const2),
                  pl.BlockSpec(dtb_row.shape, const2)],
        out_specs=[pl.BlockSpec((1, tm, o.shape[-1]), tok) for o in outs],
        out_shape=outs,
        compiler_params=_cparams("parallel", "arbitrary"),
        name="mixer_prep",
    )(pc, pc, pc, pq, pq, pq, pab, conv_w, dn_conv_w, alog_row, dtb_row)


def _pair_block_diag(y16, keep_left, keep_right):
    return jnp.concatenate([y16 * keep_left, y16 * keep_right], axis=0)


def _pair_inverses(a_list, eye_pair, keep_left, keep_right):
    d = lambda x, y: jnp.dot(x, y, preferred_element_type=F32)
    ps = [-a for a in a_list]
    ts = [eye_pair + p for p in ps]
    n_sq = CHUNK.bit_length() - 2
    for i in range(n_sq + 1):
        new_ps, new_ts = [], []
        for p, t in zip(ps, ts):
            p_hi, p_lo = _split_bf16(p)
            r_hi = _pair_block_diag(p_hi, keep_left, keep_right)
            r_lo = _pair_block_diag(p_lo, keep_left, keep_right)
            r_hh = jnp.concatenate([r_hi, r_hi], axis=0)
            if i == 0:
                new_ps.append(d(jnp.concatenate([p_hi, p_lo], axis=1), r_hh) + d(p_hi, r_lo))
                new_ts.append(t)
                continue
            t_hi, t_lo = _split_bf16(t)
            if i == n_sq:
                new_ps.append(p)
                new_ts.append(t + (d(jnp.concatenate([t_hi, t_lo], axis=1), r_hh) + d(t_hi, r_lo)))
                continue
            l1 = jnp.concatenate([jnp.concatenate([p_hi, p_lo], axis=1),
                                  jnp.concatenate([t_hi, t_lo], axis=1)], axis=0)
            l2 = jnp.concatenate([p_hi, t_hi], axis=0)
            res = d(l1, r_hh) + d(l2, r_lo)
            new_ps.append(res[:CHUNK])
            new_ts.append(t + res[CHUNK:])
        ps, ts = new_ps, new_ts
    return ts


def _chunk_prep_kernel(q_ref, k_ref, v_ref, g_ref,
                       uf_ref, ub_ref, wf_ref, wb_ref, qgf_ref, qgb_ref, kgf_ref, kgb_ref,
                       qkf_ref, qkb_ref, glf_ref, glb_ref):
    u_refs, w_refs, qg_refs = (uf_ref, ub_ref), (wf_ref, wb_ref), (qgf_ref, qgb_ref)
    kg_refs, qk_refs, gl_refs = (kgf_ref, kgb_ref), (qkf_ref, qkb_ref), (glf_ref, glb_ref)
    row = lax.broadcasted_iota(jnp.int32, (CHUNK, 2 * CHUNK), 0)
    lane = lax.broadcasted_iota(jnp.int32, (CHUNK, 2 * CHUNK), 1)
    col = lane & (CHUNK - 1)
    left = lane < CHUNK
    keep_left = jnp.where(left, 1.0, 0.0).astype(BF16)
    keep_right = jnp.where(left, 0.0, 1.0).astype(BF16)
    eye_pair = jnp.where(row == col, 1.0, 0.0)
    zeros16 = jnp.zeros((CHUNK, HEAD_DIM), BF16)
    dot = lambda x, y: jnp.dot(x, y, preferred_element_type=F32)

    chains, a_list = [], []
    for c in range(PREP_CHUNKS):
        rs = slice(c * CHUNK, (c + 1) * CHUNK)
        gates = g_ref[0, rs, :]
        for p in range(N_PAIRS):
            sl = slice(2 * p * HEAD_DIM, 2 * (p + 1) * HEAD_DIM)
            q2, k2, v2 = q_ref[0, rs, sl], k_ref[0, rs, sl], v_ref[0, rs, sl]
            k16 = k2.astype(BF16)
            k_bd = jnp.concatenate([jnp.concatenate([k16[:, :HEAD_DIM], zeros16], axis=1),
                                    jnp.concatenate([zeros16, k16[:, HEAD_DIM:]], axis=1)], axis=0)
            kq = jnp.concatenate([k16, q2.astype(BF16)], axis=0)
            gram = lax.dot_general(kq, k_bd, (((1,), (1,)), ((), ())), preferred_element_type=F32)
            for d in range(N_DIRS):
                gcol, bcol = 2 * d * N_HEADS, (2 * d + 1) * N_HEADS
                gc = [gates[:, gcol + 2 * p + e:gcol + 2 * p + e + 1] for e in range(2)]
                beta = [gates[:, bcol + 2 * p + e:bcol + 2 * p + e + 1] for e in range(2)]
                gc_pair = jnp.where(left, gc[0], gc[1])
                beta_pair = jnp.where(left, beta[0], beta[1])
                gr_pair = jnp.sum(jnp.where(row == col, gc_pair, 0.0), axis=0, keepdims=True)
                if d == 0:
                    incl, strict, last = row >= col, row > col, CHUNK - 1
                else:
                    incl, strict, last = row <= col, row < col, 0
                decay = jnp.where(incl, jnp.exp(jnp.where(incl, gc_pair - gr_pair, 0.0)), 0.0)
                a_list.append(jnp.where(strict, beta_pair * gram[:CHUNK] * decay, 0.0))
                qk_refs[d][0, rs, p * HEAD_DIM:(p + 1) * HEAD_DIM] = (
                    jnp.where(incl, gram[CHUNK:] * decay, 0.0).astype(BF16))
                chains.append((c, rs, p, d, q2, k2, v2, gc, beta, last))

    t_list = _pair_inverses(a_list, eye_pair, keep_left, keep_right)

    for (c, rs, p, d, q2, k2, v2, gc, beta, last), t in zip(chains, t_list):
        t16 = t.astype(BF16)
        for e in range(2):
            h = 2 * p + e
            sl = slice(e * HEAD_DIM, (e + 1) * HEAD_DIM)
            hs = slice(h * HEAD_DIM, (h + 1) * HEAD_DIM)
            eg = jnp.exp(gc[e])
            g_last = gc[e][last:last + 1, :]
            rhs = jnp.concatenate([v2[:, sl] * beta[e], k2[:, sl] * (beta[e] * eg)], axis=1).astype(BF16)
            zpad = jnp.zeros_like(rhs)
            uw = dot(t16, jnp.concatenate([rhs, zpad] if e == 0 else [zpad, rhs], axis=0))
            u_refs[d][0, rs, hs] = uw[:, :HEAD_DIM]
            w_refs[d][0, rs, hs] = uw[:, HEAD_DIM:].astype(BF16)
            qg_refs[d][0, rs, hs] = (q2[:, sl] * eg).astype(BF16)
            kg_refs[d][0, rs, hs] = (k2[:, sl] * jnp.exp(g_last - gc[e])).astype(BF16)
            gl_refs[d][0, c, h:h + 1, :] = jnp.broadcast_to(jnp.exp(g_last), (1, HEAD_DIM))


def _chunk_prep(q, k, v, gb):
    bsz, seq, kwid = q.shape
    rows = PREP_CHUNKS * CHUNK
    n = seq // CHUNK
    tok = lambda b, i: (b, i, 0)
    f32_full = jax.ShapeDtypeStruct(q.shape, F32)
    bf_full = jax.ShapeDtypeStruct(q.shape, BF16)
    bf_half = jax.ShapeDtypeStruct((bsz, seq, kwid // 2), BF16)
    gl = jax.ShapeDtypeStruct((bsz, n, N_HEADS, HEAD_DIM), F32)
    outs = [f32_full] * 2 + [bf_full] * 6 + [bf_half] * 2 + [gl] * 2
    out_specs = ([pl.BlockSpec((1, rows, kwid), tok)] * 8 + [pl.BlockSpec((1, rows, kwid // 2), tok)] * 2
                 + [pl.BlockSpec((1, PREP_CHUNKS, N_HEADS, HEAD_DIM), lambda b, i: (b, i, 0, 0))] * 2)
    return pl.pallas_call(
        _chunk_prep_kernel,
        grid=(bsz, seq // rows),
        in_specs=[pl.BlockSpec((1, rows, kwid), tok)] * 3 + [pl.BlockSpec((1, rows, gb.shape[-1]), tok)],
        out_specs=out_specs,
        out_shape=outs,
        compiler_params=_cparams("parallel", "parallel"),
        name="delta_chunk_prep",
    )(q, k, v, gb)


def _scan_kernel(uf_ref, wf_ref, qgf_ref, kgf_ref, qkf_ref, glf_ref,
                 ub_ref, wb_ref, qgb_ref, kgb_ref, qkb_ref, glb_ref, s0f_ref, s0b_ref,
                 of_ref, ob_ref, sff_ref, sfb_ref, s_ref, *, n_chunks, has_s0):
    j = pl.program_id(1)

    @pl.when(j == 0)
    def _():
        if has_s0:
            s_ref[0:N_HEADS] = s0f_ref[0]
            s_ref[N_HEADS:] = s0b_ref[0]
        else:
            s_ref[...] = jnp.zeros_like(s_ref)

    dot = lambda x, y: jnp.dot(x, y, preferred_element_type=F32)
    zeros16 = jnp.zeros((CHUNK, HEAD_DIM), BF16)
    ins = ((uf_ref, wf_ref, qgf_ref, kgf_ref, qkf_ref, glf_ref), (ub_ref, wb_ref, qgb_ref, kgb_ref, qkb_ref, glb_ref))
    outs = (of_ref, ob_ref)
    hds = [(d, h) for d in range(N_DIRS) for h in range(N_HEADS)]
    hsl = lambda h: slice(h * HEAD_DIM, (h + 1) * HEAD_DIM)

    s_old, ws_qs = [], []
    for d, h in hds:
        s = s_ref[d * N_HEADS + h]
        s_old.append(s)
        lhs = jnp.concatenate([ins[d][1][0, :, hsl(h)], ins[d][2][0, :, hsl(h)]], axis=0)
        ws_qs.append(dot(lhs, s.astype(BF16)))
    v_new = []
    for i, (d, h) in enumerate(hds):
        vn = (ins[d][0][0, :, hsl(h)] - ws_qs[i][:CHUNK]).astype(BF16)
        v_new.append(vn)
        v_pad = jnp.concatenate([vn, zeros16] if h % 2 == 0 else [zeros16, vn], axis=0)
        qk_pair = ins[d][4][0, :, hsl(h // 2)]
        outs[d][0, :, hsl(h)] = ws_qs[i][CHUNK:] + dot(qk_pair, v_pad)
    for i, (d, h) in enumerate(hds):
        kg = ins[d][3][0, :, hsl(h)]
        upd = lax.dot_general(kg, v_new[i], (((0,), (0,)), ((), ())), preferred_element_type=F32)
        s_ref[d * N_HEADS + h] = s_old[i] * ins[d][5][0, 0, h:h + 1, :] + upd

    @pl.when(j == n_chunks - 1)
    def _():
        sff_ref[0] = s_ref[0:N_HEADS]
        sfb_ref[0] = s_ref[N_HEADS:]


def _scan(prep, s0_f, s0_b):
    (u_f, u_b, w_f, w_b, qg_f, qg_b, kg_f, kg_b, qk_f, qk_b, gl_f, gl_b) = prep
    bsz, seq, kwid = u_f.shape
    n = seq // CHUNK
    has_s0 = s0_f is not None
    sblk = (1, N_HEADS, HEAD_DIM, HEAD_DIM)
    if not has_s0:
        s0_f = s0_b = jnp.zeros(sblk, F32)
    fwd = lambda b, j: (b, j, 0)
    bwd = lambda b, j: (b, n - 1 - j, 0)
    fwd4 = lambda b, j: (b, j, 0, 0)
    bwd4 = lambda b, j: (b, n - 1 - j, 0, 0)
    smap = (lambda b, j: (b, 0, 0, 0)) if has_s0 else (lambda b, j: (0, 0, 0, 0))

    def dir_specs(tok, tok4):
        return ([pl.BlockSpec((1, CHUNK, kwid), tok)] * 4 + [pl.BlockSpec((1, CHUNK, kwid // 2), tok),
                pl.BlockSpec((1, 1, N_HEADS, HEAD_DIM), tok4)])

    return pl.pallas_call(
        functools.partial(_scan_kernel, n_chunks=n, has_s0=has_s0),
        grid=(bsz, n),
        in_specs=dir_specs(fwd, fwd4) + dir_specs(bwd, bwd4) + [pl.BlockSpec(sblk, smap)] * 2,
        out_specs=[pl.BlockSpec((1, CHUNK, kwid), fwd),
                   pl.BlockSpec((1, CHUNK, kwid), bwd),
                   pl.BlockSpec(sblk, lambda b, j: (b, 0, 0, 0)),
                   pl.BlockSpec(sblk, lambda b, j: (b, 0, 0, 0))],
        out_shape=[jax.ShapeDtypeStruct(u_f.shape, F32)] * 2
                  + [jax.ShapeDtypeStruct((bsz,) + sblk[1:], F32)] * 2,
        scratch_shapes=[pltpu.VMEM((N_DIRS * N_HEADS, HEAD_DIM, HEAD_DIM), F32)],
        compiler_params=_cparams("parallel", "arbitrary"),
        name="delta_scan",
    )(u_f, w_f, qg_f, kg_f, qk_f, gl_f, u_b, w_b, qg_b, kg_b, qk_b, gl_b, s0_f, s0_b)


def _mixout_kernel(x_ref, mod_ref, yc_ref, of_ref, ob_ref, z_ref, gn_ref, wt_ref, wb_ref, o_ref):
    o = of_ref[0] + ob_ref[0]
    gate = _silu(z_ref[0])
    gn = gn_ref[...]
    ys = []
    for h in range(N_HEADS):
        sl = slice(h * HEAD_DIM, (h + 1) * HEAD_DIM)
        ys.append(_rms(o[:, sl]) * gn * gate[:, sl])
    y_dn = jnp.concatenate(ys, axis=1)
    m = _bdot(yc_ref[0], wt_ref[...]) + _bdot(y_dn, wb_ref[...])
    o_ref[0] = x_ref[0] + mod_ref[0, 5:6, :] * m


def _mixout(x, mod, per_batch_mod, yc, o_f, o_b, z, gn, wt, wb, tm):
    bsz, seq, d = x.shape
    half = yc.shape[-1]
    mod_map = (lambda b, i: (b, 0, 0)) if per_batch_mod else (lambda b, i: (0, 0, 0))
    const2 = lambda b, i: (0, 0)
    tok = lambda b, i: (b, i, 0)
    return pl.pallas_call(
        _mixout_kernel,
        grid=(bsz, seq // tm),
        in_specs=[pl.BlockSpec((1, tm, d), tok),
                  pl.BlockSpec((1, N_MOD, d), mod_map)]
                 + [pl.BlockSpec((1, tm, half), tok)] * 4
                 + [pl.BlockSpec((1, HEAD_DIM), const2),
                    pl.BlockSpec((half, d), const2),
                    pl.BlockSpec((half, d), const2)],
        out_specs=pl.BlockSpec((1, tm, d), tok),
        out_shape=jax.ShapeDtypeStruct(x.shape, F32),
        compiler_params=_cparams("parallel", "arbitrary"),
        name="mix_out_proj",
    )(x, mod, yc, o_f, o_b, z, gn, wt, wb)


def _layer(x, mod, per_batch_mod, s0_f, s0_b, on_grid, lw, norm_final, final_norm, tm):
    (g1, wa1, wb1, wo1, gm, wc, wq, wz, wab, conv_w, dn_conv_w, alog_row, dtb_row, gn, wt, wbo,
     g2, wa2, wb2, wo2) = lw
    x = _ffn(x, mod, per_batch_mod, 0, g1, wa1, wb1, wo1, norm_final, False, tm)
    pc, pq, pz, pab = _proj(x, mod, per_batch_mod, gm, wc, wq, wz, wab, tm)
    yc, q, k, v, gb = _prep(pc, pq, pab, conv_w, dn_conv_w, alog_row, dtb_row,
                            GRID_W if on_grid else 1, tm)
    o_f, o_b, s_f, s_b = _scan(_chunk_prep(q, k, v, gb), s0_f, s0_b)
    x = _mixout(x, mod, per_batch_mod, yc, o_f, o_b, pz, gn, wt, wbo, tm)
    y = _ffn(x, mod, per_batch_mod, 6, g2, wa2, wb2, wo2, norm_final, final_norm, tm)
    return y, s_f, s_b


def _ffn_weights(w_in, w_out):
    d, two_ff = w_in.shape
    ff = two_ff // 2
    nck = ff // FF_CHUNK
    w = w_in.astype(BF16).reshape(d, 2, nck, FF_CHUNK).transpose(1, 2, 0, 3)
    return w[0], w[1], w_out.astype(BF16).reshape(nck, FF_CHUNK, d)


def kernel(x_prompt, x_sample, state_dn_fwd, state_dn_bwd, c, c_ctx, w_ada, b_ada, norm_ffn1, w_ffn1_in,
           w_ffn1_out, norm_mix, w_mix_in, conv_w, dn_conv_w, dn_a_log, dn_dt_bias, dn_norm, w_mix_out,
           norm_ffn2, w_ffn2_in, w_ffn2_out, norm_final):
    depth = w_ada.shape[0]
    d = x_prompt.shape[-1]
    n_lat = c.shape[0]
    cwid = conv_w.shape[-1]
    kwid = N_HEADS * HEAD_DIM
    n_gate = 4 * N_HEADS
    row = lambda a: a.reshape(1, -1)

    cvec = jnp.concatenate([c_ctx[None, :], c, jnp.zeros((16 - 1 - n_lat, d), F32)], axis=0)

    xp, xs = x_prompt, x_sample
    new_f, new_b = [], []
    for l in range(depth):
        mod = _modulation(cvec, w_ada[l], b_ada[l]).reshape(16, N_MOD, d)
        mod_ctx, mod_lat = mod[0:1], mod[1:1 + n_lat]

        wa1, wb1, wo1 = _ffn_weights(w_ffn1_in[l], w_ffn1_out[l])
        wa2, wb2, wo2 = _ffn_weights(w_ffn2_in[l], w_ffn2_out[l])
        wm = w_mix_in[l].astype(BF16)
        wc = wm[:, :3 * cwid]
        wq = wm[:, 3 * cwid:3 * cwid + 3 * kwid]
        wz = wm[:, 3 * cwid + 3 * kwid:3 * cwid + 4 * kwid]
        wab = jnp.pad(wm[:, 3 * cwid + 4 * kwid:], ((0, 0), (0, HEAD_DIM - n_gate)))
        zpad = jnp.zeros((N_HEADS,), F32)
        alog_row = jnp.concatenate([dn_a_log[l, 0], zpad, dn_a_log[l, 1], zpad,
                                    jnp.zeros((HEAD_DIM - n_gate,), F32)])[None, :]
        dtb_row = jnp.concatenate([dn_dt_bias[l, 0], zpad, dn_dt_bias[l, 1], zpad,
                                   jnp.zeros((HEAD_DIM - n_gate,), F32)])[None, :]
        wo = w_mix_out[l].astype(BF16)
        lw = (row(norm_ffn1[l]), wa1, wb1, wo1, row(norm_mix[l]), wc, wq, wz, wab, conv_w[l], dn_conv_w[l],
              alog_row, dtb_row, row(dn_norm[l]), wo[:cwid], wo[cwid:], row(norm_ffn2[l]), wa2, wb2, wo2)
        last = l == depth - 1
        xp, sf, sb = _layer(xp, mod_ctx, False, None, None, False, lw, row(norm_final), last, 256)
        new_f.append(sf)
        new_b.append(sb)
        xs, _, _ = _layer(xs, mod_lat, True, state_dn_fwd[:, l], state_dn_bwd[:, l], True, lw,
                          row(norm_final), last, 512)
    return xp, xs, jnp.stack(new_f, axis=1), jnp.stack(new_b, axis=1)
```

```python
import functools

import jax
import jax.numpy as jnp
from jax import lax
from jax.experimental import pallas as pl
from jax.experimental.pallas import tpu as pltpu

F32 = jnp.float32
BF16 = jnp.bfloat16

EPS = 1e-6
CHUNK = 64
GRID_W = 64
N_HEADS = 4
HEAD_DIM = 128
N_PAIRS = N_HEADS // 2
N_DIRS = 2
N_MOD = 9
FF_CHUNK = 256
PREP_CHUNKS = 2
SCAN_ROWS = 2
VMEM_LIMIT = 56 * 1024 * 1024


def _cparams(*sem):
    return pltpu.CompilerParams(dimension_semantics=sem, vmem_limit_bytes=VMEM_LIMIT)


def _bdot(a, b):
    return jnp.dot(a.astype(BF16), b.astype(BF16), preferred_element_type=F32)


def _split_bf16(a):
    hi = a.astype(BF16)
    lo = (a - hi.astype(F32)).astype(BF16)
    return hi, lo


def _dot_x3(a, b):
    a_hi, a_lo = _split_bf16(a)
    b_hi, b_lo = _split_bf16(b)
    d = lambda x, y: jnp.dot(x, y, preferred_element_type=F32)
    return d(a_hi, b_hi) + (d(a_hi, b_lo) + d(a_lo, b_hi))


def _sigmoid(x):
    return 1.0 / (1.0 + jnp.exp(-x))


def _silu(x):
    return x * _sigmoid(x)


def _rms(x):
    return x * lax.rsqrt(jnp.mean(x * x, axis=-1, keepdims=True) + EPS)


def _mod_kernel(c_ref, w_ref, b_ref, o_ref):
    s = _silu(c_ref[...])
    o_ref[...] = _dot_x3(s, w_ref[...]) + b_ref[...]


def _modulation(cvec, w_ada, b_ada):
    rows, d = cvec.shape
    n = w_ada.shape[1]
    tn = 1536
    return pl.pallas_call(
        _mod_kernel,
        grid=(n // tn,),
        in_specs=[pl.BlockSpec((rows, d), lambda i: (0, 0)),
                  pl.BlockSpec((d, tn), lambda i: (0, i)),
                  pl.BlockSpec((1, tn), lambda i: (0, i))],
        out_specs=pl.BlockSpec((rows, tn), lambda i: (0, i)),
        out_shape=jax.ShapeDtypeStruct((rows, n), F32),
        compiler_params=_cparams("arbitrary"),
        name="modulation",
    )(cvec, w_ada, b_ada.reshape(1, n))


def _ffn_kernel(x_ref, mod_ref, g_ref, wa_ref, wb_ref, wo_ref, gfin_ref, o_ref, h_ref, acc_ref,
                *, mod_row, final_norm):
    x = x_ref[0]
    sh = mod_ref[0, mod_row:mod_row + 1, :]
    sc = mod_ref[0, mod_row + 1:mod_row + 2, :]
    gt = mod_ref[0, mod_row + 2:mod_row + 3, :]
    h_ref[...] = (_rms(x) * g_ref[...] * (1.0 + sc) + sh).astype(BF16)
    acc_ref[...] = jnp.zeros_like(acc_ref)

    def body(c, carry):
        h = h_ref[...]
        a = jnp.dot(h, wa_ref[c], preferred_element_type=F32)
        b = jnp.dot(h, wb_ref[c], preferred_element_type=F32)
        act = (_silu(a) * b).astype(BF16)
        acc_ref[...] += jnp.dot(act, wo_ref[c], preferred_element_type=F32)
        return carry

    lax.fori_loop(0, wa_ref.shape[0], body, 0, unroll=True)
    y = x + 0.5 * gt * acc_ref[...]
    if final_norm:
        y = _rms(y) * gfin_ref[...]
    o_ref[0] = y


def _ffn(x, mod, per_batch_mod, mod_row, g, wa, wb, wo, gfin, final_norm, tm):
    bsz, seq, d = x.shape
    nck, _, ck = wa.shape
    mod_map = (lambda b, i: (b, 0, 0)) if per_batch_mod else (lambda b, i: (0, 0, 0))
    const3 = lambda b, i: (0, 0, 0)
    const2 = lambda b, i: (0, 0)
    return pl.pallas_call(
        functools.partial(_ffn_kernel, mod_row=mod_row, final_norm=final_norm),
        grid=(bsz, seq // tm),
        in_specs=[pl.BlockSpec((1, tm, d), lambda b, i: (b, i, 0)),
                  pl.BlockSpec((1, N_MOD, d), mod_map),
                  pl.BlockSpec((1, d), const2),
                  pl.BlockSpec((nck, d, ck), const3, pipeline_mode=pl.Buffered(1)),
                  pl.BlockSpec((nck, d, ck), const3, pipeline_mode=pl.Buffered(1)),
                  pl.BlockSpec((nck, ck, d), const3, pipeline_mode=pl.Buffered(1)),
                  pl.BlockSpec((1, d), const2)],
        out_specs=pl.BlockSpec((1, tm, d), lambda b, i: (b, i, 0)),
        out_shape=jax.ShapeDtypeStruct(x.shape, F32),
        scratch_shapes=[pltpu.VMEM((tm, d), BF16), pltpu.VMEM((tm, d), F32)],
        compiler_params=_cparams("parallel", "arbitrary"),
        name="ffn_final" if final_norm else "ffn",
    )(x, mod, g, wa, wb, wo, gfin)


def _proj_kernel(x_ref, mod_ref, g_ref, wc_ref, wq_ref, wz_ref, wab_ref, pc_ref, pq_ref, pz_ref, pab_ref):
    x = x_ref[0]
    sh = mod_ref[0, 3:4, :]
    sc = mod_ref[0, 4:5, :]
    h = (_rms(x) * g_ref[...] * (1.0 + sc) + sh).astype(BF16)
    pc_ref[0] = jnp.dot(h, wc_ref[...], preferred_element_type=F32)
    pq_ref[0] = jnp.dot(h, wq_ref[...], preferred_element_type=F32)
    pz_ref[0] = jnp.dot(h, wz_ref[...], preferred_element_type=F32)
    pab_ref[0] = jnp.dot(h, wab_ref[...], preferred_element_type=F32)


def _proj(x, mod, per_batch_mod, g, wc, wq, wz, wab, tm):
    bsz, seq, d = x.shape
    mod_map = (lambda b, i: (b, 0, 0)) if per_batch_mod else (lambda b, i: (0, 0, 0))
    const2 = lambda b, i: (0, 0)
    tok = lambda b, i: (b, i, 0)
    widths = (wc.shape[1], wq.shape[1], wz.shape[1], wab.shape[1])
    return pl.pallas_call(
        _proj_kernel,
        grid=(bsz, seq // tm),
        in_specs=[pl.BlockSpec((1, tm, d), tok),
                  pl.BlockSpec((1, N_MOD, d), mod_map),
                  pl.BlockSpec((1, d), const2)]
                 + [pl.BlockSpec((d, w), const2, pipeline_mode=pl.Buffered(1)) for w in widths],
        out_specs=[pl.BlockSpec((1, tm, w), tok) for w in widths],
        out_shape=[jax.ShapeDtypeStruct((bsz, seq, w), F32) for w in widths],
        compiler_params=_cparams("parallel", "arbitrary"),
        name="mix_in_proj",
    )(x, mod, g, wc, wq, wz, wab)


def _shift_rows(cur, prev_halo, next_halo, dist, first, last):
    tm = cur.shape[0]
    hp = jnp.where(first, 0.0, prev_halo)
    hn = jnp.where(last, 0.0, next_halo)
    if dist % 8 == 0:
        prev = jnp.concatenate([hp[hp.shape[0] - dist:], cur[:tm - dist]], axis=0)
        nxt = jnp.concatenate([cur[dist:], hn[:dist]], axis=0)
        return prev, nxt
    assert dist == 1
    row = lax.broadcasted_iota(jnp.int32, cur.shape, 0)
    prev = jnp.where(row == 0, hp[hp.shape[0] - 1:], pltpu.roll(cur, 1, axis=0))
    nxt = jnp.where(row == tm - 1, hn[:1], pltpu.roll(cur, tm - 1, axis=0))
    return prev, nxt


def _prep_kernel(pc_ref, pcp_ref, pcn_ref, pq_ref, pqp_ref, pqn_ref, pab_ref,
                 cw_ref, dw_ref, alog_ref, dtb_ref,
                 yc_ref, q_ref, k_ref, v_ref, gb_ref, *, conv_dist, n_tiles):
    i = pl.program_id(1)
    first = i == 0
    last = i == n_tiles - 1
    cwid = yc_ref.shape[-1]
    kwid = q_ref.shape[-1]

    pc = pc_ref[0]
    u = pc[:, cwid:2 * cwid] * pc[:, 2 * cwid:]
    pcp = pcp_ref[0]
    pcn = pcn_ref[0]
    up, un = _shift_rows(u, pcp[:, cwid:2 * cwid] * pcp[:, 2 * cwid:],
                         pcn[:, cwid:2 * cwid] * pcn[:, 2 * cwid:], conv_dist, first, last)
    cw = cw_ref[...]
    yc_ref[0] = pc[:, :cwid] * (up * cw[0:1] + u * cw[1:2] + un * cw[2:3])

    pq = pq_ref[0]
    qp, qn = _shift_rows(pq, pqp_ref[0], pqn_ref[0], 1, first, last)
    dw = dw_ref[...]
    qkv = _silu(qp * dw[0:1] + pq * dw[1:2] + qn * dw[2:3])
    for h in range(N_HEADS):
        sl = slice(h * HEAD_DIM, (h + 1) * HEAD_DIM)
        qh = qkv[:, sl]
        kh = qkv[:, kwid + h * HEAD_DIM: kwid + (h + 1) * HEAD_DIM]
        q_ref[0, :, sl] = qh * (lax.rsqrt(jnp.sum(qh * qh, axis=-1, keepdims=True) + EPS) * HEAD_DIM ** -0.5)
        k_ref[0, :, sl] = kh * lax.rsqrt(jnp.sum(kh * kh, axis=-1, keepdims=True) + EPS)
    v_ref[0] = qkv[:, 2 * kwid:]

    ab = pab_ref[0]
    lane = lax.broadcasted_iota(jnp.int32, ab.shape, 1)
    is_gf = lane < N_HEADS
    is_gb = (lane >= 2 * N_HEADS) & (lane < 3 * N_HEADS)
    xg = ab + dtb_ref[...]
    softplus = jnp.maximum(xg, 0.0) + jnp.log1p(jnp.exp(-jnp.abs(xg)))
    gates = jnp.where(is_gf | is_gb, -jnp.exp(alog_ref[...]) * softplus, _sigmoid(ab))
    r = lax.broadcasted_iota(jnp.int32, (CHUNK, CHUNK), 0)
    c = lax.broadcasted_iota(jnp.int32, (CHUNK, CHUNK), 1)
    tril = (r >= c).astype(BF16)
    triu = (r <= c).astype(BF16)
    g_hi = gates.astype(BF16)
    rem = gates - g_hi.astype(F32)
    g_mid = rem.astype(BF16)
    g_lo = (rem - g_mid.astype(F32)).astype(BF16)
    d = lambda m, x: jnp.dot(m, x, preferred_element_type=F32)
    lane_c = lax.broadcasted_iota(jnp.int32, (CHUNK, ab.shape[1]), 1)
    is_gf_c = lane_c < N_HEADS
    is_gb_c = (lane_c >= 2 * N_HEADS) & (lane_c < 3 * N_HEADS)
    for t in range(ab.shape[0] // CHUNK):
        rs = slice(t * CHUNK, (t + 1) * CHUNK)
        parts = (g_hi[rs], g_mid[rs], g_lo[rs])
        pre = d(tril, parts[0]) + (d(tril, parts[1]) + d(tril, parts[2]))
        suf = d(triu, parts[0]) + (d(triu, parts[1]) + d(triu, parts[2]))
        gb_ref[0, rs, :] = jnp.where(is_gf_c, pre, jnp.where(is_gb_c, suf, gates[rs]))


def _prep(pc, pq, pab, conv_w, dn_conv_w, alog_row, dtb_row, conv_dist, tm):
    bsz, seq, wc3 = pc.shape
    wq3 = pq.shape[-1]
    cwid, kwid = wc3 // 3, wq3 // 3
    n_tiles = seq // tm
    hc = max(conv_dist, 8)
    hq = 8
    tok = lambda b, i: (b, i, 0)
    const2 = lambda b, i: (0, 0)

    def prev_map(hrows):
        return lambda b, i: (b, jnp.maximum(i * (tm // hrows) - 1, 0), 0)

    def next_map(hrows):
        return lambda b, i: (b, jnp.minimum((i + 1) * (tm // hrows), seq // hrows - 1), 0)

    outs = [jax.ShapeDtypeStruct((bsz, seq, w), F32) for w in (cwid, kwid, kwid, kwid, pab.shape[-1])]
    return pl.pallas_call(
        functools.partial(_prep_kernel, conv_dist=conv_dist, n_tiles=n_tiles),
        grid=(bsz, n_tiles),
        in_specs=[pl.BlockSpec((1, tm, wc3), tok),
                  pl.BlockSpec((1, hc, wc3), prev_map(hc)),
                  pl.BlockSpec((1, hc, wc3), next_map(hc)),
                  pl.BlockSpec((1, tm, wq3), tok),
                  pl.BlockSpec((1, hq, wq3), prev_map(hq)),
                  pl.BlockSpec((1, hq, wq3), next_map(hq)),
                  pl.BlockSpec((1, tm, pab.shape[-1]), tok),
                  pl.BlockSpec(conv_w.shape, const2),
                  pl.BlockSpec(dn_conv_w.shape, const2),
                  pl.BlockSpec(alog_row.shape, const2),
                  pl.BlockSpec(dtb_row.shape, const2)],
        out_specs=[pl.BlockSpec((1, tm, o.shape[-1]), tok) for o in outs],
        out_shape=outs,
        compiler_params=_cparams("parallel", "arbitrary"),
        name="mixer_prep",
    )(pc, pc, pc, pq, pq, pq, pab, conv_w, dn_conv_w, alog_row, dtb_row)


def _pair_block_diag(y16, keep_left, keep_right):
    return jnp.concatenate([y16 * keep_left, y16 * keep_right], axis=0)


def _pair_inverses(a_list, eye_pair, keep_left, keep_right):
    d = lambda x, y: jnp.dot(x, y, preferred_element_type=F32)
    bd = lambda y16: _pair_block_diag(y16, keep_left, keep_right)
    ps = [-a for a in a_list]
    ts = [eye_pair + p for p in ps]
    width = eye_pair.shape[1]
    n_sq = CHUNK.bit_length() - 2
    for i in range(n_sq + 1):
        new_ps, new_ts = [], []
        for p, t in zip(ps, ts):
            p_hi, p_lo = _split_bf16(p)
            lhs = jnp.concatenate([p_hi, p_lo], axis=1)
            if i == 0:
                r_hi, r_lo = bd(p_hi), bd(p_lo)
            else:
                t_hi, t_lo = _split_bf16(t)
                if i == n_sq:
                    r_hi, r_lo = bd(t_hi), bd(t_lo)
                else:
                    r_hi = jnp.concatenate([bd(p_hi), bd(t_hi)], axis=1)
                    r_lo = jnp.concatenate([bd(p_lo), bd(t_lo)], axis=1)
            res = d(lhs, jnp.concatenate([r_hi, r_hi], axis=0)) + d(p_hi, r_lo)
            if i == 0:
                new_ps.append(res)
                new_ts.append(t)
            elif i == n_sq:
                new_ps.append(p)
                new_ts.append(t + res)
            else:
                new_ps.append(res[:, :width])
                new_ts.append(t + res[:, width:])
        ps, ts = new_ps, new_ts
    return ts


def _chunk_prep_kernel(q_ref, k_ref, v_ref, g_ref,
                       uf_ref, ub_ref, wf_ref, wb_ref, qgf_ref, qgb_ref, kgf_ref, kgb_ref,
                       qkf_ref, qkb_ref, glf_ref, glb_ref):
    u_refs, w_refs, qg_refs = (uf_ref, ub_ref), (wf_ref, wb_ref), (qgf_ref, qgb_ref)
    kg_refs, qk_refs, gl_refs = (kgf_ref, kgb_ref), (qkf_ref, qkb_ref), (glf_ref, glb_ref)
    row = lax.broadcasted_iota(jnp.int32, (CHUNK, 2 * CHUNK), 0)
    lane = lax.broadcasted_iota(jnp.int32, (CHUNK, 2 * CHUNK), 1)
    col = lane & (CHUNK - 1)
    left = lane < CHUNK
    keep_left = jnp.where(left, 1.0, 0.0).astype(BF16)
    keep_right = jnp.where(left, 0.0, 1.0).astype(BF16)
    eye_pair = jnp.where(row == col, 1.0, 0.0)
    zeros16 = jnp.zeros((CHUNK, HEAD_DIM), BF16)
    dot = lambda x, y: jnp.dot(x, y, preferred_element_type=F32)

    chains, a_list = [], []
    for c in range(PREP_CHUNKS):
        rs = slice(c * CHUNK, (c + 1) * CHUNK)
        gates = g_ref[0, rs, :]
        for p in range(N_PAIRS):
            sl = slice(2 * p * HEAD_DIM, 2 * (p + 1) * HEAD_DIM)
            q2, k2, v2 = q_ref[0, rs, sl], k_ref[0, rs, sl], v_ref[0, rs, sl]
            k16 = k2.astype(BF16)
            k_bd = jnp.concatenate([jnp.concatenate([k16[:, :HEAD_DIM], zeros16], axis=1),
                                    jnp.concatenate([zeros16, k16[:, HEAD_DIM:]], axis=1)], axis=0)
            kq = jnp.concatenate([k16, q2.astype(BF16)], axis=0)
            gram = lax.dot_general(kq, k_bd, (((1,), (1,)), ((), ())), preferred_element_type=F32)
            for d in range(N_DIRS):
                gcol, bcol = 2 * d * N_HEADS, (2 * d + 1) * N_HEADS
                gc = [gates[:, gcol + 2 * p + e:gcol + 2 * p + e + 1] for e in range(2)]
                beta = [gates[:, bcol + 2 * p + e:bcol + 2 * p + e + 1] for e in range(2)]
                gc_pair = jnp.where(left, gc[0], gc[1])
                beta_pair = jnp.where(left, beta[0], beta[1])
                gr_pair = jnp.sum(jnp.where(row == col, gc_pair, 0.0), axis=0, keepdims=True)
                if d == 0:
                    incl, strict, last = row >= col, row > col, CHUNK - 1
                else:
                    incl, strict, last = row <= col, row < col, 0
                decay = jnp.where(incl, jnp.exp(jnp.where(incl, gc_pair - gr_pair, 0.0)), 0.0)
                a_list.append(jnp.where(strict, beta_pair * gram[:CHUNK] * decay, 0.0))
                qk_refs[d][0, rs, p * HEAD_DIM:(p + 1) * HEAD_DIM] = (
                    jnp.where(incl, gram[CHUNK:] * decay, 0.0).astype(BF16))
                chains.append((c, rs, p, d, q2, k2, v2, gc, beta, last))

    t_list = _pair_inverses(a_list, eye_pair, keep_left, keep_right)

    for (c, rs, p, d, q2, k2, v2, gc, beta, last), t in zip(chains, t_list):
        t16 = t.astype(BF16)
        for e in range(2):
            h = 2 * p + e
            sl = slice(e * HEAD_DIM, (e + 1) * HEAD_DIM)
            hs = slice(h * HEAD_DIM, (h + 1) * HEAD_DIM)
            eg = jnp.exp(gc[e])
            g_last = gc[e][last:last + 1, :]
            rhs = jnp.concatenate([v2[:, sl] * beta[e], k2[:, sl] * (beta[e] * eg)], axis=1).astype(BF16)
            zpad = jnp.zeros_like(rhs)
            uw = dot(t16, jnp.concatenate([rhs, zpad] if e == 0 else [zpad, rhs], axis=0))
            u_refs[d][0, rs, hs] = uw[:, :HEAD_DIM]
            w_refs[d][0, rs, hs] = uw[:, HEAD_DIM:].astype(BF16)
            qg_refs[d][0, rs, hs] = (q2[:, sl] * eg).astype(BF16)
            kg_refs[d][0, rs, hs] = (k2[:, sl] * jnp.exp(g_last - gc[e])).astype(BF16)
            gl_refs[d][0, c, h:h + 1, :] = jnp.broadcast_to(jnp.exp(g_last), (1, HEAD_DIM))


def _chunk_prep(q, k, v, gb):
    bsz, seq, kwid = q.shape
    rows = PREP_CHUNKS * CHUNK
    n = seq // CHUNK
    tok = lambda b, i: (b, i, 0)
    f32_full = jax.ShapeDtypeStruct(q.shape, F32)
    bf_full = jax.ShapeDtypeStruct(q.shape, BF16)
    bf_half = jax.ShapeDtypeStruct((bsz, seq, kwid // 2), BF16)
    gl = jax.ShapeDtypeStruct((bsz, n, N_HEADS, HEAD_DIM), F32)
    outs = [f32_full] * 2 + [bf_full] * 6 + [bf_half] * 2 + [gl] * 2
    out_specs = ([pl.BlockSpec((1, rows, kwid), tok)] * 8 + [pl.BlockSpec((1, rows, kwid // 2), tok)] * 2
                 + [pl.BlockSpec((1, PREP_CHUNKS, N_HEADS, HEAD_DIM), lambda b, i: (b, i, 0, 0))] * 2)
    return pl.pallas_call(
        _chunk_prep_kernel,
        grid=(bsz, seq // rows),
        in_specs=[pl.BlockSpec((1, rows, kwid), tok)] * 3 + [pl.BlockSpec((1, rows, gb.shape[-1]), tok)],
        out_specs=out_specs,
        out_shape=outs,
        compiler_params=_cparams("parallel", "parallel"),
        name="delta_chunk_prep",
    )(q, k, v, gb)


def _scan_kernel(uf_ref, wf_ref, qgf_ref, kgf_ref, qkf_ref, glf_ref,
                 ub_ref, wb_ref, qgb_ref, kgb_ref, qkb_ref, glb_ref, s0f_ref, s0b_ref,
                 of_ref, ob_ref, sff_ref, sfb_ref, s_ref, *, n_chunks, has_s0):
    j = pl.program_id(1)
    n_hd = N_DIRS * N_HEADS

    @pl.when(j == 0)
    def _():
        for r in range(SCAN_ROWS):
            if has_s0:
                s_ref[r * n_hd:r * n_hd + N_HEADS] = s0f_ref[r]
                s_ref[r * n_hd + N_HEADS:(r + 1) * n_hd] = s0b_ref[r]
            else:
                s_ref[r * n_hd:(r + 1) * n_hd] = jnp.zeros((n_hd, HEAD_DIM, HEAD_DIM), F32)

    dot = lambda x, y: jnp.dot(x, y, preferred_element_type=F32)
    zeros16 = jnp.zeros((CHUNK, HEAD_DIM), BF16)
    ins = ((uf_ref, wf_ref, qgf_ref, kgf_ref, qkf_ref, glf_ref), (ub_ref, wb_ref, qgb_ref, kgb_ref, qkb_ref, glb_ref))
    outs = (of_ref, ob_ref)
    hds = [(r, d, h) for r in range(SCAN_ROWS) for d in range(N_DIRS) for h in range(N_HEADS)]
    hsl = lambda h: slice(h * HEAD_DIM, (h + 1) * HEAD_DIM)
    sidx = lambda r, d, h: r * n_hd + d * N_HEADS + h

    s_old, ws_qs = [], []
    for r, d, h in hds:
        s = s_ref[sidx(r, d, h)]
        s_old.append(s)
        lhs = jnp.concatenate([ins[d][1][r, :, hsl(h)], ins[d][2][r, :, hsl(h)]], axis=0)
        ws_qs.append(dot(lhs, s.astype(BF16)))
    v_new = []
    for i, (r, d, h) in enumerate(hds):
        vn = (ins[d][0][r, :, hsl(h)] - ws_qs[i][:CHUNK]).astype(BF16)
        v_new.append(vn)
        v_pad = jnp.concatenate([vn, zeros16] if h % 2 == 0 else [zeros16, vn], axis=0)
        qk_pair = ins[d][4][r, :, hsl(h // 2)]
        outs[d][r, :, hsl(h)] = ws_qs[i][CHUNK:] + dot(qk_pair, v_pad)
    for i, (r, d, h) in enumerate(hds):
        kg = ins[d][3][r, :, hsl(h)]
        upd = lax.dot_general(kg, v_new[i], (((0,), (0,)), ((), ())), preferred_element_type=F32)
        s_ref[sidx(r, d, h)] = s_old[i] * ins[d][5][r, 0, h:h + 1, :] + upd

    @pl.when(j == n_chunks - 1)
    def _():
        for r in range(SCAN_ROWS):
            sff_ref[r] = s_ref[r * n_hd:r * n_hd + N_HEADS]
            sfb_ref[r] = s_ref[r * n_hd + N_HEADS:(r + 1) * n_hd]


def _scan(prep, s0_f, s0_b):
    (u_f, u_b, w_f, w_b, qg_f, qg_b, kg_f, kg_b, qk_f, qk_b, gl_f, gl_b) = prep
    bsz, seq, kwid = u_f.shape
    assert bsz % SCAN_ROWS == 0
    n = seq // CHUNK
    has_s0 = s0_f is not None
    sblk = (SCAN_ROWS, N_HEADS, HEAD_DIM, HEAD_DIM)
    if not has_s0:
        s0_f = s0_b = jnp.zeros(sblk, F32)
    fwd = lambda b, j: (b, j, 0)
    bwd = lambda b, j: (b, n - 1 - j, 0)
    fwd4 = lambda b, j: (b, j, 0, 0)
    bwd4 = lambda b, j: (b, n - 1 - j, 0, 0)
    smap = (lambda b, j: (b, 0, 0, 0)) if has_s0 else (lambda b, j: (0, 0, 0, 0))

    def dir_specs(tok, tok4):
        return ([pl.BlockSpec((SCAN_ROWS, CHUNK, kwid), tok)] * 4
                + [pl.BlockSpec((SCAN_ROWS, CHUNK, kwid // 2), tok),
                   pl.BlockSpec((SCAN_ROWS, 1, N_HEADS, HEAD_DIM), tok4)])

    return pl.pallas_call(
        functools.partial(_scan_kernel, n_chunks=n, has_s0=has_s0),
        grid=(bsz // SCAN_ROWS, n),
        in_specs=dir_specs(fwd, fwd4) + dir_specs(bwd, bwd4) + [pl.BlockSpec(sblk, smap)] * 2,
        out_specs=[pl.BlockSpec((SCAN_ROWS, CHUNK, kwid), fwd),
                   pl.BlockSpec((SCAN_ROWS, CHUNK, kwid), bwd),
                   pl.BlockSpec(sblk, lambda b, j: (b, 0, 0, 0)),
                   pl.BlockSpec(sblk, lambda b, j: (b, 0, 0, 0))],
        out_shape=[jax.ShapeDtypeStruct(u_f.shape, F32)] * 2
                  + [jax.ShapeDtypeStruct((bsz,) + sblk[1:], F32)] * 2,
        scratch_shapes=[pltpu.VMEM((SCAN_ROWS * N_DIRS * N_HEADS, HEAD_DIM, HEAD_DIM), F32)],
        compiler_params=_cparams("parallel", "arbitrary"),
        name="delta_scan",
    )(u_f, w_f, qg_f, kg_f, qk_f, gl_f, u_b, w_b, qg_b, kg_b, qk_b, gl_b, s0_f, s0_b)


def _mixout_kernel(x_ref, mod_ref, yc_ref, of_ref, ob_ref, z_ref, gn_ref, wt_ref, wb_ref, o_ref):
    o = of_ref[0] + ob_ref[0]
    gate = _silu(z_ref[0])
    gn = gn_ref[...]
    ys = []
    for h in range(N_HEADS):
        sl = slice(h * HEAD_DIM, (h + 1) * HEAD_DIM)
        ys.append(_rms(o[:, sl]) * gn * gate[:, sl])
    y_dn = jnp.concatenate(ys, axis=1)
    m = _bdot(yc_ref[0], wt_ref[...]) + _bdot(y_dn, wb_ref[...])
    o_ref[0] = x_ref[0] + mod_ref[0, 5:6, :] * m


def _mixout(x, mod, per_batch_mod, yc, o_f, o_b, z, gn, wt, wb, tm):
    bsz, seq, d = x.shape
    half = yc.shape[-1]
    mod_map = (lambda b, i: (b, 0, 0)) if per_batch_mod else (lambda b, i: (0, 0, 0))
    const2 = lambda b, i: (0, 0)
    tok = lambda b, i: (b, i, 0)
    return pl.pallas_call(
        _mixout_kernel,
        grid=(bsz, seq // tm),
        in_specs=[pl.BlockSpec((1, tm, d), tok),
                  pl.BlockSpec((1, N_MOD, d), mod_map)]
                 + [pl.BlockSpec((1, tm, half), tok)] * 4
                 + [pl.BlockSpec((1, HEAD_DIM), const2),
                    pl.BlockSpec((half, d), const2),
                    pl.BlockSpec((half, d), const2)],
        out_specs=pl.BlockSpec((1, tm, d), tok),
        out_shape=jax.ShapeDtypeStruct(x.shape, F32),
        compiler_params=_cparams("parallel", "arbitrary"),
        name="mix_out_proj",
    )(x, mod, yc, o_f, o_b, z, gn, wt, wb)


def _layer(x, mod, per_batch_mod, s0_f, s0_b, on_grid, lw, norm_final, final_norm, tm):
    (g1, wa1, wb1, wo1, gm, wc, wq, wz, wab, conv_w, dn_conv_w, alog_row, dtb_row, gn, wt, wbo,
     g2, wa2, wb2, wo2) = lw
    x = _ffn(x, mod, per_batch_mod, 0, g1, wa1, wb1, wo1, norm_final, False, tm)
    pc, pq, pz, pab = _proj(x, mod, per_batch_mod, gm, wc, wq, wz, wab, tm)
    yc, q, k, v, gb = _prep(pc, pq, pab, conv_w, dn_conv_w, alog_row, dtb_row,
                            GRID_W if on_grid else 1, tm)
    o_f, o_b, s_f, s_b = _scan(_chunk_prep(q, k, v, gb), s0_f, s0_b)
    x = _mixout(x, mod, per_batch_mod, yc, o_f, o_b, pz, gn, wt, wbo, tm)
    y = _ffn(x, mod, per_batch_mod, 6, g2, wa2, wb2, wo2, norm_final, final_norm, tm)
    return y, s_f, s_b


def _ffn_weights(w_in, w_out):
    d, two_ff = w_in.shape
    ff = two_ff // 2
    nck = ff // FF_CHUNK
    w = w_in.astype(BF16).reshape(d, 2, nck, FF_CHUNK).transpose(1, 2, 0, 3)
    return w[0], w[1], w_out.astype(BF16).reshape(nck, FF_CHUNK, d)


def kernel(x_prompt, x_sample, state_dn_fwd, state_dn_bwd, c, c_ctx, w_ada, b_ada, norm_ffn1, w_ffn1_in,
           w_ffn1_out, norm_mix, w_mix_in, conv_w, dn_conv_w, dn_a_log, dn_dt_bias, dn_norm, w_mix_out,
           norm_ffn2, w_ffn2_in, w_ffn2_out, norm_final):
    depth = w_ada.shape[0]
    d = x_prompt.shape[-1]
    n_lat = c.shape[0]
    cwid = conv_w.shape[-1]
    kwid = N_HEADS * HEAD_DIM
    n_gate = 4 * N_HEADS
    row = lambda a: a.reshape(1, -1)

    cvec = jnp.concatenate([c_ctx[None, :], c, jnp.zeros((16 - 1 - n_lat, d), F32)], axis=0)

    xp, xs = x_prompt, x_sample
    new_f, new_b = [], []
    for l in range(depth):
        mod = _modulation(cvec, w_ada[l], b_ada[l]).reshape(16, N_MOD, d)
        mod_ctx, mod_lat = mod[0:1], mod[1:1 + n_lat]

        wa1, wb1, wo1 = _ffn_weights(w_ffn1_in[l], w_ffn1_out[l])
        wa2, wb2, wo2 = _ffn_weights(w_ffn2_in[l], w_ffn2_out[l])
        wm = w_mix_in[l].astype(BF16)
        wc = wm[:, :3 * cwid]
        wq = wm[:, 3 * cwid:3 * cwid + 3 * kwid]
        wz = wm[:, 3 * cwid + 3 * kwid:3 * cwid + 4 * kwid]
        wab = jnp.pad(wm[:, 3 * cwid + 4 * kwid:], ((0, 0), (0, HEAD_DIM - n_gate)))
        zpad = jnp.zeros((N_HEADS,), F32)
        alog_row = jnp.concatenate([dn_a_log[l, 0], zpad, dn_a_log[l, 1], zpad,
                                    jnp.zeros((HEAD_DIM - n_gate,), F32)])[None, :]
        dtb_row = jnp.concatenate([dn_dt_bias[l, 0], zpad, dn_dt_bias[l, 1], zpad,
                                   jnp.zeros((HEAD_DIM - n_gate,), F32)])[None, :]
        wo = w_mix_out[l].astype(BF16)
        lw = (row(norm_ffn1[l]), wa1, wb1, wo1, row(norm_mix[l]), wc, wq, wz, wab, conv_w[l], dn_conv_w[l],
              alog_row, dtb_row, row(dn_norm[l]), wo[:cwid], wo[cwid:], row(norm_ffn2[l]), wa2, wb2, wo2)
        last = l == depth - 1
        xp, sf, sb = _layer(xp, mod_ctx, False, None, None, False, lw, row(norm_final), last, 256)
        new_f.append(sf)
        new_b.append(sb)
        xs, _, _ = _layer(xs, mod_lat, True, state_dn_fwd[:, l], state_dn_bwd[:, l], True, lw,
                          row(norm_final), last, 512)
    return xp, xs, jnp.stack(new_f, axis=1), jnp.stack(new_b, axis=1)
```

```python
import functools

import jax
import jax.numpy as jnp
from jax import lax
from jax.experimental import pallas as pl
from jax.experimental.pallas import tpu as pltpu

F32 = jnp.float32
BF16 = jnp.bfloat16

EPS = 1e-6
CHUNK = 64
GRID_W = 64
N_HEADS = 4
HEAD_DIM = 128
N_PAIRS = N_HEADS // 2
N_DIRS = 2
N_MOD = 9
FF_CHUNK = 256
TOKEN_TILE = 512
PREP_CHUNKS = 2
SCAN_ROWS = 2
VMEM_LIMIT = 56 * 1024 * 1024


def _cparams(*sem):
    return pltpu.CompilerParams(dimension_semantics=sem, vmem_limit_bytes=VMEM_LIMIT)


def _bdot(a, b):
    return jnp.dot(a.astype(BF16), b.astype(BF16), preferred_element_type=F32)


def _split_bf16(a):
    hi = a.astype(BF16)
    lo = (a - hi.astype(F32)).astype(BF16)
    return hi, lo


def _dot_x3(a, b):
    a_hi, a_lo = _split_bf16(a)
    b_hi, b_lo = _split_bf16(b)
    d = lambda x, y: jnp.dot(x, y, preferred_element_type=F32)
    return d(a_hi, b_hi) + (d(a_hi, b_lo) + d(a_lo, b_hi))


def _sigmoid(x):
    return 1.0 / (1.0 + jnp.exp(-x))


def _silu(x):
    return x * _sigmoid(x)


def _rms(x):
    return x * lax.rsqrt(jnp.mean(x * x, axis=-1, keepdims=True) + EPS)


def _mod_kernel(c_ref, w_ref, b_ref, o_ref):
    s = _silu(c_ref[...])
    o_ref[...] = _dot_x3(s, w_ref[...]) + b_ref[...]


def _modulation(cvec, w_ada, b_ada):
    rows, d = cvec.shape
    n = w_ada.shape[1]
    tn = 1536
    return pl.pallas_call(
        _mod_kernel,
        grid=(n // tn,),
        in_specs=[pl.BlockSpec((rows, d), lambda i: (0, 0)),
                  pl.BlockSpec((d, tn), lambda i: (0, i)),
                  pl.BlockSpec((1, tn), lambda i: (0, i))],
        out_specs=pl.BlockSpec((rows, tn), lambda i: (0, i)),
        out_shape=jax.ShapeDtypeStruct((rows, n), F32),
        compiler_params=_cparams("arbitrary"),
        name="modulation",
    )(cvec, w_ada, b_ada.reshape(1, n))


def _swiglu_residual(x, x_keep_ref, mod_ref, mod_row, g_ref, wa_ref, wb_ref, wo_ref, h_ref, acc_ref):
    sh = mod_ref[0, mod_row:mod_row + 1, :]
    sc = mod_ref[0, mod_row + 1:mod_row + 2, :]
    gt = mod_ref[0, mod_row + 2:mod_row + 3, :]
    h_ref[...] = (_rms(x) * g_ref[...] * (1.0 + sc) + sh).astype(BF16)
    acc_ref[...] = jnp.zeros_like(acc_ref)

    def body(c, carry):
        h = h_ref[...]
        a = jnp.dot(h, wa_ref[c], preferred_element_type=F32)
        b = jnp.dot(h, wb_ref[c], preferred_element_type=F32)
        act = (_silu(a) * b).astype(BF16)
        acc_ref[...] += jnp.dot(act, wo_ref[c], preferred_element_type=F32)
        return carry

    lax.fori_loop(0, wa_ref.shape[0], body, 0, unroll=True)
    return x_keep_ref[...] + 0.5 * gt * acc_ref[...]


def _ffn1_kernel(x_ref, mod_ref, g_ref, wa_ref, wb_ref, wo_ref, o_ref, h_ref, acc_ref):
    o_ref[0] = _swiglu_residual(x_ref[0], x_ref.at[0], mod_ref, 0, g_ref, wa_ref, wb_ref, wo_ref, h_ref, acc_ref)


def _mix_ffn2_kernel(x_ref, mod_ref, yc_ref, of_ref, ob_ref, z_ref, gn_ref, wt_ref, wbm_ref,
                     g_ref, wa_ref, wb_ref, wo_ref, gfin_ref, o_ref, h_ref, acc_ref, x2_ref, *, final_norm):
    o = of_ref[0] + ob_ref[0]
    gate = _silu(z_ref[0])
    gn = gn_ref[...]
    ys = []
    for h in range(N_HEADS):
        sl = slice(h * HEAD_DIM, (h + 1) * HEAD_DIM)
        ys.append(_rms(o[:, sl]) * gn * gate[:, sl])
    y_dn = jnp.concatenate(ys, axis=1)
    m = _bdot(yc_ref[0], wt_ref[...]) + _bdot(y_dn, wbm_ref[...])
    x2_ref[...] = x_ref[0] + mod_ref[0, 5:6, :] * m
    y = _swiglu_residual(x2_ref[...], x2_ref, mod_ref, 6, g_ref, wa_ref, wb_ref, wo_ref, h_ref, acc_ref)
    if final_norm:
        y = _rms(y) * gfin_ref[...]
    o_ref[0] = y


def _token_specs(d, tm, per_batch_mod):
    mod_map = (lambda b, i: (b, 0, 0)) if per_batch_mod else (lambda b, i: (0, 0, 0))
    return pl.BlockSpec((1, tm, d), lambda b, i: (b, i, 0)), pl.BlockSpec((1, N_MOD, d), mod_map)


def _ffn_weight_specs(wa, wo):
    nck, d, ck = wa.shape
    const3 = lambda b, i: (0, 0, 0)
    return [pl.BlockSpec((nck, d, ck), const3, pipeline_mode=pl.Buffered(1)),
            pl.BlockSpec((nck, d, ck), const3, pipeline_mode=pl.Buffered(1)),
            pl.BlockSpec((nck, ck, d), const3, pipeline_mode=pl.Buffered(1))]


def _ffn1(x, mod, per_batch_mod, g, wa, wb, wo, tm):
    bsz, seq, d = x.shape
    const2 = lambda b, i: (0, 0)
    x_spec, mod_spec = _token_specs(d, tm, per_batch_mod)
    return pl.pallas_call(
        _ffn1_kernel,
        grid=(bsz, seq // tm),
        in_specs=[x_spec, mod_spec, pl.BlockSpec((1, d), const2)] + _ffn_weight_specs(wa, wo),
        out_specs=x_spec,
        out_shape=jax.ShapeDtypeStruct(x.shape, F32),
        scratch_shapes=[pltpu.VMEM((tm, d), BF16), pltpu.VMEM((tm, d), F32)],
        compiler_params=_cparams("parallel", "arbitrary"),
        name="ffn1",
    )(x, mod, g, wa, wb, wo)


def _mix_ffn2(x, mod, per_batch_mod, yc, o_f, o_b, z, gn, wt, wbm, g, wa, wb, wo, gfin, final_norm, tm):
    bsz, seq, d = x.shape
    half = yc.shape[-1]
    const2 = lambda b, i: (0, 0)
    x_spec, mod_spec = _token_specs(d, tm, per_batch_mod)
    half_spec = pl.BlockSpec((1, tm, half), lambda b, i: (b, i, 0))
    return pl.pallas_call(
        functools.partial(_mix_ffn2_kernel, final_norm=final_norm),
        grid=(bsz, seq // tm),
        in_specs=[x_spec, mod_spec] + [half_spec] * 4
                 + [pl.BlockSpec((1, HEAD_DIM), const2),
                    pl.BlockSpec((half, d), const2, pipeline_mode=pl.Buffered(1)),
                    pl.BlockSpec((half, d), const2, pipeline_mode=pl.Buffered(1)),
                    pl.BlockSpec((1, d), const2)]
                 + _ffn_weight_specs(wa, wo) + [pl.BlockSpec((1, d), const2)],
        out_specs=x_spec,
        out_shape=jax.ShapeDtypeStruct(x.shape, F32),
        scratch_shapes=[pltpu.VMEM((tm, d), BF16), pltpu.VMEM((tm, d), F32), pltpu.VMEM((tm, d), F32)],
        compiler_params=_cparams("parallel", "arbitrary"),
        name="mix_out_ffn2",
    )(x, mod, yc, o_f, o_b, z, gn, wt, wbm, g, wa, wb, wo, gfin)


def _proj_kernel(x_ref, mod_ref, g_ref, wc_ref, wq_ref, wz_ref, wab_ref, pc_ref, pq_ref, pz_ref, pab_ref):
    x = x_ref[0]
    sh = mod_ref[0, 3:4, :]
    sc = mod_ref[0, 4:5, :]
    h = (_rms(x) * g_ref[...] * (1.0 + sc) + sh).astype(BF16)
    pc_ref[0] = jnp.dot(h, wc_ref[...], preferred_element_type=F32)
    pq_ref[0] = jnp.dot(h, wq_ref[...], preferred_element_type=F32)
    pz_ref[0] = jnp.dot(h, wz_ref[...], preferred_element_type=F32)
    pab_ref[0] = jnp.dot(h, wab_ref[...], preferred_element_type=F32)


def _proj(x, mod, per_batch_mod, g, wc, wq, wz, wab, tm):
    bsz, seq, d = x.shape
    mod_map = (lambda b, i: (b, 0, 0)) if per_batch_mod else (lambda b, i: (0, 0, 0))
    const2 = lambda b, i: (0, 0)
    tok = lambda b, i: (b, i, 0)
    widths = (wc.shape[1], wq.shape[1], wz.shape[1], wab.shape[1])
    return pl.pallas_call(
        _proj_kernel,
        grid=(bsz, seq // tm),
        in_specs=[pl.BlockSpec((1, tm, d), tok),
                  pl.BlockSpec((1, N_MOD, d), mod_map),
                  pl.BlockSpec((1, d), const2)]
                 + [pl.BlockSpec((d, w), const2, pipeline_mode=pl.Buffered(1)) for w in widths],
        out_specs=[pl.BlockSpec((1, tm, w), tok) for w in widths],
        out_shape=[jax.ShapeDtypeStruct((bsz, seq, w), F32) for w in widths],
        compiler_params=_cparams("parallel", "arbitrary"),
        name="mix_in_proj",
    )(x, mod, g, wc, wq, wz, wab)


def _shift_rows(cur, prev_halo, next_halo, dist, first, last):
    tm = cur.shape[0]
    hp = jnp.where(first, 0.0, prev_halo)
    hn = jnp.where(last, 0.0, next_halo)
    if dist % 8 == 0:
        prev = jnp.concatenate([hp[hp.shape[0] - dist:], cur[:tm - dist]], axis=0)
        nxt = jnp.concatenate([cur[dist:], hn[:dist]], axis=0)
        return prev, nxt
    assert dist == 1
    row = lax.broadcasted_iota(jnp.int32, cur.shape, 0)
    prev = jnp.where(row == 0, hp[hp.shape[0] - 1:], pltpu.roll(cur, 1, axis=0))
    nxt = jnp.where(row == tm - 1, hn[:1], pltpu.roll(cur, tm - 1, axis=0))
    return prev, nxt


def _prep_kernel(pc_ref, pcp_ref, pcn_ref, pq_ref, pqp_ref, pqn_ref, pab_ref,
                 cw_ref, dw_ref, alog_ref, dtb_ref,
                 yc_ref, q_ref, k_ref, v_ref, gb_ref, *, conv_dist, n_tiles):
    i = pl.program_id(1)
    first = i == 0
    last = i == n_tiles - 1
    cwid = yc_ref.shape[-1]
    kwid = q_ref.shape[-1]

    pc = pc_ref[0]
    u = pc[:, cwid:2 * cwid] * pc[:, 2 * cwid:]
    pcp = pcp_ref[0]
    pcn = pcn_ref[0]
    up, un = _shift_rows(u, pcp[:, cwid:2 * cwid] * pcp[:, 2 * cwid:],
                         pcn[:, cwid:2 * cwid] * pcn[:, 2 * cwid:], conv_dist, first, last)
    cw = cw_ref[...]
    yc_ref[0] = pc[:, :cwid] * (up * cw[0:1] + u * cw[1:2] + un * cw[2:3])

    pq = pq_ref[0]
    qp, qn = _shift_rows(pq, pqp_ref[0], pqn_ref[0], 1, first, last)
    dw = dw_ref[...]
    qkv = _silu(qp * dw[0:1] + pq * dw[1:2] + qn * dw[2:3])
    for h in range(N_HEADS):
        sl = slice(h * HEAD_DIM, (h + 1) * HEAD_DIM)
        qh = qkv[:, sl]
        kh = qkv[:, kwid + h * HEAD_DIM: kwid + (h + 1) * HEAD_DIM]
        q_ref[0, :, sl] = qh * (lax.rsqrt(jnp.sum(qh * qh, axis=-1, keepdims=True) + EPS) * HEAD_DIM ** -0.5)
        k_ref[0, :, sl] = kh * lax.rsqrt(jnp.sum(kh * kh, axis=-1, keepdims=True) + EPS)
    v_ref[0] = qkv[:, 2 * kwid:]

    ab = pab_ref[0]
    lane = lax.broadcasted_iota(jnp.int32, ab.shape, 1)
    is_gf = lane < N_HEADS
    is_gb = (lane >= 2 * N_HEADS) & (lane < 3 * N_HEADS)
    xg = ab + dtb_ref[...]
    softplus = jnp.maximum(xg, 0.0) + jnp.log1p(jnp.exp(-jnp.abs(xg)))
    gates = jnp.where(is_gf | is_gb, -jnp.exp(alog_ref[...]) * softplus, _sigmoid(ab))
    r = lax.broadcasted_iota(jnp.int32, (CHUNK, CHUNK), 0)
    c = lax.broadcasted_iota(jnp.int32, (CHUNK, CHUNK), 1)
    tril = (r >= c).astype(BF16)
    triu = (r <= c).astype(BF16)
    g_hi = gates.astype(BF16)
    rem = gates - g_hi.astype(F32)
    g_mid = rem.astype(BF16)
    g_lo = (rem - g_mid.astype(F32)).astype(BF16)
    d = lambda m, x: jnp.dot(m, x, preferred_element_type=F32)
    lane_c = lax.broadcasted_iota(jnp.int32, (CHUNK, ab.shape[1]), 1)
    is_gf_c = lane_c < N_HEADS
    is_gb_c = (lane_c >= 2 * N_HEADS) & (lane_c < 3 * N_HEADS)
    for t in range(ab.shape[0] // CHUNK):
        rs = slice(t * CHUNK, (t + 1) * CHUNK)
        parts = (g_hi[rs], g_mid[rs], g_lo[rs])
        pre = d(tril, parts[0]) + (d(tril, parts[1]) + d(tril, parts[2]))
        suf = d(triu, parts[0]) + (d(triu, parts[1]) + d(triu, parts[2]))
        gb_ref[0, rs, :] = jnp.where(is_gf_c, pre, jnp.where(is_gb_c, suf, gates[rs]))


def _prep(pc, pq, pab, conv_w, dn_conv_w, alog_row, dtb_row, conv_dist, tm):
    bsz, seq, wc3 = pc.shape
    wq3 = pq.shape[-1]
    cwid, kwid = wc3 // 3, wq3 // 3
    n_tiles = seq // tm
    hc = max(conv_dist, 8)
    hq = 8
    tok = lambda b, i: (b, i, 0)
    const2 = lambda b, i: (0, 0)

    def prev_map(hrows):
        return lambda b, i: (b, jnp.maximum(i * (tm // hrows) - 1, 0), 0)

    def next_map(hrows):
        return lambda b, i: (b, jnp.minimum((i + 1) * (tm // hrows), seq // hrows - 1), 0)

    outs = [jax.ShapeDtypeStruct((bsz, seq, w), F32) for w in (cwid, kwid, kwid, kwid, pab.shape[-1])]
    return pl.pallas_call(
        functools.partial(_prep_kernel, conv_dist=conv_dist, n_tiles=n_tiles),
        grid=(bsz, n_tiles),
        in_specs=[pl.BlockSpec((1, tm, wc3), tok),
                  pl.BlockSpec((1, hc, wc3), prev_map(hc)),
                  pl.BlockSpec((1, hc, wc3), next_map(hc)),
                  pl.BlockSpec((1, tm, wq3), tok),
                  pl.BlockSpec((1, hq, wq3), prev_map(hq)),
                  pl.BlockSpec((1, hq, wq3), next_map(hq)),
                  pl.BlockSpec((1, tm, pab.shape[-1]), tok),
                  pl.BlockSpec(conv_w.shape, const2),
                  pl.BlockSpec(dn_conv_w.shape, const2),
                  pl.BlockSpec(alog_row.shape, const2),
                  pl.BlockSpec(dtb_row.shape, const2)],
        out_specs=[pl.BlockSpec((1, tm, o.shape[-1]), tok) for o in outs],
        out_shape=outs,
        compiler_params=_cparams("parallel", "arbitrary"),
        name="mixer_prep",
    )(pc, pc, pc, pq, pq, pq, pab, conv_w, dn_conv_w, alog_row, dtb_row)


def _pair_block_diag(y16, keep_left, keep_right):
    return jnp.concatenate([y16 * keep_left, y16 * keep_right], axis=0)


def _pair_inverses(a_list, eye_pair, keep_left, keep_right):
    d = lambda x, y: jnp.dot(x, y, preferred_element_type=F32)
    ps = [-a for a in a_list]
    ts = [eye_pair + p for p in ps]
    n_sq = CHUNK.bit_length() - 2
    for i in range(n_sq + 1):
        new_ps, new_ts = [], []
        for p, t in zip(ps, ts):
            p_hi, p_lo = _split_bf16(p)
            r_hi = _pair_block_diag(p_hi, keep_left, keep_right)
            r_lo = _pair_block_diag(p_lo, keep_left, keep_right)
            r_hh = jnp.concatenate([r_hi, r_hi], axis=0)
            if i == 0:
                new_ps.append(d(jnp.concatenate([p_hi, p_lo], axis=1), r_hh) + d(p_hi, r_lo))
                new_ts.append(t)
                continue
            t_hi, t_lo = _split_bf16(t)
            if i == n_sq:
                new_ps.append(p)
                new_ts.append(t + (d(jnp.concatenate([t_hi, t_lo], axis=1), r_hh) + d(t_hi, r_lo)))
                continue
            l1 = jnp.concatenate([jnp.concatenate([p_hi, p_lo], axis=1),
                                  jnp.concatenate([t_hi, t_lo], axis=1)], axis=0)
            l2 = jnp.concatenate([p_hi, t_hi], axis=0)
            res = d(l1, r_hh) + d(l2, r_lo)
            new_ps.append(res[:CHUNK])
            new_ts.append(t + res[CHUNK:])
        ps, ts = new_ps, new_ts
    return ts


def _chunk_prep_kernel(q_ref, k_ref, v_ref, g_ref,
                       uf_ref, ub_ref, wf_ref, wb_ref, qgf_ref, qgb_ref, kgf_ref, kgb_ref,
                       qkf_ref, qkb_ref, glf_ref, glb_ref):
    u_refs, w_refs, qg_refs = (uf_ref, ub_ref), (wf_ref, wb_ref), (qgf_ref, qgb_ref)
    kg_refs, qk_refs, gl_refs = (kgf_ref, kgb_ref), (qkf_ref, qkb_ref), (glf_ref, glb_ref)
    row = lax.broadcasted_iota(jnp.int32, (CHUNK, 2 * CHUNK), 0)
    lane = lax.broadcasted_iota(jnp.int32, (CHUNK, 2 * CHUNK), 1)
    col = lane & (CHUNK - 1)
    left = lane < CHUNK
    keep_left = jnp.where(left, 1.0, 0.0).astype(BF16)
    keep_right = jnp.where(left, 0.0, 1.0).astype(BF16)
    eye_pair = jnp.where(row == col, 1.0, 0.0)
    zeros16 = jnp.zeros((CHUNK, HEAD_DIM), BF16)
    dot = lambda x, y: jnp.dot(x, y, preferred_element_type=F32)

    chains, a_list = [], []
    for c in range(PREP_CHUNKS):
        rs = slice(c * CHUNK, (c + 1) * CHUNK)
        gates = g_ref[0, rs, :]
        for p in range(N_PAIRS):
            sl = slice(2 * p * HEAD_DIM, 2 * (p + 1) * HEAD_DIM)
            q2, k2, v2 = q_ref[0, rs, sl], k_ref[0, rs, sl], v_ref[0, rs, sl]
            k16 = k2.astype(BF16)
            k_bd = jnp.concatenate([jnp.concatenate([k16[:, :HEAD_DIM], zeros16], axis=1),
                                    jnp.concatenate([zeros16, k16[:, HEAD_DIM:]], axis=1)], axis=0)
            kq = jnp.concatenate([k16, q2.astype(BF16)], axis=0)
            gram = lax.dot_general(kq, k_bd, (((1,), (1,)), ((), ())), preferred_element_type=F32)
            for d in range(N_DIRS):
                gcol, bcol = 2 * d * N_HEADS, (2 * d + 1) * N_HEADS
                gc = [gates[:, gcol + 2 * p + e:gcol + 2 * p + e + 1] for e in range(2)]
                beta = [gates[:, bcol + 2 * p + e:bcol + 2 * p + e + 1] for e in range(2)]
                gc_pair = jnp.where(left, gc[0], gc[1])
                beta_pair = jnp.where(left, beta[0], beta[1])
                gr_pair = jnp.sum(jnp.where(row == col, gc_pair, 0.0), axis=0, keepdims=True)
                if d == 0:
                    incl, strict, last = row >= col, row > col, CHUNK - 1
                else:
                    incl, strict, last = row <= col, row < col, 0
                decay = jnp.where(incl, jnp.exp(jnp.where(incl, gc_pair - gr_pair, 0.0)), 0.0)
                a_list.append(jnp.where(strict, beta_pair * gram[:CHUNK] * decay, 0.0))
                qk_refs[d][0, rs, p * HEAD_DIM:(p + 1) * HEAD_DIM] = (
                    jnp.where(incl, gram[CHUNK:] * decay, 0.0).astype(BF16))
                chains.append((c, rs, p, d, q2, k2, v2, gc, beta, last))

    t_list = _pair_inverses(a_list, eye_pair, keep_left, keep_right)

    for (c, rs, p, d, q2, k2, v2, gc, beta, last), t in zip(chains, t_list):
        t16 = t.astype(BF16)
        for e in range(2):
            h = 2 * p + e
            sl = slice(e * HEAD_DIM, (e + 1) * HEAD_DIM)
            hs = slice(h * HEAD_DIM, (h + 1) * HEAD_DIM)
            eg = jnp.exp(gc[e])
            g_last = gc[e][last:last + 1, :]
            rhs = jnp.concatenate([v2[:, sl] * beta[e], k2[:, sl] * (beta[e] * eg)], axis=1).astype(BF16)
            zpad = jnp.zeros_like(rhs)
            uw = dot(t16, jnp.concatenate([rhs, zpad] if e == 0 else [zpad, rhs], axis=0))
            u_refs[d][0, rs, hs] = uw[:, :HEAD_DIM]
            w_refs[d][0, rs, hs] = uw[:, HEAD_DIM:].astype(BF16)
            qg_refs[d][0, rs, hs] = (q2[:, sl] * eg).astype(BF16)
            kg_refs[d][0, rs, hs] = (k2[:, sl] * jnp.exp(g_last - gc[e])).astype(BF16)
            gl_refs[d][0, c, h:h + 1, :] = jnp.broadcast_to(jnp.exp(g_last), (1, HEAD_DIM))


def _chunk_prep(q, k, v, gb):
    bsz, seq, kwid = q.shape
    rows = PREP_CHUNKS * CHUNK
    n = seq // CHUNK
    tok = lambda b, i: (b, i, 0)
    f32_full = jax.ShapeDtypeStruct(q.shape, F32)
    bf_full = jax.ShapeDtypeStruct(q.shape, BF16)
    bf_half = jax.ShapeDtypeStruct((bsz, seq, kwid // 2), BF16)
    gl = jax.ShapeDtypeStruct((bsz, n, N_HEADS, HEAD_DIM), F32)
    outs = [f32_full] * 2 + [bf_full] * 6 + [bf_half] * 2 + [gl] * 2
    out_specs = ([pl.BlockSpec((1, rows, kwid), tok)] * 8 + [pl.BlockSpec((1, rows, kwid // 2), tok)] * 2
                 + [pl.BlockSpec((1, PREP_CHUNKS, N_HEADS, HEAD_DIM), lambda b, i: (b, i, 0, 0))] * 2)
    return pl.pallas_call(
        _chunk_prep_kernel,
        grid=(bsz, seq // rows),
        in_specs=[pl.BlockSpec((1, rows, kwid), tok)] * 3 + [pl.BlockSpec((1, rows, gb.shape[-1]), tok)],
        out_specs=out_specs,
        out_shape=outs,
        compiler_params=_cparams("parallel", "parallel"),
        name="delta_chunk_prep",
    )(q, k, v, gb)


def _scan_kernel(uf_ref, wf_ref, qgf_ref, kgf_ref, qkf_ref, glf_ref,
                 ub_ref, wb_ref, qgb_ref, kgb_ref, qkb_ref, glb_ref, s0f_ref, s0b_ref,
                 of_ref, ob_ref, sff_ref, sfb_ref, s_ref, *, n_chunks, has_s0):
    j = pl.program_id(1)
    n_hd = N_DIRS * N_HEADS

    @pl.when(j == 0)
    def _():
        for r in range(SCAN_ROWS):
            if has_s0:
                s_ref[r * n_hd:r * n_hd + N_HEADS] = s0f_ref[r]
                s_ref[r * n_hd + N_HEADS:(r + 1) * n_hd] = s0b_ref[r]
            else:
                s_ref[r * n_hd:(r + 1) * n_hd] = jnp.zeros((n_hd, HEAD_DIM, HEAD_DIM), F32)

    dot = lambda x, y: jnp.dot(x, y, preferred_element_type=F32)
    zeros16 = jnp.zeros((CHUNK, HEAD_DIM), BF16)
    ins = ((uf_ref, wf_ref, qgf_ref, kgf_ref, qkf_ref, glf_ref), (ub_ref, wb_ref, qgb_ref, kgb_ref, qkb_ref, glb_ref))
    outs = (of_ref, ob_ref)
    hds = [(r, d, h) for r in range(SCAN_ROWS) for d in range(N_DIRS) for h in range(N_HEADS)]
    hsl = lambda h: slice(h * HEAD_DIM, (h + 1) * HEAD_DIM)
    sidx = lambda r, d, h: r * n_hd + d * N_HEADS + h

    s_old, ws_qs = [], []
    for r, d, h in hds:
        s = s_ref[sidx(r, d, h)]
        s_old.append(s)
        lhs = jnp.concatenate([ins[d][1][r, :, hsl(h)], ins[d][2][r, :, hsl(h)]], axis=0)
        ws_qs.append(dot(lhs, s.astype(BF16)))
    v_new = []
    for i, (r, d, h) in enumerate(hds):
        vn = (ins[d][0][r, :, hsl(h)] - ws_qs[i][:CHUNK]).astype(BF16)
        v_new.append(vn)
        v_pad = jnp.concatenate([vn, zeros16] if h % 2 == 0 else [zeros16, vn], axis=0)
        qk_pair = ins[d][4][r, :, hsl(h // 2)]
        outs[d][r, :, hsl(h)] = ws_qs[i][CHUNK:] + dot(qk_pair, v_pad)
    for i, (r, d, h) in enumerate(hds):
        kg = ins[d][3][r, :, hsl(h)]
        upd = lax.dot_general(kg, v_new[i], (((0,), (0,)), ((), ())), preferred_element_type=F32)
        s_ref[sidx(r, d, h)] = s_old[i] * ins[d][5][r, 0, h:h + 1, :] + upd

    @pl.when(j == n_chunks - 1)
    def _():
        for r in range(SCAN_ROWS):
            sff_ref[r] = s_ref[r * n_hd:r * n_hd + N_HEADS]
            sfb_ref[r] = s_ref[r * n_hd + N_HEADS:(r + 1) * n_hd]


def _scan(prep, s0_f, s0_b):
    (u_f, u_b, w_f, w_b, qg_f, qg_b, kg_f, kg_b, qk_f, qk_b, gl_f, gl_b) = prep
    bsz, seq, kwid = u_f.shape
    assert bsz % SCAN_ROWS == 0
    n = seq // CHUNK
    has_s0 = s0_f is not None
    sblk = (SCAN_ROWS, N_HEADS, HEAD_DIM, HEAD_DIM)
    if not has_s0:
        s0_f = s0_b = jnp.zeros(sblk, F32)
    fwd = lambda b, j: (b, j, 0)
    bwd = lambda b, j: (b, n - 1 - j, 0)
    fwd4 = lambda b, j: (b, j, 0, 0)
    bwd4 = lambda b, j: (b, n - 1 - j, 0, 0)
    smap = (lambda b, j: (b, 0, 0, 0)) if has_s0 else (lambda b, j: (0, 0, 0, 0))

    def dir_specs(tok, tok4):
        return ([pl.BlockSpec((SCAN_ROWS, CHUNK, kwid), tok)] * 4
                + [pl.BlockSpec((SCAN_ROWS, CHUNK, kwid // 2), tok),
                   pl.BlockSpec((SCAN_ROWS, 1, N_HEADS, HEAD_DIM), tok4)])

    return pl.pallas_call(
        functools.partial(_scan_kernel, n_chunks=n, has_s0=has_s0),
        grid=(bsz // SCAN_ROWS, n),
        in_specs=dir_specs(fwd, fwd4) + dir_specs(bwd, bwd4) + [pl.BlockSpec(sblk, smap)] * 2,
        out_specs=[pl.BlockSpec((SCAN_ROWS, CHUNK, kwid), fwd),
                   pl.BlockSpec((SCAN_ROWS, CHUNK, kwid), bwd),
                   pl.BlockSpec(sblk, lambda b, j: (b, 0, 0, 0)),
                   pl.BlockSpec(sblk, lambda b, j: (b, 0, 0, 0))],
        out_shape=[jax.ShapeDtypeStruct(u_f.shape, F32)] * 2
                  + [jax.ShapeDtypeStruct((bsz,) + sblk[1:], F32)] * 2,
        scratch_shapes=[pltpu.VMEM((SCAN_ROWS * N_DIRS * N_HEADS, HEAD_DIM, HEAD_DIM), F32)],
        compiler_params=_cparams("parallel", "arbitrary"),
        name="delta_scan",
    )(u_f, w_f, qg_f, kg_f, qk_f, gl_f, u_b, w_b, qg_b, kg_b, qk_b, gl_b, s0_f, s0_b)


def _layer(x, mod, per_batch_mod, s0_f, s0_b, on_grid, lw, norm_final, final_norm, tm, fold):
    (g1, wa1, wb1, wo1, gm, wc, wq, wz, wab, conv_w, dn_conv_w, alog_row, dtb_row, gn, wt, wbo,
     g2, wa2, wb2, wo2) = lw
    bsz, seq, d = x.shape
    assert fold == 1 or not per_batch_mod
    folded = lambda a: a.reshape(bsz // fold, fold * seq, a.shape[-1])
    unfolded = lambda a: a.reshape(bsz, seq, a.shape[-1])
    x = _ffn1(folded(x), mod, per_batch_mod, g1, wa1, wb1, wo1, tm)
    pc, pq, pz, pab = _proj(x, mod, per_batch_mod, gm, wc, wq, wz, wab, tm)
    yc, q, k, v, gb = _prep(unfolded(pc), unfolded(pq), unfolded(pab), conv_w, dn_conv_w, alog_row, dtb_row,
                            GRID_W if on_grid else 1, min(tm, seq))
    o_f, o_b, s_f, s_b = _scan(_chunk_prep(q, k, v, gb), s0_f, s0_b)
    y = _mix_ffn2(x, mod, per_batch_mod, folded(yc), folded(o_f), folded(o_b), pz, gn, wt, wbo,
                  g2, wa2, wb2, wo2, norm_final, final_norm, tm)
    return unfolded(y), s_f, s_b


def _ffn_weights(w_in, w_out):
    d, two_ff = w_in.shape
    ff = two_ff // 2
    nck = ff // FF_CHUNK
    w = w_in.astype(BF16).reshape(d, 2, nck, FF_CHUNK).transpose(1, 2, 0, 3)
    return w[0], w[1], w_out.astype(BF16).reshape(nck, FF_CHUNK, d)


def kernel(x_prompt, x_sample, state_dn_fwd, state_dn_bwd, c, c_ctx, w_ada, b_ada, norm_ffn1, w_ffn1_in,
           w_ffn1_out, norm_mix, w_mix_in, conv_w, dn_conv_w, dn_a_log, dn_dt_bias, dn_norm, w_mix_out,
           norm_ffn2, w_ffn2_in, w_ffn2_out, norm_final):
    depth = w_ada.shape[0]
    d = x_prompt.shape[-1]
    n_lat = c.shape[0]
    cwid = conv_w.shape[-1]
    kwid = N_HEADS * HEAD_DIM
    n_gate = 4 * N_HEADS
    row = lambda a: a.reshape(1, -1)

    cvec = jnp.concatenate([c_ctx[None, :], c, jnp.zeros((16 - 1 - n_lat, d), F32)], axis=0)

    xp, xs = x_prompt, x_sample
    new_f, new_b = [], []
    for l in range(depth):
        mod = _modulation(cvec, w_ada[l], b_ada[l]).reshape(16, N_MOD, d)
        mod_ctx, mod_lat = mod[0:1], mod[1:1 + n_lat]

        wa1, wb1, wo1 = _ffn_weights(w_ffn1_in[l], w_ffn1_out[l])
        wa2, wb2, wo2 = _ffn_weights(w_ffn2_in[l], w_ffn2_out[l])
        wm = w_mix_in[l].astype(BF16)
        wc = wm[:, :3 * cwid]
        wq = wm[:, 3 * cwid:3 * cwid + 3 * kwid]
        wz = wm[:, 3 * cwid + 3 * kwid:3 * cwid + 4 * kwid]
        wab = jnp.pad(wm[:, 3 * cwid + 4 * kwid:], ((0, 0), (0, HEAD_DIM - n_gate)))
        zpad = jnp.zeros((N_HEADS,), F32)
        alog_row = jnp.concatenate([dn_a_log[l, 0], zpad, dn_a_log[l, 1], zpad,
                                    jnp.zeros((HEAD_DIM - n_gate,), F32)])[None, :]
        dtb_row = jnp.concatenate([dn_dt_bias[l, 0], zpad, dn_dt_bias[l, 1], zpad,
                                   jnp.zeros((HEAD_DIM - n_gate,), F32)])[None, :]
        wo = w_mix_out[l].astype(BF16)
        lw = (row(norm_ffn1[l]), wa1, wb1, wo1, row(norm_mix[l]), wc, wq, wz, wab, conv_w[l], dn_conv_w[l],
              alog_row, dtb_row, row(dn_norm[l]), wo[:cwid], wo[cwid:], row(norm_ffn2[l]), wa2, wb2, wo2)
        last = l == depth - 1
        xp, sf, sb = _layer(xp, mod_ctx, False, None, None, False, lw, row(norm_final), last,
                            TOKEN_TILE, TOKEN_TILE // xp.shape[1])
        new_f.append(sf)
        new_b.append(sb)
        xs, _, _ = _layer(xs, mod_lat, True, state_dn_fwd[:, l], state_dn_bwd[:, l], True, lw,
                          row(norm_final), last, TOKEN_TILE, 1)
    return xp, xs, jnp.stack(new_f, axis=1), jnp.stack(new_b, axis=1)
```

```python
import functools

import jax
import jax.numpy as jnp
from jax import lax
from jax.experimental import pallas as pl
from jax.experimental.pallas import tpu as pltpu

F32 = jnp.float32
BF16 = jnp.bfloat16

EPS = 1e-6
CHUNK = 64
GRID_W = 64
N_HEADS = 4
HEAD_DIM = 128
N_PAIRS = N_HEADS // 2
N_DIRS = 2
N_MOD = 9
FF_CHUNK = 256
TOKEN_TILE = 512
PREP_CHUNKS = 4
SCAN_ROWS = 4
VMEM_LIMIT = 56 * 1024 * 1024


def _cparams(*sem):
    return pltpu.CompilerParams(dimension_semantics=sem, vmem_limit_bytes=VMEM_LIMIT)


def _bdot(a, b):
    return jnp.dot(a.astype(BF16), b.astype(BF16), preferred_element_type=F32)


def _split_bf16(a):
    hi = a.astype(BF16)
    lo = (a - hi.astype(F32)).astype(BF16)
    return hi, lo


def _dot_x3(a, b):
    a_hi, a_lo = _split_bf16(a)
    b_hi, b_lo = _split_bf16(b)
    d = lambda x, y: jnp.dot(x, y, preferred_element_type=F32)
    return d(a_hi, b_hi) + (d(a_hi, b_lo) + d(a_lo, b_hi))


def _sigmoid(x):
    return 1.0 / (1.0 + jnp.exp(-x))


def _silu(x):
    return x * _sigmoid(x)


def _rms(x):
    return x * lax.rsqrt(jnp.mean(x * x, axis=-1, keepdims=True) + EPS)


def _mod_kernel(c_ref, w_ref, b_ref, o_ref):
    s = _silu(c_ref[...])
    o_ref[...] = _dot_x3(s, w_ref[...]) + b_ref[...]


def _modulation(cvec, w_ada, b_ada):
    rows, d = cvec.shape
    n = w_ada.shape[1]
    tn = 1536
    return pl.pallas_call(
        _mod_kernel,
        grid=(n // tn,),
        in_specs=[pl.BlockSpec((rows, d), lambda i: (0, 0)),
                  pl.BlockSpec((d, tn), lambda i: (0, i)),
                  pl.BlockSpec((1, tn), lambda i: (0, i))],
        out_specs=pl.BlockSpec((rows, tn), lambda i: (0, i)),
        out_shape=jax.ShapeDtypeStruct((rows, n), F32),
        compiler_params=_cparams("arbitrary"),
        name="modulation",
    )(cvec, w_ada, b_ada.reshape(1, n))


def _swiglu_residual(x, x_keep_ref, mod_ref, mod_row, g_ref, win_ref, wo_ref, h_ref, acc_ref):
    sh = mod_ref[0, mod_row:mod_row + 1, :]
    sc = mod_ref[0, mod_row + 1:mod_row + 2, :]
    gt = mod_ref[0, mod_row + 2:mod_row + 3, :]
    h_ref[...] = (_rms(x) * g_ref[...] * (1.0 + sc) + sh).astype(BF16)
    acc_ref[...] = jnp.zeros_like(acc_ref)
    ff = wo_ref.shape[0]
    for c in range(ff // FF_CHUNK):
        lo, hi = c * FF_CHUNK, (c + 1) * FF_CHUNK
        h = h_ref[...]
        a = jnp.dot(h, win_ref[:, lo:hi], preferred_element_type=F32)
        b = jnp.dot(h, win_ref[:, ff + lo:ff + hi], preferred_element_type=F32)
        act = (_silu(a) * b).astype(BF16)
        acc_ref[...] += jnp.dot(act, wo_ref[lo:hi, :], preferred_element_type=F32)
    return x_keep_ref[...] + 0.5 * gt * acc_ref[...]


def _ffn1_kernel(x_ref, mod_ref, g_ref, win_ref, wo_ref, o_ref, h_ref, acc_ref):
    o_ref[0] = _swiglu_residual(x_ref[0], x_ref.at[0], mod_ref, 0, g_ref, win_ref, wo_ref, h_ref, acc_ref)


def _mix_ffn2_kernel(x_ref, mod_ref, yc_ref, of_ref, ob_ref, z_ref, gn_ref, wt_ref, wbm_ref,
                     g_ref, win_ref, wo_ref, gfin_ref, o_ref, h_ref, acc_ref, x2_ref, *, final_norm):
    o = of_ref[0] + ob_ref[0]
    gate = _silu(z_ref[0])
    gn = gn_ref[...]
    ys = []
    for h in range(N_HEADS):
        sl = slice(h * HEAD_DIM, (h + 1) * HEAD_DIM)
        ys.append(_rms(o[:, sl]) * gn * gate[:, sl])
    y_dn = jnp.concatenate(ys, axis=1)
    m = _bdot(yc_ref[0], wt_ref[...]) + _bdot(y_dn, wbm_ref[...])
    x2_ref[...] = x_ref[0] + mod_ref[0, 5:6, :] * m
    y = _swiglu_residual(x2_ref[...], x2_ref, mod_ref, 6, g_ref, win_ref, wo_ref, h_ref, acc_ref)
    if final_norm:
        y = _rms(y) * gfin_ref[...]
    o_ref[0] = y


def _token_specs(d, tm, per_batch_mod):
    mod_map = (lambda b, i: (b, 0, 0)) if per_batch_mod else (lambda b, i: (0, 0, 0))
    return pl.BlockSpec((1, tm, d), lambda b, i: (b, i, 0)), pl.BlockSpec((1, N_MOD, d), mod_map)


def _ffn_weight_specs(win, wo):
    assert win.shape[1] == 2 * wo.shape[0] and wo.shape[0] % FF_CHUNK == 0
    const2 = lambda b, i: (0, 0)
    return [pl.BlockSpec(win.shape, const2, pipeline_mode=pl.Buffered(1)),
            pl.BlockSpec(wo.shape, const2, pipeline_mode=pl.Buffered(1))]


def _ffn1(x, mod, per_batch_mod, g, win, wo, tm):
    bsz, seq, d = x.shape
    const2 = lambda b, i: (0, 0)
    x_spec, mod_spec = _token_specs(d, tm, per_batch_mod)
    return pl.pallas_call(
        _ffn1_kernel,
        grid=(bsz, seq // tm),
        in_specs=[x_spec, mod_spec, pl.BlockSpec((1, d), const2)] + _ffn_weight_specs(win, wo),
        out_specs=x_spec,
        out_shape=jax.ShapeDtypeStruct(x.shape, F32),
        scratch_shapes=[pltpu.VMEM((tm, d), BF16), pltpu.VMEM((tm, d), F32)],
        compiler_params=_cparams("parallel", "arbitrary"),
        name="ffn1",
    )(x, mod, g, win, wo)


def _mix_ffn2(x, mod, per_batch_mod, yc, o_f, o_b, z, gn, wt, wbm, g, win, wo, gfin, final_norm, tm):
    bsz, seq, d = x.shape
    half = yc.shape[-1]
    const2 = lambda b, i: (0, 0)
    x_spec, mod_spec = _token_specs(d, tm, per_batch_mod)
    half_spec = pl.BlockSpec((1, tm, half), lambda b, i: (b, i, 0))
    return pl.pallas_call(
        functools.partial(_mix_ffn2_kernel, final_norm=final_norm),
        grid=(bsz, seq // tm),
        in_specs=[x_spec, mod_spec] + [half_spec] * 4
                 + [pl.BlockSpec((1, HEAD_DIM), const2),
                    pl.BlockSpec((half, d), const2, pipeline_mode=pl.Buffered(1)),
                    pl.BlockSpec((half, d), const2, pipeline_mode=pl.Buffered(1)),
                    pl.BlockSpec((1, d), const2)]
                 + _ffn_weight_specs(win, wo) + [pl.BlockSpec((1, d), const2)],
        out_specs=x_spec,
        out_shape=jax.ShapeDtypeStruct(x.shape, F32),
        scratch_shapes=[pltpu.VMEM((tm, d), BF16), pltpu.VMEM((tm, d), F32), pltpu.VMEM((tm, d), F32)],
        compiler_params=_cparams("parallel", "arbitrary"),
        name="mix_out_ffn2",
    )(x, mod, yc, o_f, o_b, z, gn, wt, wbm, g, win, wo, gfin)


def _proj_kernel(x_ref, mod_ref, g_ref, wc_ref, wq_ref, wz_ref, wab_ref, pc_ref, pq_ref, pz_ref, pab_ref):
    x = x_ref[0]
    sh = mod_ref[0, 3:4, :]
    sc = mod_ref[0, 4:5, :]
    h = (_rms(x) * g_ref[...] * (1.0 + sc) + sh).astype(BF16)
    pc_ref[0] = jnp.dot(h, wc_ref[...], preferred_element_type=F32)
    pq_ref[0] = jnp.dot(h, wq_ref[...], preferred_element_type=F32)
    pz_ref[0] = jnp.dot(h, wz_ref[...], preferred_element_type=F32)
    pab_ref[0] = jnp.dot(h, wab_ref[...], preferred_element_type=F32)


def _proj(x, mod, per_batch_mod, g, wc, wq, wz, wab, tm):
    bsz, seq, d = x.shape
    mod_map = (lambda b, i: (b, 0, 0)) if per_batch_mod else (lambda b, i: (0, 0, 0))
    const2 = lambda b, i: (0, 0)
    tok = lambda b, i: (b, i, 0)
    widths = (wc.shape[1], wq.shape[1], wz.shape[1], wab.shape[1])
    return pl.pallas_call(
        _proj_kernel,
        grid=(bsz, seq // tm),
        in_specs=[pl.BlockSpec((1, tm, d), tok),
                  pl.BlockSpec((1, N_MOD, d), mod_map),
                  pl.BlockSpec((1, d), const2)]
                 + [pl.BlockSpec((d, w), const2, pipeline_mode=pl.Buffered(1)) for w in widths],
        out_specs=[pl.BlockSpec((1, tm, w), tok) for w in widths],
        out_shape=[jax.ShapeDtypeStruct((bsz, seq, w), F32) for w in widths],
        compiler_params=_cparams("parallel", "arbitrary"),
        name="mix_in_proj",
    )(x, mod, g, wc, wq, wz, wab)


def _shift_rows(cur, prev_halo, next_halo, dist, first, last):
    tm = cur.shape[0]
    hp = jnp.where(first, 0.0, prev_halo)
    hn = jnp.where(last, 0.0, next_halo)
    if dist % 8 == 0:
        prev = jnp.concatenate([hp[hp.shape[0] - dist:], cur[:tm - dist]], axis=0)
        nxt = jnp.concatenate([cur[dist:], hn[:dist]], axis=0)
        return prev, nxt
    assert dist == 1
    row = lax.broadcasted_iota(jnp.int32, cur.shape, 0)
    prev = jnp.where(row == 0, hp[hp.shape[0] - 1:], pltpu.roll(cur, 1, axis=0))
    nxt = jnp.where(row == tm - 1, hn[:1], pltpu.roll(cur, tm - 1, axis=0))
    return prev, nxt


def _prep_kernel(pc_ref, pcp_ref, pcn_ref, pq_ref, pqp_ref, pqn_ref, pab_ref,
                 cw_ref, dw_ref, alog_ref, dtb_ref,
                 yc_ref, q_ref, k_ref, v_ref, gb_ref, *, conv_dist, n_tiles):
    i = pl.program_id(1)
    first = i == 0
    last = i == n_tiles - 1
    cwid = yc_ref.shape[-1]
    kwid = q_ref.shape[-1]

    pc = pc_ref[0]
    u = pc[:, cwid:2 * cwid] * pc[:, 2 * cwid:]
    pcp = pcp_ref[0]
    pcn = pcn_ref[0]
    up, un = _shift_rows(u, pcp[:, cwid:2 * cwid] * pcp[:, 2 * cwid:],
                         pcn[:, cwid:2 * cwid] * pcn[:, 2 * cwid:], conv_dist, first, last)
    cw = cw_ref[...]
    yc_ref[0] = pc[:, :cwid] * (up * cw[0:1] + u * cw[1:2] + un * cw[2:3])

    pq = pq_ref[0]
    qp, qn = _shift_rows(pq, pqp_ref[0], pqn_ref[0], 1, first, last)
    dw = dw_ref[...]
    qkv = _silu(qp * dw[0:1] + pq * dw[1:2] + qn * dw[2:3])
    for h in range(N_HEADS):
        sl = slice(h * HEAD_DIM, (h + 1) * HEAD_DIM)
        qh = qkv[:, sl]
        kh = qkv[:, kwid + h * HEAD_DIM: kwid + (h + 1) * HEAD_DIM]
        q_ref[0, :, sl] = qh * (lax.rsqrt(jnp.sum(qh * qh, axis=-1, keepdims=True) + EPS) * HEAD_DIM ** -0.5)
        k_ref[0, :, sl] = kh * lax.rsqrt(jnp.sum(kh * kh, axis=-1, keepdims=True) + EPS)
    v_ref[0] = qkv[:, 2 * kwid:]

    ab = pab_ref[0]
    lane = lax.broadcasted_iota(jnp.int32, ab.shape, 1)
    is_gf = lane < N_HEADS
    is_gb = (lane >= 2 * N_HEADS) & (lane < 3 * N_HEADS)
    xg = ab + dtb_ref[...]
    softplus = jnp.maximum(xg, 0.0) + jnp.log1p(jnp.exp(-jnp.abs(xg)))
    gates = jnp.where(is_gf | is_gb, -jnp.exp(alog_ref[...]) * softplus, _sigmoid(ab))
    r = lax.broadcasted_iota(jnp.int32, (CHUNK, CHUNK), 0)
    c = lax.broadcasted_iota(jnp.int32, (CHUNK, CHUNK), 1)
    tril = (r >= c).astype(BF16)
    triu = (r <= c).astype(BF16)
    g_hi = gates.astype(BF16)
    rem = gates - g_hi.astype(F32)
    g_mid = rem.astype(BF16)
    g_lo = (rem - g_mid.astype(F32)).astype(BF16)
    d = lambda m, x: jnp.dot(m, x, preferred_element_type=F32)
    lane_c = lax.broadcasted_iota(jnp.int32, (CHUNK, ab.shape[1]), 1)
    is_gf_c = lane_c < N_HEADS
    is_gb_c = (lane_c >= 2 * N_HEADS) & (lane_c < 3 * N_HEADS)
    for t in range(ab.shape[0] // CHUNK):
        rs = slice(t * CHUNK, (t + 1) * CHUNK)
        parts = (g_hi[rs], g_mid[rs], g_lo[rs])
        pre = d(tril, parts[0]) + (d(tril, parts[1]) + d(tril, parts[2]))
        suf = d(triu, parts[0]) + (d(triu, parts[1]) + d(triu, parts[2]))
        gb_ref[0, rs, :] = jnp.where(is_gf_c, pre, jnp.where(is_gb_c, suf, gates[rs]))


def _prep(pc, pq, pab, conv_w, dn_conv_w, alog_row, dtb_row, conv_dist, tm):
    bsz, seq, wc3 = pc.shape
    wq3 = pq.shape[-1]
    cwid, kwid = wc3 // 3, wq3 // 3
    n_tiles = seq // tm
    hc = max(conv_dist, 8)
    hq = 8
    tok = lambda b, i: (b, i, 0)
    const2 = lambda b, i: (0, 0)

    def prev_map(hrows):
        return lambda b, i: (b, jnp.maximum(i * (tm // hrows) - 1, 0), 0)

    def next_map(hrows):
        return lambda b, i: (b, jnp.minimum((i + 1) * (tm // hrows), seq // hrows - 1), 0)

    outs = [jax.ShapeDtypeStruct((bsz, seq, w), F32) for w in (cwid, kwid, kwid, kwid, pab.shape[-1])]
    return pl.pallas_call(
        functools.partial(_prep_kernel, conv_dist=conv_dist, n_tiles=n_tiles),
        grid=(bsz, n_tiles),
        in_specs=[pl.BlockSpec((1, tm, wc3), tok),
                  pl.BlockSpec((1, hc, wc3), prev_map(hc)),
                  pl.BlockSpec((1, hc, wc3), next_map(hc)),
                  pl.BlockSpec((1, tm, wq3), tok),
                  pl.BlockSpec((1, hq, wq3), prev_map(hq)),
                  pl.BlockSpec((1, hq, wq3), next_map(hq)),
                  pl.BlockSpec((1, tm, pab.shape[-1]), tok),
                  pl.BlockSpec(conv_w.shape, const2),
                  pl.BlockSpec(dn_conv_w.shape, const2),
                  pl.BlockSpec(alog_row.shape, const2),
                  pl.BlockSpec(dtb_row.shape, const2)],
        out_specs=[pl.BlockSpec((1, tm, o.shape[-1]), tok) for o in outs],
        out_shape=outs,
        compiler_params=_cparams("parallel", "arbitrary"),
        name="mixer_prep",
    )(pc, pc, pc, pq, pq, pq, pab, conv_w, dn_conv_w, alog_row, dtb_row)


def _pair_block_diag(y16, keep_left, keep_right):
    return jnp.concatenate([y16 * keep_left, y16 * keep_right], axis=0)


def _pair_inverses(a_list, eye_pair, keep_left, keep_right):
    d = lambda x, y: jnp.dot(x, y, preferred_element_type=F32)
    ps = [-a for a in a_list]
    ts = [eye_pair + p for p in ps]
    n_sq = CHUNK.bit_length() - 2
    for i in range(n_sq + 1):
        new_ps, new_ts = [], []
        for p, t in zip(ps, ts):
            p_hi, p_lo = _split_bf16(p)
            r_hi = _pair_block_diag(p_hi, keep_left, keep_right)
            r_lo = _pair_block_diag(p_lo, keep_left, keep_right)
            r_hh = jnp.concatenate([r_hi, r_hi], axis=0)
            if i == 0:
                new_ps.append(d(jnp.concatenate([p_hi, p_lo], axis=1), r_hh) + d(p_hi, r_lo))
                new_ts.append(t)
                continue
            t_hi, t_lo = _split_bf16(t)
            if i == n_sq:
                new_ps.append(p)
                new_ts.append(t + (d(jnp.concatenate([t_hi, t_lo], axis=1), r_hh) + d(t_hi, r_lo)))
                continue
            l1 = jnp.concatenate([jnp.concatenate([p_hi, p_lo], axis=1),
                                  jnp.concatenate([t_hi, t_lo], axis=1)], axis=0)
            l2 = jnp.concatenate([p_hi, t_hi], axis=0)
            res = d(l1, r_hh) + d(l2, r_lo)
            new_ps.append(res[:CHUNK])
            new_ts.append(t + res[CHUNK:])
        ps, ts = new_ps, new_ts
    return ts


def _chunk_prep_kernel(q_ref, k_ref, v_ref, g_ref,
                       uf_ref, ub_ref, wf_ref, wb_ref, qgf_ref, qgb_ref, kgf_ref, kgb_ref,
                       qkf_ref, qkb_ref, glf_ref, glb_ref):
    u_refs, w_refs, qg_refs = (uf_ref, ub_ref), (wf_ref, wb_ref), (qgf_ref, qgb_ref)
    kg_refs, qk_refs, gl_refs = (kgf_ref, kgb_ref), (qkf_ref, qkb_ref), (glf_ref, glb_ref)
    row = lax.broadcasted_iota(jnp.int32, (CHUNK, 2 * CHUNK), 0)
    lane = lax.broadcasted_iota(jnp.int32, (CHUNK, 2 * CHUNK), 1)
    col = lane & (CHUNK - 1)
    left = lane < CHUNK
    keep_left = jnp.where(left, 1.0, 0.0).astype(BF16)
    keep_right = jnp.where(left, 0.0, 1.0).astype(BF16)
    eye_pair = jnp.where(row == col, 1.0, 0.0)
    zeros16 = jnp.zeros((CHUNK, HEAD_DIM), BF16)
    dot = lambda x, y: jnp.dot(x, y, preferred_element_type=F32)

    chains, a_list = [], []
    for c in range(PREP_CHUNKS):
        rs = slice(c * CHUNK, (c + 1) * CHUNK)
        gates = g_ref[0, rs, :]
        for p in range(N_PAIRS):
            sl = slice(2 * p * HEAD_DIM, 2 * (p + 1) * HEAD_DIM)
            q2, k2, v2 = q_ref[0, rs, sl], k_ref[0, rs, sl], v_ref[0, rs, sl]
            k16 = k2.astype(BF16)
            k_bd = jnp.concatenate([jnp.concatenate([k16[:, :HEAD_DIM], zeros16], axis=1),
                                    jnp.concatenate([zeros16, k16[:, HEAD_DIM:]], axis=1)], axis=0)
            kq = jnp.concatenate([k16, q2.astype(BF16)], axis=0)
            gram = lax.dot_general(kq, k_bd, (((1,), (1,)), ((), ())), preferred_element_type=F32)
            for d in range(N_DIRS):
                gcol, bcol = 2 * d * N_HEADS, (2 * d + 1) * N_HEADS
                gc = [gates[:, gcol + 2 * p + e:gcol + 2 * p + e + 1] for e in range(2)]
                beta = [gates[:, bcol + 2 * p + e:bcol + 2 * p + e + 1] for e in range(2)]
                gc_pair = jnp.where(left, gc[0], gc[1])
                beta_pair = jnp.where(left, beta[0], beta[1])
                gr_pair = jnp.sum(jnp.where(row == col, gc_pair, 0.0), axis=0, keepdims=True)
                if d == 0:
                    incl, strict, last = row >= col, row > col, CHUNK - 1
                else:
                    incl, strict, last = row <= col, row < col, 0
                decay = jnp.where(incl, jnp.exp(jnp.where(incl, gc_pair - gr_pair, 0.0)), 0.0)
                a_list.append(jnp.where(strict, beta_pair * gram[:CHUNK] * decay, 0.0))
                qk_refs[d][0, rs, p * HEAD_DIM:(p + 1) * HEAD_DIM] = (
                    jnp.where(incl, gram[CHUNK:] * decay, 0.0).astype(BF16))
                chains.append((c, rs, p, d, q2, k2, v2, gc, beta, last))

    t_list = _pair_inverses(a_list, eye_pair, keep_left, keep_right)

    for (c, rs, p, d, q2, k2, v2, gc, beta, last), t in zip(chains, t_list):
        t16 = t.astype(BF16)
        for e in range(2):
            h = 2 * p + e
            sl = slice(e * HEAD_DIM, (e + 1) * HEAD_DIM)
            hs = slice(h * HEAD_DIM, (h + 1) * HEAD_DIM)
            eg = jnp.exp(gc[e])
            g_last = gc[e][last:last + 1, :]
            rhs = jnp.concatenate([v2[:, sl] * beta[e], k2[:, sl] * (beta[e] * eg)], axis=1).astype(BF16)
            zpad = jnp.zeros_like(rhs)
            uw = dot(t16, jnp.concatenate([rhs, zpad] if e == 0 else [zpad, rhs], axis=0))
            u_refs[d][0, rs, hs] = uw[:, :HEAD_DIM]
            w_refs[d][0, rs, hs] = uw[:, HEAD_DIM:].astype(BF16)
            qg_refs[d][0, rs, hs] = (q2[:, sl] * eg).astype(BF16)
            kg_refs[d][0, rs, hs] = (k2[:, sl] * jnp.exp(g_last - gc[e])).astype(BF16)
            gl_refs[d][0, c, h:h + 1, :] = jnp.broadcast_to(jnp.exp(g_last), (1, HEAD_DIM))


def _chunk_prep(q, k, v, gb):
    bsz, seq, kwid = q.shape
    rows = PREP_CHUNKS * CHUNK
    n = seq // CHUNK
    tok = lambda b, i: (b, i, 0)
    f32_full = jax.ShapeDtypeStruct(q.shape, F32)
    bf_full = jax.ShapeDtypeStruct(q.shape, BF16)
    bf_half = jax.ShapeDtypeStruct((bsz, seq, kwid // 2), BF16)
    gl = jax.ShapeDtypeStruct((bsz, n, N_HEADS, HEAD_DIM), F32)
    outs = [f32_full] * 2 + [bf_full] * 6 + [bf_half] * 2 + [gl] * 2
    out_specs = ([pl.BlockSpec((1, rows, kwid), tok)] * 8 + [pl.BlockSpec((1, rows, kwid // 2), tok)] * 2
                 + [pl.BlockSpec((1, PREP_CHUNKS, N_HEADS, HEAD_DIM), lambda b, i: (b, i, 0, 0))] * 2)
    return pl.pallas_call(
        _chunk_prep_kernel,
        grid=(bsz, seq // rows),
        in_specs=[pl.BlockSpec((1, rows, kwid), tok)] * 3 + [pl.BlockSpec((1, rows, gb.shape[-1]), tok)],
        out_specs=out_specs,
        out_shape=outs,
        compiler_params=_cparams("parallel", "parallel"),
        name="delta_chunk_prep",
    )(q, k, v, gb)


def _scan_kernel(uf_ref, wf_ref, qgf_ref, kgf_ref, qkf_ref, glf_ref,
                 ub_ref, wb_ref, qgb_ref, kgb_ref, qkb_ref, glb_ref, s0f_ref, s0b_ref,
                 of_ref, ob_ref, sff_ref, sfb_ref, s_ref, *, n_chunks, has_s0):
    j = pl.program_id(1)
    n_hd = N_DIRS * N_HEADS

    @pl.when(j == 0)
    def _():
        for r in range(SCAN_ROWS):
            if has_s0:
                s_ref[r * n_hd:r * n_hd + N_HEADS] = s0f_ref[r]
                s_ref[r * n_hd + N_HEADS:(r + 1) * n_hd] = s0b_ref[r]
            else:
                s_ref[r * n_hd:(r + 1) * n_hd] = jnp.zeros((n_hd, HEAD_DIM, HEAD_DIM), F32)

    dot = lambda x, y: jnp.dot(x, y, preferred_element_type=F32)
    zeros16 = jnp.zeros((CHUNK, HEAD_DIM), BF16)
    ins = ((uf_ref, wf_ref, qgf_ref, kgf_ref, qkf_ref, glf_ref), (ub_ref, wb_ref, qgb_ref, kgb_ref, qkb_ref, glb_ref))
    outs = (of_ref, ob_ref)
    hds = [(r, d, h) for r in range(SCAN_ROWS) for d in range(N_DIRS) for h in range(N_HEADS)]
    hsl = lambda h: slice(h * HEAD_DIM, (h + 1) * HEAD_DIM)
    sidx = lambda r, d, h: r * n_hd + d * N_HEADS + h

    s_old, ws_qs = [], []
    for r, d, h in hds:
        s = s_ref[sidx(r, d, h)]
        s_old.append(s)
        lhs = jnp.concatenate([ins[d][1][r, :, hsl(h)], ins[d][2][r, :, hsl(h)]], axis=0)
        ws_qs.append(dot(lhs, s.astype(BF16)))
    v_new = []
    for i, (r, d, h) in enumerate(hds):
        vn = (ins[d][0][r, :, hsl(h)] - ws_qs[i][:CHUNK]).astype(BF16)
        v_new.append(vn)
        v_pad = jnp.concatenate([vn, zeros16] if h % 2 == 0 else [zeros16, vn], axis=0)
        qk_pair = ins[d][4][r, :, hsl(h // 2)]
        outs[d][r, :, hsl(h)] = ws_qs[i][CHUNK:] + dot(qk_pair, v_pad)
    for i, (r, d, h) in enumerate(hds):
        kg = ins[d][3][r, :, hsl(h)]
        upd = lax.dot_general(kg, v_new[i], (((0,), (0,)), ((), ())), preferred_element_type=F32)
        s_ref[sidx(r, d, h)] = s_old[i] * ins[d][5][r, 0, h:h + 1, :] + upd

    @pl.when(j == n_chunks - 1)
    def _():
        for r in range(SCAN_ROWS):
            sff_ref[r] = s_ref[r * n_hd:r * n_hd + N_HEADS]
            sfb_ref[r] = s_ref[r * n_hd + N_HEADS:(r + 1) * n_hd]


def _scan(prep, s0_f, s0_b):
    (u_f, u_b, w_f, w_b, qg_f, qg_b, kg_f, kg_b, qk_f, qk_b, gl_f, gl_b) = prep
    bsz, seq, kwid = u_f.shape
    assert bsz % SCAN_ROWS == 0
    n = seq // CHUNK
    has_s0 = s0_f is not None
    sblk = (SCAN_ROWS, N_HEADS, HEAD_DIM, HEAD_DIM)
    if not has_s0:
        s0_f = s0_b = jnp.zeros(sblk, F32)
    fwd = lambda b, j: (b, j, 0)
    bwd = lambda b, j: (b, n - 1 - j, 0)
    fwd4 = lambda b, j: (b, j, 0, 0)
    bwd4 = lambda b, j: (b, n - 1 - j, 0, 0)
    smap = (lambda b, j: (b, 0, 0, 0)) if has_s0 else (lambda b, j: (0, 0, 0, 0))

    def dir_specs(tok, tok4):
        return ([pl.BlockSpec((SCAN_ROWS, CHUNK, kwid), tok)] * 4
                + [pl.BlockSpec((SCAN_ROWS, CHUNK, kwid // 2), tok),
                   pl.BlockSpec((SCAN_ROWS, 1, N_HEADS, HEAD_DIM), tok4)])

    return pl.pallas_call(
        functools.partial(_scan_kernel, n_chunks=n, has_s0=has_s0),
        grid=(bsz // SCAN_ROWS, n),
        in_specs=dir_specs(fwd, fwd4) + dir_specs(bwd, bwd4) + [pl.BlockSpec(sblk, smap)] * 2,
        out_specs=[pl.BlockSpec((SCAN_ROWS, CHUNK, kwid), fwd),
                   pl.BlockSpec((SCAN_ROWS, CHUNK, kwid), bwd),
                   pl.BlockSpec(sblk, lambda b, j: (b, 0, 0, 0)),
                   pl.BlockSpec(sblk, lambda b, j: (b, 0, 0, 0))],
        out_shape=[jax.ShapeDtypeStruct(u_f.shape, F32)] * 2
                  + [jax.ShapeDtypeStruct((bsz,) + sblk[1:], F32)] * 2,
        scratch_shapes=[pltpu.VMEM((SCAN_ROWS * N_DIRS * N_HEADS, HEAD_DIM, HEAD_DIM), F32)],
        compiler_params=_cparams("parallel", "arbitrary"),
        name="delta_scan",
    )(u_f, w_f, qg_f, kg_f, qk_f, gl_f, u_b, w_b, qg_b, kg_b, qk_b, gl_b, s0_f, s0_b)


def _layer(x, mod, per_batch_mod, s0_f, s0_b, on_grid, lw, norm_final, final_norm, tm, fold):
    (g1, win1, wo1, gm, wc, wq, wz, wab, conv_w, dn_conv_w, alog_row, dtb_row, gn, wt, wbo,
     g2, win2, wo2) = lw
    bsz, seq, d = x.shape
    assert fold == 1 or not per_batch_mod
    folded = lambda a: a.reshape(bsz // fold, fold * seq, a.shape[-1])
    unfolded = lambda a: a.reshape(bsz, seq, a.shape[-1])
    x = _ffn1(folded(x), mod, per_batch_mod, g1, win1, wo1, tm)
    pc, pq, pz, pab = _proj(x, mod, per_batch_mod, gm, wc, wq, wz, wab, tm)
    yc, q, k, v, gb = _prep(unfolded(pc), unfolded(pq), unfolded(pab), conv_w, dn_conv_w, alog_row, dtb_row,
                            GRID_W if on_grid else 1, min(tm, seq))
    o_f, o_b, s_f, s_b = _scan(_chunk_prep(q, k, v, gb), s0_f, s0_b)
    y = _mix_ffn2(x, mod, per_batch_mod, folded(yc), folded(o_f), folded(o_b), pz, gn, wt, wbo,
                  g2, win2, wo2, norm_final, final_norm, tm)
    return unfolded(y), s_f, s_b


def kernel(x_prompt, x_sample, state_dn_fwd, state_dn_bwd, c, c_ctx, w_ada, b_ada, norm_ffn1, w_ffn1_in,
           w_ffn1_out, norm_mix, w_mix_in, conv_w, dn_conv_w, dn_a_log, dn_dt_bias, dn_norm, w_mix_out,
           norm_ffn2, w_ffn2_in, w_ffn2_out, norm_final):
    depth = w_ada.shape[0]
    d = x_prompt.shape[-1]
    n_lat = c.shape[0]
    cwid = conv_w.shape[-1]
    kwid = N_HEADS * HEAD_DIM
    n_gate = 4 * N_HEADS
    row = lambda a: a.reshape(1, -1)

    cvec = jnp.concatenate([c_ctx[None, :], c, jnp.zeros((16 - 1 - n_lat, d), F32)], axis=0)

    xp, xs = x_prompt, x_sample
    new_f, new_b = [], []
    for l in range(depth):
        mod = _modulation(cvec, w_ada[l], b_ada[l]).reshape(16, N_MOD, d)
        mod_ctx, mod_lat = mod[0:1], mod[1:1 + n_lat]

        win1, wo1 = w_ffn1_in[l].astype(BF16), w_ffn1_out[l].astype(BF16)
        win2, wo2 = w_ffn2_in[l].astype(BF16), w_ffn2_out[l].astype(BF16)
        wm = w_mix_in[l].astype(BF16)
        wc = wm[:, :3 * cwid]
        wq = wm[:, 3 * cwid:3 * cwid + 3 * kwid]
        wz = wm[:, 3 * cwid + 3 * kwid:3 * cwid + 4 * kwid]
        wab = jnp.pad(wm[:, 3 * cwid + 4 * kwid:], ((0, 0), (0, HEAD_DIM - n_gate)))
        zpad = jnp.zeros((N_HEADS,), F32)
        alog_row = jnp.concatenate([dn_a_log[l, 0], zpad, dn_a_log[l, 1], zpad,
                                    jnp.zeros((HEAD_DIM - n_gate,), F32)])[None, :]
        dtb_row = jnp.concatenate([dn_dt_bias[l, 0], zpad, dn_dt_bias[l, 1], zpad,
                                   jnp.zeros((HEAD_DIM - n_gate,), F32)])[None, :]
        wo = w_mix_out[l].astype(BF16)
        lw = (row(norm_ffn1[l]), win1, wo1, row(norm_mix[l]), wc, wq, wz, wab, conv_w[l], dn_conv_w[l],
              alog_row, dtb_row, row(dn_norm[l]), wo[:cwid], wo[cwid:], row(norm_ffn2[l]), win2, wo2)
        last = l == depth - 1
        xp, sf, sb = _layer(xp, mod_ctx, False, None, None, False, lw, row(norm_final), last,
                            TOKEN_TILE, TOKEN_TILE // xp.shape[1])
        new_f.append(sf)
        new_b.append(sb)
        xs, _, _ = _layer(xs, mod_lat, True, state_dn_fwd[:, l], state_dn_bwd[:, l], True, lw,
                          row(norm_final), last, TOKEN_TILE, 1)
    return xp, xs, jnp.stack(new_f, axis=1), jnp.stack(new_b, axis=1)
```

```python
import functools

import jax
import jax.numpy as jnp
from jax import lax
from jax.experimental import pallas as pl
from jax.experimental.pallas import tpu as pltpu

F32 = jnp.float32
BF16 = jnp.bfloat16

EPS = 1e-6
CHUNK = 64
GRID_W = 64
N_HEADS = 4
HEAD_DIM = 128
N_PAIRS = N_HEADS // 2
N_DIRS = 2
N_MOD = 9
FF_CHUNK = 256
TOKEN_TILE = 512
PREP_CHUNKS = 4
SCAN_ROWS = 4
VMEM_LIMIT = 56 * 1024 * 1024


def _cparams(*sem):
    return pltpu.CompilerParams(dimension_semantics=sem, vmem_limit_bytes=VMEM_LIMIT)


def _bdot(a, b):
    return jnp.dot(a.astype(BF16), b.astype(BF16), preferred_element_type=F32)


def _split_bf16(a):
    hi = a.astype(BF16)
    lo = (a - hi.astype(F32)).astype(BF16)
    return hi, lo


def _dot_x3(a, b):
    a_hi, a_lo = _split_bf16(a)
    b_hi, b_lo = _split_bf16(b)
    d = lambda x, y: jnp.dot(x, y, preferred_element_type=F32)
    return d(a_hi, b_hi) + (d(a_hi, b_lo) + d(a_lo, b_hi))


def _sigmoid(x):
    return 1.0 / (1.0 + jnp.exp(-x))


def _silu(x):
    return x * _sigmoid(x)


def _rms(x):
    return x * lax.rsqrt(jnp.mean(x * x, axis=-1, keepdims=True) + EPS)


def _mod_kernel(c_ref, w_ref, b_ref, o_ref):
    s = _silu(c_ref[...])
    o_ref[...] = _dot_x3(s, w_ref[...]) + b_ref[...]


def _modulation(cvec, w_ada, b_ada):
    rows, d = cvec.shape
    n = w_ada.shape[1]
    tn = 1536
    return pl.pallas_call(
        _mod_kernel,
        grid=(n // tn,),
        in_specs=[pl.BlockSpec((rows, d), lambda i: (0, 0)),
                  pl.BlockSpec((d, tn), lambda i: (0, i)),
                  pl.BlockSpec((1, tn), lambda i: (0, i))],
        out_specs=pl.BlockSpec((rows, tn), lambda i: (0, i)),
        out_shape=jax.ShapeDtypeStruct((rows, n), F32),
        compiler_params=_cparams("arbitrary"),
        name="modulation",
    )(cvec, w_ada, b_ada.reshape(1, n))


def _swiglu_residual(x, x_keep_ref, mod_ref, mod_row, g_ref, win_ref, wo_ref, h_ref, acc_ref):
    sh = mod_ref[0, mod_row:mod_row + 1, :]
    sc = mod_ref[0, mod_row + 1:mod_row + 2, :]
    gt = mod_ref[0, mod_row + 2:mod_row + 3, :]
    h_ref[...] = (_rms(x) * g_ref[...] * (1.0 + sc) + sh).astype(BF16)
    acc_ref[...] = jnp.zeros_like(acc_ref)
    ff = wo_ref.shape[0]
    for c in range(ff // FF_CHUNK):
        lo, hi = c * FF_CHUNK, (c + 1) * FF_CHUNK
        h = h_ref[...]
        a = jnp.dot(h, win_ref[:, lo:hi], preferred_element_type=F32)
        b = jnp.dot(h, win_ref[:, ff + lo:ff + hi], preferred_element_type=F32)
        act = (_silu(a) * b).astype(BF16)
        acc_ref[...] += jnp.dot(act, wo_ref[lo:hi, :], preferred_element_type=F32)
    return x_keep_ref[...] + 0.5 * gt * acc_ref[...]


def _ffn1_kernel(x_ref, mod_ref, g_ref, win_ref, wo_ref, o_ref, h_ref, acc_ref):
    o_ref[0] = _swiglu_residual(x_ref[0], x_ref.at[0], mod_ref, 0, g_ref, win_ref, wo_ref, h_ref, acc_ref)


def _mix_ffn2_kernel(x_ref, mod_ref, yc_ref, of_ref, ob_ref, z_ref, gn_ref, wt_ref, wbm_ref,
                     g_ref, win_ref, wo_ref, gfin_ref, o_ref, h_ref, acc_ref, x2_ref, *, final_norm):
    o = of_ref[0] + ob_ref[0]
    gate = _silu(z_ref[0])
    gn = gn_ref[...]
    ys = []
    for h in range(N_HEADS):
        sl = slice(h * HEAD_DIM, (h + 1) * HEAD_DIM)
        ys.append(_rms(o[:, sl]) * gn * gate[:, sl])
    y_dn = jnp.concatenate(ys, axis=1)
    m = _bdot(yc_ref[0], wt_ref[...]) + _bdot(y_dn, wbm_ref[...])
    x2_ref[...] = x_ref[0] + mod_ref[0, 5:6, :] * m
    y = _swiglu_residual(x2_ref[...], x2_ref, mod_ref, 6, g_ref, win_ref, wo_ref, h_ref, acc_ref)
    if final_norm:
        y = _rms(y) * gfin_ref[...]
    o_ref[0] = y


def _token_specs(d, tm, per_batch_mod):
    mod_map = (lambda b, i: (b, 0, 0)) if per_batch_mod else (lambda b, i: (0, 0, 0))
    return pl.BlockSpec((1, tm, d), lambda b, i: (b, i, 0)), pl.BlockSpec((1, N_MOD, d), mod_map)


def _ffn_weight_specs(win, wo):
    assert win.shape[1] == 2 * wo.shape[0] and wo.shape[0] % FF_CHUNK == 0
    const2 = lambda b, i: (0, 0)
    return [pl.BlockSpec(win.shape, const2, pipeline_mode=pl.Buffered(1)),
            pl.BlockSpec(wo.shape, const2, pipeline_mode=pl.Buffered(1))]


def _ffn1(x, mod, per_batch_mod, g, win, wo, tm):
    bsz, seq, d = x.shape
    const2 = lambda b, i: (0, 0)
    x_spec, mod_spec = _token_specs(d, tm, per_batch_mod)
    return pl.pallas_call(
        _ffn1_kernel,
        grid=(bsz, seq // tm),
        in_specs=[x_spec, mod_spec, pl.BlockSpec((1, d), const2)] + _ffn_weight_specs(win, wo),
        out_specs=x_spec,
        out_shape=jax.ShapeDtypeStruct(x.shape, F32),
        scratch_shapes=[pltpu.VMEM((tm, d), BF16), pltpu.VMEM((tm, d), F32)],
        compiler_params=_cparams("parallel", "arbitrary"),
        name="ffn1",
    )(x, mod, g, win, wo)


def _mix_ffn2(x, mod, per_batch_mod, yc, o_f, o_b, z, gn, wt, wbm, g, win, wo, gfin, final_norm, tm):
    bsz, seq, d = x.shape
    half = yc.shape[-1]
    const2 = lambda b, i: (0, 0)
    x_spec, mod_spec = _token_specs(d, tm, per_batch_mod)
    half_spec = pl.BlockSpec((1, tm, half), lambda b, i: (b, i, 0))
    return pl.pallas_call(
        functools.partial(_mix_ffn2_kernel, final_norm=final_norm),
        grid=(bsz, seq // tm),
        in_specs=[x_spec, mod_spec] + [half_spec] * 4
                 + [pl.BlockSpec((1, HEAD_DIM), const2),
                    pl.BlockSpec((half, d), const2, pipeline_mode=pl.Buffered(1)),
                    pl.BlockSpec((half, d), const2, pipeline_mode=pl.Buffered(1)),
                    pl.BlockSpec((1, d), const2)]
                 + _ffn_weight_specs(win, wo) + [pl.BlockSpec((1, d), const2)],
        out_specs=x_spec,
        out_shape=jax.ShapeDtypeStruct(x.shape, F32),
        scratch_shapes=[pltpu.VMEM((tm, d), BF16), pltpu.VMEM((tm, d), F32), pltpu.VMEM((tm, d), F32)],
        compiler_params=_cparams("parallel", "arbitrary"),
        name="mix_out_ffn2",
    )(x, mod, yc, o_f, o_b, z, gn, wt, wbm, g, win, wo, gfin)


def _proj_kernel(x_ref, mod_ref, g_ref, wc_ref, wq_ref, wz_ref, wab_ref, pc_ref, pq_ref, pz_ref, pab_ref):
    x = x_ref[0]
    sh = mod_ref[0, 3:4, :]
    sc = mod_ref[0, 4:5, :]
    h = (_rms(x) * g_ref[...] * (1.0 + sc) + sh).astype(BF16)
    pc_ref[0] = jnp.dot(h, wc_ref[...], preferred_element_type=F32)
    pq_ref[0] = jnp.dot(h, wq_ref[...], preferred_element_type=F32)
    pz_ref[0] = jnp.dot(h, wz_ref[...], preferred_element_type=F32)
    pab_ref[0] = jnp.dot(h, wab_ref[...], preferred_element_type=F32)


def _proj(x, mod, per_batch_mod, g, wc, wq, wz, wab, tm):
    bsz, seq, d = x.shape
    mod_map = (lambda b, i: (b, 0, 0)) if per_batch_mod else (lambda b, i: (0, 0, 0))
    const2 = lambda b, i: (0, 0)
    tok = lambda b, i: (b, i, 0)
    widths = (wc.shape[1], wq.shape[1], wz.shape[1], wab.shape[1])
    return pl.pallas_call(
        _proj_kernel,
        grid=(bsz, seq // tm),
        in_specs=[pl.BlockSpec((1, tm, d), tok),
                  pl.BlockSpec((1, N_MOD, d), mod_map),
                  pl.BlockSpec((1, d), const2)]
                 + [pl.BlockSpec((d, w), const2, pipeline_mode=pl.Buffered(1)) for w in widths],
        out_specs=[pl.BlockSpec((1, tm, w), tok) for w in widths],
        out_shape=[jax.ShapeDtypeStruct((bsz, seq, w), F32) for w in widths],
        compiler_params=_cparams("parallel", "arbitrary"),
        name="mix_in_proj",
    )(x, mod, g, wc, wq, wz, wab)


def _shift_rows(cur, prev_halo, next_halo, dist, first, last):
    tm = cur.shape[0]
    hp = jnp.where(first, 0.0, prev_halo)
    hn = jnp.where(last, 0.0, next_halo)
    if dist % 8 == 0:
        prev = jnp.concatenate([hp[hp.shape[0] - dist:], cur[:tm - dist]], axis=0)
        nxt = jnp.concatenate([cur[dist:], hn[:dist]], axis=0)
        return prev, nxt
    assert dist == 1
    row = lax.broadcasted_iota(jnp.int32, cur.shape, 0)
    prev = jnp.where(row == 0, hp[hp.shape[0] - 1:], pltpu.roll(cur, 1, axis=0))
    nxt = jnp.where(row == tm - 1, hn[:1], pltpu.roll(cur, tm - 1, axis=0))
    return prev, nxt


def _prep_kernel(pc_ref, pcp_ref, pcn_ref, pq_ref, pqp_ref, pqn_ref, pab_ref,
                 cw_ref, dw_ref, alog_ref, dtb_ref,
                 yc_ref, q_ref, k_ref, v_ref, gb_ref, *, conv_dist, n_tiles):
    i = pl.program_id(1)
    first = i == 0
    last = i == n_tiles - 1
    cwid = yc_ref.shape[-1]
    kwid = q_ref.shape[-1]

    pc = pc_ref[0]
    u = pc[:, cwid:2 * cwid] * pc[:, 2 * cwid:]
    pcp = pcp_ref[0]
    pcn = pcn_ref[0]
    up, un = _shift_rows(u, pcp[:, cwid:2 * cwid] * pcp[:, 2 * cwid:],
                         pcn[:, cwid:2 * cwid] * pcn[:, 2 * cwid:], conv_dist, first, last)
    cw = cw_ref[...]
    yc_ref[0] = pc[:, :cwid] * (up * cw[0:1] + u * cw[1:2] + un * cw[2:3])

    pq = pq_ref[0]
    qp, qn = _shift_rows(pq, pqp_ref[0], pqn_ref[0], 1, first, last)
    dw = dw_ref[...]
    qkv = _silu(qp * dw[0:1] + pq * dw[1:2] + qn * dw[2:3])
    for h in range(N_HEADS):
        sl = slice(h * HEAD_DIM, (h + 1) * HEAD_DIM)
        qh = qkv[:, sl]
        kh = qkv[:, kwid + h * HEAD_DIM: kwid + (h + 1) * HEAD_DIM]
        q_ref[0, :, sl] = qh * (lax.rsqrt(jnp.sum(qh * qh, axis=-1, keepdims=True) + EPS) * HEAD_DIM ** -0.5)
        k_ref[0, :, sl] = kh * lax.rsqrt(jnp.sum(kh * kh, axis=-1, keepdims=True) + EPS)
    v_ref[0] = qkv[:, 2 * kwid:]

    ab = pab_ref[0]
    lane = lax.broadcasted_iota(jnp.int32, ab.shape, 1)
    is_gf = lane < N_HEADS
    is_gb = (lane >= 2 * N_HEADS) & (lane < 3 * N_HEADS)
    xg = ab + dtb_ref[...]
    softplus = jnp.maximum(xg, 0.0) + jnp.log1p(jnp.exp(-jnp.abs(xg)))
    gates = jnp.where(is_gf | is_gb, -jnp.exp(alog_ref[...]) * softplus, _sigmoid(ab))
    r = lax.broadcasted_iota(jnp.int32, (CHUNK, CHUNK), 0)
    c = lax.broadcasted_iota(jnp.int32, (CHUNK, CHUNK), 1)
    tril = (r >= c).astype(BF16)
    triu = (r <= c).astype(BF16)
    g_hi = gates.astype(BF16)
    rem = gates - g_hi.astype(F32)
    g_mid = rem.astype(BF16)
    g_lo = (rem - g_mid.astype(F32)).astype(BF16)
    d = lambda m, x: jnp.dot(m, x, preferred_element_type=F32)
    lane_c = lax.broadcasted_iota(jnp.int32, (CHUNK, ab.shape[1]), 1)
    is_gf_c = lane_c < N_HEADS
    is_gb_c = (lane_c >= 2 * N_HEADS) & (lane_c < 3 * N_HEADS)
    for t in range(ab.shape[0] // CHUNK):
        rs = slice(t * CHUNK, (t + 1) * CHUNK)
        parts = (g_hi[rs], g_mid[rs], g_lo[rs])
        pre = d(tril, parts[0]) + (d(tril, parts[1]) + d(tril, parts[2]))
        suf = d(triu, parts[0]) + (d(triu, parts[1]) + d(triu, parts[2]))
        gb_ref[0, rs, :] = jnp.where(is_gf_c, pre, jnp.where(is_gb_c, suf, gates[rs]))


def _prep(pc, pq, pab, conv_w, dn_conv_w, alog_row, dtb_row, conv_dist, tm):
    bsz, seq, wc3 = pc.shape
    wq3 = pq.shape[-1]
    cwid, kwid = wc3 // 3, wq3 // 3
    n_tiles = seq // tm
    hc = max(conv_dist, 8)
    hq = 8
    tok = lambda b, i: (b, i, 0)
    const2 = lambda b, i: (0, 0)

    def prev_map(hrows):
        return lambda b, i: (b, jnp.maximum(i * (tm // hrows) - 1, 0), 0)

    def next_map(hrows):
        return lambda b, i: (b, jnp.minimum((i + 1) * (tm // hrows), seq // hrows - 1), 0)

    outs = [jax.ShapeDtypeStruct((bsz, seq, w), F32) for w in (cwid, kwid, kwid, kwid, pab.shape[-1])]
    return pl.pallas_call(
        functools.partial(_prep_kernel, conv_dist=conv_dist, n_tiles=n_tiles),
        grid=(bsz, n_tiles),
        in_specs=[pl.BlockSpec((1, tm, wc3), tok),
                  pl.BlockSpec((1, hc, wc3), prev_map(hc)),
                  pl.BlockSpec((1, hc, wc3), next_map(hc)),
                  pl.BlockSpec((1, tm, wq3), tok),
                  pl.BlockSpec((1, hq, wq3), prev_map(hq)),
                  pl.BlockSpec((1, hq, wq3), next_map(hq)),
                  pl.BlockSpec((1, tm, pab.shape[-1]), tok),
                  pl.BlockSpec(conv_w.shape, const2),
                  pl.BlockSpec(dn_conv_w.shape, const2),
                  pl.BlockSpec(alog_row.shape, const2),
                  pl.BlockSpec(dtb_row.shape, const2)],
        out_specs=[pl.BlockSpec((1, tm, o.shape[-1]), tok) for o in outs],
        out_shape=outs,
        compiler_params=_cparams("parallel", "arbitrary"),
        name="mixer_prep",
    )(pc, pc, pc, pq, pq, pq, pab, conv_w, dn_conv_w, alog_row, dtb_row)


def _pair_block_diag(y16, keep_left, keep_right):
    return jnp.concatenate([y16 * keep_left, y16 * keep_right], axis=0)


INV_BASE = 8


def _pair_matmul_x3(lhs_list, rhs, keep_left, keep_right):
    d = lambda x, y: jnp.dot(x, y, preferred_element_type=F32)
    r_hi, r_lo = _split_bf16(rhs)
    r_hi = _pair_block_diag(r_hi, keep_left, keep_right)
    r_lo = _pair_block_diag(r_lo, keep_left, keep_right)
    parts = [_split_bf16(x) for x in lhs_list]
    l1 = jnp.concatenate([jnp.concatenate([hi, lo], axis=1) for hi, lo in parts], axis=0)
    l2 = jnp.concatenate([hi for hi, _ in parts], axis=0)
    res = d(l1, jnp.concatenate([r_hi, r_hi], axis=0)) + d(l2, r_lo)
    return [res[i * CHUNK:(i + 1) * CHUNK] for i in range(len(lhs_list))]


def _pair_inverses(a_list, row, col, eye_pair, keep_left, keep_right):
    mm = lambda lhs_list, rhs: _pair_matmul_x3(lhs_list, rhs, keep_left, keep_right)
    shift = INV_BASE.bit_length() - 1
    same_base = (row >> shift) == (col >> shift)
    ps = [jnp.where(same_base, -a, 0.0) for a in a_list]
    ts = [eye_pair + p for p in ps]
    n_sq = shift - 1
    for i in range(n_sq + 1):
        new_ps, new_ts = [], []
        for p, t in zip(ps, ts):
            if i == 0:
                new_ps.append(mm([p], p)[0])
                new_ts.append(t)
            elif i == n_sq:
                new_ps.append(p)
                new_ts.append(t + mm([t], p)[0])
            else:
                p2, tp = mm([p, t], p)
                new_ps.append(p2)
                new_ts.append(t + tp)
        ps, ts = new_ps, new_ts
    size = INV_BASE
    while size < CHUNK:
        s1 = size.bit_length() - 1
        off = ((row >> (s1 + 1)) == (col >> (s1 + 1))) & ((row >> s1) != (col >> s1))
        es = [jnp.where(off, a, 0.0) for a in a_list]
        xs = [mm([e], t)[0] for e, t in zip(es, ts)]
        ts = [t - mm([t], x)[0] for t, x in zip(ts, xs)]
        size *= 2
    return ts


def _chunk_prep_kernel(q_ref, k_ref, v_ref, g_ref,
                       uf_ref, ub_ref, wf_ref, wb_ref, qgf_ref, qgb_ref, kgf_ref, kgb_ref,
                       qkf_ref, qkb_ref, glf_ref, glb_ref):
    u_refs, w_refs, qg_refs = (uf_ref, ub_ref), (wf_ref, wb_ref), (qgf_ref, qgb_ref)
    kg_refs, qk_refs, gl_refs = (kgf_ref, kgb_ref), (qkf_ref, qkb_ref), (glf_ref, glb_ref)
    row = lax.broadcasted_iota(jnp.int32, (CHUNK, 2 * CHUNK), 0)
    lane = lax.broadcasted_iota(jnp.int32, (CHUNK, 2 * CHUNK), 1)
    col = lane & (CHUNK - 1)
    left = lane < CHUNK
    keep_left = jnp.where(left, 1.0, 0.0).astype(BF16)
    keep_right = jnp.where(left, 0.0, 1.0).astype(BF16)
    eye_pair = jnp.where(row == col, 1.0, 0.0)
    zeros16 = jnp.zeros((CHUNK, HEAD_DIM), BF16)
    dot = lambda x, y: jnp.dot(x, y, preferred_element_type=F32)

    chains, a_list = [], []
    for c in range(PREP_CHUNKS):
        rs = slice(c * CHUNK, (c + 1) * CHUNK)
        gates = g_ref[0, rs, :]
        for p in range(N_PAIRS):
            sl = slice(2 * p * HEAD_DIM, 2 * (p + 1) * HEAD_DIM)
            q2, k2, v2 = q_ref[0, rs, sl], k_ref[0, rs, sl], v_ref[0, rs, sl]
            k16 = k2.astype(BF16)
            k_bd = jnp.concatenate([jnp.concatenate([k16[:, :HEAD_DIM], zeros16], axis=1),
                                    jnp.concatenate([zeros16, k16[:, HEAD_DIM:]], axis=1)], axis=0)
            kq = jnp.concatenate([k16, q2.astype(BF16)], axis=0)
            gram = lax.dot_general(kq, k_bd, (((1,), (1,)), ((), ())), preferred_element_type=F32)
            for d in range(N_DIRS):
                gcol, bcol = 2 * d * N_HEADS, (2 * d + 1) * N_HEADS
                gc = [gates[:, gcol + 2 * p + e:gcol + 2 * p + e + 1] for e in range(2)]
                beta = [gates[:, bcol + 2 * p + e:bcol + 2 * p + e + 1] for e in range(2)]
                gc_pair = jnp.where(left, gc[0], gc[1])
                beta_pair = jnp.where(left, beta[0], beta[1])
                gr_pair = jnp.sum(jnp.where(row == col, gc_pair, 0.0), axis=0, keepdims=True)
                if d == 0:
                    incl, strict, last = row >= col, row > col, CHUNK - 1
                else:
                    incl, strict, last = row <= col, row < col, 0
                decay = jnp.where(incl, jnp.exp(jnp.where(incl, gc_pair - gr_pair, 0.0)), 0.0)
                a_list.append(jnp.where(strict, beta_pair * gram[:CHUNK] * decay, 0.0))
                qk_refs[d][0, rs, p * HEAD_DIM:(p + 1) * HEAD_DIM] = (
                    jnp.where(incl, gram[CHUNK:] * decay, 0.0).astype(BF16))
                chains.append((c, rs, p, d, q2, k2, v2, gc, beta, last))

    t_list = _pair_inverses(a_list, row, col, eye_pair, keep_left, keep_right)

    for (c, rs, p, d, q2, k2, v2, gc, beta, last), t in zip(chains, t_list):
        t16 = t.astype(BF16)
        for e in range(2):
            h = 2 * p + e
            sl = slice(e * HEAD_DIM, (e + 1) * HEAD_DIM)
            hs = slice(h * HEAD_DIM, (h + 1) * HEAD_DIM)
            eg = jnp.exp(gc[e])
            g_last = gc[e][last:last + 1, :]
            rhs = jnp.concatenate([v2[:, sl] * beta[e], k2[:, sl] * (beta[e] * eg)], axis=1).astype(BF16)
            zpad = jnp.zeros_like(rhs)
            uw = dot(t16, jnp.concatenate([rhs, zpad] if e == 0 else [zpad, rhs], axis=0))
            u_refs[d][0, rs, hs] = uw[:, :HEAD_DIM]
            w_refs[d][0, rs, hs] = uw[:, HEAD_DIM:].astype(BF16)
            qg_refs[d][0, rs, hs] = (q2[:, sl] * eg).astype(BF16)
            kg_refs[d][0, rs, hs] = (k2[:, sl] * jnp.exp(g_last - gc[e])).astype(BF16)
            gl_refs[d][0, c, h:h + 1, :] = jnp.broadcast_to(jnp.exp(g_last), (1, HEAD_DIM))


def _chunk_prep(q, k, v, gb):
    bsz, seq, kwid = q.shape
    rows = PREP_CHUNKS * CHUNK
    n = seq // CHUNK
    tok = lambda b, i: (b, i, 0)
    f32_full = jax.ShapeDtypeStruct(q.shape, F32)
    bf_full = jax.ShapeDtypeStruct(q.shape, BF16)
    bf_half = jax.ShapeDtypeStruct((bsz, seq, kwid // 2), BF16)
    gl = jax.ShapeDtypeStruct((bsz, n, N_HEADS, HEAD_DIM), F32)
    outs = [f32_full] * 2 + [bf_full] * 6 + [bf_half] * 2 + [gl] * 2
    out_specs = ([pl.BlockSpec((1, rows, kwid), tok)] * 8 + [pl.BlockSpec((1, rows, kwid // 2), tok)] * 2
                 + [pl.BlockSpec((1, PREP_CHUNKS, N_HEADS, HEAD_DIM), lambda b, i: (b, i, 0, 0))] * 2)
    return pl.pallas_call(
        _chunk_prep_kernel,
        grid=(bsz, seq // rows),
        in_specs=[pl.BlockSpec((1, rows, kwid), tok)] * 3 + [pl.BlockSpec((1, rows, gb.shape[-1]), tok)],
        out_specs=out_specs,
        out_shape=outs,
        compiler_params=_cparams("parallel", "parallel"),
        name="delta_chunk_prep",
    )(q, k, v, gb)


def _scan_kernel(uf_ref, wf_ref, qgf_ref, kgf_ref, qkf_ref, glf_ref,
                 ub_ref, wb_ref, qgb_ref, kgb_ref, qkb_ref, glb_ref, s0f_ref, s0b_ref,
                 of_ref, ob_ref, sff_ref, sfb_ref, s_ref, *, n_chunks, has_s0):
    j = pl.program_id(1)
    n_hd = N_DIRS * N_HEADS

    @pl.when(j == 0)
    def _():
        for r in range(SCAN_ROWS):
            if has_s0:
                s_ref[r * n_hd:r * n_hd + N_HEADS] = s0f_ref[r]
                s_ref[r * n_hd + N_HEADS:(r + 1) * n_hd] = s0b_ref[r]
            else:
                s_ref[r * n_hd:(r + 1) * n_hd] = jnp.zeros((n_hd, HEAD_DIM, HEAD_DIM), F32)

    dot = lambda x, y: jnp.dot(x, y, preferred_element_type=F32)
    zeros16 = jnp.zeros((CHUNK, HEAD_DIM), BF16)
    ins = ((uf_ref, wf_ref, qgf_ref, kgf_ref, qkf_ref, glf_ref), (ub_ref, wb_ref, qgb_ref, kgb_ref, qkb_ref, glb_ref))
    outs = (of_ref, ob_ref)
    hds = [(r, d, h) for r in range(SCAN_ROWS) for d in range(N_DIRS) for h in range(N_HEADS)]
    hsl = lambda h: slice(h * HEAD_DIM, (h + 1) * HEAD_DIM)
    sidx = lambda r, d, h: r * n_hd + d * N_HEADS + h

    s_old, ws_qs = [], []
    for r, d, h in hds:
        s = s_ref[sidx(r, d, h)]
        s_old.append(s)
        lhs = jnp.concatenate([ins[d][1][r, :, hsl(h)], ins[d][2][r, :, hsl(h)]], axis=0)
        ws_qs.append(dot(lhs, s.astype(BF16)))
    v_new = []
    for i, (r, d, h) in enumerate(hds):
        vn = (ins[d][0][r, :, hsl(h)] - ws_qs[i][:CHUNK]).astype(BF16)
        v_new.append(vn)
        v_pad = jnp.concatenate([vn, zeros16] if h % 2 == 0 else [zeros16, vn], axis=0)
        qk_pair = ins[d][4][r, :, hsl(h // 2)]
        outs[d][r, :, hsl(h)] = ws_qs[i][CHUNK:] + dot(qk_pair, v_pad)
    for i, (r, d, h) in enumerate(hds):
        kg = ins[d][3][r, :, hsl(h)]
        upd = lax.dot_general(kg, v_new[i], (((0,), (0,)), ((), ())), preferred_element_type=F32)
        s_ref[sidx(r, d, h)] = s_old[i] * ins[d][5][r, 0, h:h + 1, :] + upd

    @pl.when(j == n_chunks - 1)
    def _():
        for r in range(SCAN_ROWS):
            sff_ref[r] = s_ref[r * n_hd:r * n_hd + N_HEADS]
            sfb_ref[r] = s_ref[r * n_hd + N_HEADS:(r + 1) * n_hd]


def _scan(prep, s0_f, s0_b):
    (u_f, u_b, w_f, w_b, qg_f, qg_b, kg_f, kg_b, qk_f, qk_b, gl_f, gl_b) = prep
    bsz, seq, kwid = u_f.shape
    assert bsz % SCAN_ROWS == 0
    n = seq // CHUNK
    has_s0 = s0_f is not None
    sblk = (SCAN_ROWS, N_HEADS, HEAD_DIM, HEAD_DIM)
    if not has_s0:
        s0_f = s0_b = jnp.zeros(sblk, F32)
    fwd = lambda b, j: (b, j, 0)
    bwd = lambda b, j: (b, n - 1 - j, 0)
    fwd4 = lambda b, j: (b, j, 0, 0)
    bwd4 = lambda b, j: (b, n - 1 - j, 0, 0)
    smap = (lambda b, j: (b, 0, 0, 0)) if has_s0 else (lambda b, j: (0, 0, 0, 0))

    def dir_specs(tok, tok4):
        return ([pl.BlockSpec((SCAN_ROWS, CHUNK, kwid), tok)] * 4
                + [pl.BlockSpec((SCAN_ROWS, CHUNK, kwid // 2), tok),
                   pl.BlockSpec((SCAN_ROWS, 1, N_HEADS, HEAD_DIM), tok4)])

    return pl.pallas_call(
        functools.partial(_scan_kernel, n_chunks=n, has_s0=has_s0),
        grid=(bsz // SCAN_ROWS, n),
        in_specs=dir_specs(fwd, fwd4) + dir_specs(bwd, bwd4) + [pl.BlockSpec(sblk, smap)] * 2,
        out_specs=[pl.BlockSpec((SCAN_ROWS, CHUNK, kwid), fwd),
                   pl.BlockSpec((SCAN_ROWS, CHUNK, kwid), bwd),
                   pl.BlockSpec(sblk, lambda b, j: (b, 0, 0, 0)),
                   pl.BlockSpec(sblk, lambda b, j: (b, 0, 0, 0))],
        out_shape=[jax.ShapeDtypeStruct(u_f.shape, F32)] * 2
                  + [jax.ShapeDtypeStruct((bsz,) + sblk[1:], F32)] * 2,
        scratch_shapes=[pltpu.VMEM((SCAN_ROWS * N_DIRS * N_HEADS, HEAD_DIM, HEAD_DIM), F32)],
        compiler_params=_cparams("parallel", "arbitrary"),
        name="delta_scan",
    )(u_f, w_f, qg_f, kg_f, qk_f, gl_f, u_b, w_b, qg_b, kg_b, qk_b, gl_b, s0_f, s0_b)


def _layer(x, mod, per_batch_mod, s0_f, s0_b, on_grid, lw, norm_final, final_norm, tm, fold):
    (g1, win1, wo1, gm, wc, wq, wz, wab, conv_w, dn_conv_w, alog_row, dtb_row, gn, wt, wbo,
     g2, win2, wo2) = lw
    bsz, seq, d = x.shape
    assert fold == 1 or not per_batch_mod
    folded = lambda a: a.reshape(bsz // fold, fold * seq, a.shape[-1])
    unfolded = lambda a: a.reshape(bsz, seq, a.shape[-1])
    x = _ffn1(folded(x), mod, per_batch_mod, g1, win1, wo1, tm)
    pc, pq, pz, pab = _proj(x, mod, per_batch_mod, gm, wc, wq, wz, wab, tm)
    yc, q, k, v, gb = _prep(unfolded(pc), unfolded(pq), unfolded(pab), conv_w, dn_conv_w, alog_row, dtb_row,
                            GRID_W if on_grid else 1, min(tm, seq))
    o_f, o_b, s_f, s_b = _scan(_chunk_prep(q, k, v, gb), s0_f, s0_b)
    y = _mix_ffn2(x, mod, per_batch_mod, folded(yc), folded(o_f), folded(o_b), pz, gn, wt, wbo,
                  g2, win2, wo2, norm_final, final_norm, tm)
    return unfolded(y), s_f, s_b


def kernel(x_prompt, x_sample, state_dn_fwd, state_dn_bwd, c, c_ctx, w_ada, b_ada, norm_ffn1, w_ffn1_in,
           w_ffn1_out, norm_mix, w_mix_in, conv_w, dn_conv_w, dn_a_log, dn_dt_bias, dn_norm, w_mix_out,
           norm_ffn2, w_ffn2_in, w_ffn2_out, norm_final):
    depth = w_ada.shape[0]
    d = x_prompt.shape[-1]
    n_lat = c.shape[0]
    cwid = conv_w.shape[-1]
    kwid = N_HEADS * HEAD_DIM
    n_gate = 4 * N_HEADS
    row = lambda a: a.reshape(1, -1)

    cvec = jnp.concatenate([c_ctx[None, :], c, jnp.zeros((16 - 1 - n_lat, d), F32)], axis=0)

    xp, xs = x_prompt, x_sample
    new_f, new_b = [], []
    for l in range(depth):
        mod = _modulation(cvec, w_ada[l], b_ada[l]).reshape(16, N_MOD, d)
        mod_ctx, mod_lat = mod[0:1], mod[1:1 + n_lat]

        win1, wo1 = w_ffn1_in[l].astype(BF16), w_ffn1_out[l].astype(BF16)
        win2, wo2 = w_ffn2_in[l].astype(BF16), w_ffn2_out[l].astype(BF16)
        wm = w_mix_in[l].astype(BF16)
        wc = wm[:, :3 * cwid]
        wq = wm[:, 3 * cwid:3 * cwid + 3 * kwid]
        wz = wm[:, 3 * cwid + 3 * kwid:3 * cwid + 4 * kwid]
        wab = jnp.pad(wm[:, 3 * cwid + 4 * kwid:], ((0, 0), (0, HEAD_DIM - n_gate)))
        zpad = jnp.zeros((N_HEADS,), F32)
        alog_row = jnp.concatenate([dn_a_log[l, 0], zpad, dn_a_log[l, 1], zpad,
                                    jnp.zeros((HEAD_DIM - n_gate,), F32)])[None, :]
        dtb_row = jnp.concatenate([dn_dt_bias[l, 0], zpad, dn_dt_bias[l, 1], zpad,
                                   jnp.zeros((HEAD_DIM - n_gate,), F32)])[None, :]
        wo = w_mix_out[l].astype(BF16)
        lw = (row(norm_ffn1[l]), win1, wo1, row(norm_mix[l]), wc, wq, wz, wab, conv_w[l], dn_conv_w[l],
              alog_row, dtb_row, row(dn_norm[l]), wo[:cwid], wo[cwid:], row(norm_ffn2[l]), win2, wo2)
        last = l == depth - 1
        xp, sf, sb = _layer(xp, mod_ctx, False, None, None, False, lw, row(norm_final), last,
                            TOKEN_TILE, TOKEN_TILE // xp.shape[1])
        new_f.append(sf)
        new_b.append(sb)
        xs, _, _ = _layer(xs, mod_lat, True, state_dn_fwd[:, l], state_dn_bwd[:, l], True, lw,
                          row(norm_final), last, TOKEN_TILE, 1)
    return xp, xs, jnp.stack(new_f, axis=1), jnp.stack(new_b, axis=1)
```

```python
import functools

import jax
import jax.numpy as jnp
from jax import lax
from jax.experimental import pallas as pl
from jax.experimental.pallas import tpu as pltpu

F32 = jnp.float32
BF16 = jnp.bfloat16

EPS = 1e-6
CHUNK = 64
GRID_W = 64
N_HEADS = 4
HEAD_DIM = 128
BF16_ROWS = 16
N_PAIRS = N_HEADS // 2
N_DIRS = 2
N_MOD = 9
FF_CHUNK = 256
TOKEN_TILE = 512
PREP_CHUNKS = 4
SCAN_ROWS = 4
VMEM_LIMIT = 56 * 1024 * 1024


def _cparams(*sem):
    return pltpu.CompilerParams(dimension_semantics=sem, vmem_limit_bytes=VMEM_LIMIT)


def _bdot(a, b):
    return jnp.dot(a.astype(BF16), b.astype(BF16), preferred_element_type=F32)


def _split_bf16(a):
    hi = a.astype(BF16)
    lo = (a - hi.astype(F32)).astype(BF16)
    return hi, lo


def _dot_x3(a, b):
    a_hi, a_lo = _split_bf16(a)
    b_hi, b_lo = _split_bf16(b)
    d = lambda x, y: jnp.dot(x, y, preferred_element_type=F32)
    return d(a_hi, b_hi) + (d(a_hi, b_lo) + d(a_lo, b_hi))


def _sigmoid(x):
    return 1.0 / (1.0 + jnp.exp(-x))


def _silu(x):
    return x * _sigmoid(x)


def _rms(x):
    return x * lax.rsqrt(jnp.mean(x * x, axis=-1, keepdims=True) + EPS)


def _mod_kernel(c_ref, w_ref, b_ref, o_ref):
    s = _silu(c_ref[...])
    o_ref[...] = _dot_x3(s, w_ref[...]) + b_ref[...]


def _modulation(cvec, w_ada, b_ada):
    rows, d = cvec.shape
    n = w_ada.shape[1]
    tn = 1536
    return pl.pallas_call(
        _mod_kernel,
        grid=(n // tn,),
        in_specs=[pl.BlockSpec((rows, d), lambda i: (0, 0)),
                  pl.BlockSpec((d, tn), lambda i: (0, i)),
                  pl.BlockSpec((1, tn), lambda i: (0, i))],
        out_specs=pl.BlockSpec((rows, tn), lambda i: (0, i)),
        out_shape=jax.ShapeDtypeStruct((rows, n), F32),
        compiler_params=_cparams("arbitrary"),
        name="modulation",
    )(cvec, w_ada, b_ada.reshape(1, n))


def _swiglu_residual(x, x_keep_ref, mod_ref, mod_row, g_ref, win_ref, wo_ref, h_ref, acc_ref):
    sh = mod_ref[0, mod_row:mod_row + 1, :]
    sc = mod_ref[0, mod_row + 1:mod_row + 2, :]
    gt = mod_ref[0, mod_row + 2:mod_row + 3, :]
    h_ref[...] = (_rms(x) * g_ref[...] * (1.0 + sc) + sh).astype(BF16)
    acc_ref[...] = jnp.zeros_like(acc_ref)
    ff = wo_ref.shape[0]
    for c in range(ff // FF_CHUNK):
        lo, hi = c * FF_CHUNK, (c + 1) * FF_CHUNK
        h = h_ref[...]
        a = jnp.dot(h, win_ref[:, lo:hi], preferred_element_type=F32)
        b = jnp.dot(h, win_ref[:, ff + lo:ff + hi], preferred_element_type=F32)
        act = (_silu(a) * b).astype(BF16)
        acc_ref[...] += jnp.dot(act, wo_ref[lo:hi, :], preferred_element_type=F32)
    return x_keep_ref[...] + 0.5 * gt * acc_ref[...]


def _ffn1_kernel(x_ref, mod_ref, g_ref, win_ref, wo_ref, o_ref, h_ref, acc_ref):
    o_ref[0] = _swiglu_residual(x_ref[0], x_ref.at[0], mod_ref, 0, g_ref, win_ref, wo_ref, h_ref, acc_ref)


def _mix_ffn2_kernel(x_ref, mod_ref, yc_ref, of_ref, ob_ref, z_ref, gn_ref, wt_ref, wbm_ref,
                     g_ref, win_ref, wo_ref, gfin_ref, o_ref, h_ref, acc_ref, x2_ref, *, final_norm):
    o = of_ref[0] + ob_ref[0]
    gate = _silu(z_ref[0].astype(F32))
    gn = gn_ref[...]
    ys = []
    for h in range(N_HEADS):
        sl = slice(h * HEAD_DIM, (h + 1) * HEAD_DIM)
        ys.append(_rms(o[:, sl]) * gn * gate[:, sl])
    y_dn = jnp.concatenate(ys, axis=1)
    m = _bdot(yc_ref[0], wt_ref[...]) + _bdot(y_dn, wbm_ref[...])
    x2_ref[...] = x_ref[0] + mod_ref[0, 5:6, :] * m
    y = _swiglu_residual(x2_ref[...], x2_ref, mod_ref, 6, g_ref, win_ref, wo_ref, h_ref, acc_ref)
    if final_norm:
        y = _rms(y) * gfin_ref[...]
    o_ref[0] = y


def _token_specs(d, tm, per_batch_mod):
    mod_map = (lambda b, i: (b, 0, 0)) if per_batch_mod else (lambda b, i: (0, 0, 0))
    return pl.BlockSpec((1, tm, d), lambda b, i: (b, i, 0)), pl.BlockSpec((1, N_MOD, d), mod_map)


def _ffn_weight_specs(win, wo):
    assert win.shape[1] == 2 * wo.shape[0] and wo.shape[0] % FF_CHUNK == 0
    const2 = lambda b, i: (0, 0)
    return [pl.BlockSpec(win.shape, const2, pipeline_mode=pl.Buffered(1)),
            pl.BlockSpec(wo.shape, const2, pipeline_mode=pl.Buffered(1))]


def _ffn1(x, mod, per_batch_mod, g, win, wo, tm):
    bsz, seq, d = x.shape
    const2 = lambda b, i: (0, 0)
    x_spec, mod_spec = _token_specs(d, tm, per_batch_mod)
    return pl.pallas_call(
        _ffn1_kernel,
        grid=(bsz, seq // tm),
        in_specs=[x_spec, mod_spec, pl.BlockSpec((1, d), const2)] + _ffn_weight_specs(win, wo),
        out_specs=x_spec,
        out_shape=jax.ShapeDtypeStruct(x.shape, F32),
        scratch_shapes=[pltpu.VMEM((tm, d), BF16), pltpu.VMEM((tm, d), F32)],
        compiler_params=_cparams("parallel", "arbitrary"),
        name="ffn1",
    )(x, mod, g, win, wo)


def _mix_ffn2(x, mod, per_batch_mod, yc, o_f, o_b, z, gn, wt, wbm, g, win, wo, gfin, final_norm, tm):
    bsz, seq, d = x.shape
    half = yc.shape[-1]
    const2 = lambda b, i: (0, 0)
    x_spec, mod_spec = _token_specs(d, tm, per_batch_mod)
    half_spec = pl.BlockSpec((1, tm, half), lambda b, i: (b, i, 0))
    return pl.pallas_call(
        functools.partial(_mix_ffn2_kernel, final_norm=final_norm),
        grid=(bsz, seq // tm),
        in_specs=[x_spec, mod_spec] + [half_spec] * 4
                 + [pl.BlockSpec((1, HEAD_DIM), const2),
                    pl.BlockSpec((half, d), const2, pipeline_mode=pl.Buffered(1)),
                    pl.BlockSpec((half, d), const2, pipeline_mode=pl.Buffered(1)),
                    pl.BlockSpec((1, d), const2)]
                 + _ffn_weight_specs(win, wo) + [pl.BlockSpec((1, d), const2)],
        out_specs=x_spec,
        out_shape=jax.ShapeDtypeStruct(x.shape, F32),
        scratch_shapes=[pltpu.VMEM((tm, d), BF16), pltpu.VMEM((tm, d), F32), pltpu.VMEM((tm, d), F32)],
        compiler_params=_cparams("parallel", "arbitrary"),
        name="mix_out_ffn2",
    )(x, mod, yc, o_f, o_b, z, gn, wt, wbm, g, win, wo, gfin)


def _proj_kernel(x_ref, mod_ref, g_ref, wc_ref, wq_ref, wz_ref, wab_ref, pc_ref, pq_ref, pz_ref, pab_ref):
    x = x_ref[0]
    sh = mod_ref[0, 3:4, :]
    sc = mod_ref[0, 4:5, :]
    h = (_rms(x) * g_ref[...] * (1.0 + sc) + sh).astype(BF16)
    pc_ref[0] = jnp.dot(h, wc_ref[...], preferred_element_type=F32).astype(pc_ref.dtype)
    pq_ref[0] = jnp.dot(h, wq_ref[...], preferred_element_type=F32).astype(pq_ref.dtype)
    pz_ref[0] = jnp.dot(h, wz_ref[...], preferred_element_type=F32).astype(pz_ref.dtype)
    pab_ref[0] = jnp.dot(h, wab_ref[...], preferred_element_type=F32)


def _proj(x, mod, per_batch_mod, g, wc, wq, wz, wab, tm):
    bsz, seq, d = x.shape
    mod_map = (lambda b, i: (b, 0, 0)) if per_batch_mod else (lambda b, i: (0, 0, 0))
    const2 = lambda b, i: (0, 0)
    tok = lambda b, i: (b, i, 0)
    widths = (wc.shape[1], wq.shape[1], wz.shape[1], wab.shape[1])
    return pl.pallas_call(
        _proj_kernel,
        grid=(bsz, seq // tm),
        in_specs=[pl.BlockSpec((1, tm, d), tok),
                  pl.BlockSpec((1, N_MOD, d), mod_map),
                  pl.BlockSpec((1, d), const2)]
                 + [pl.BlockSpec((d, w), const2, pipeline_mode=pl.Buffered(1)) for w in widths],
        out_specs=[pl.BlockSpec((1, tm, w), tok) for w in widths],
        out_shape=[jax.ShapeDtypeStruct((bsz, seq, w), dt) for w, dt in zip(widths, (BF16, BF16, BF16, F32))],
        compiler_params=_cparams("parallel", "arbitrary"),
        name="mix_in_proj",
    )(x, mod, g, wc, wq, wz, wab)


def _shift_rows(cur, prev_halo, next_halo, dist, first, last):
    tm = cur.shape[0]
    hp = jnp.where(first, 0.0, prev_halo)
    hn = jnp.where(last, 0.0, next_halo)
    if dist % 8 == 0:
        prev = jnp.concatenate([hp[hp.shape[0] - dist:], cur[:tm - dist]], axis=0)
        nxt = jnp.concatenate([cur[dist:], hn[:dist]], axis=0)
        return prev, nxt
    assert dist == 1
    row = lax.broadcasted_iota(jnp.int32, cur.shape, 0)
    prev = jnp.where(row == 0, hp[hp.shape[0] - 1:], pltpu.roll(cur, 1, axis=0))
    nxt = jnp.where(row == tm - 1, hn[:1], pltpu.roll(cur, tm - 1, axis=0))
    return prev, nxt


def _prep_kernel(pc_ref, pcp_ref, pcn_ref, pq_ref, pqp_ref, pqn_ref, pab_ref,
                 cw_ref, dw_ref, alog_ref, dtb_ref,
                 yc_ref, q_ref, k_ref, v_ref, gb_ref, *, conv_dist, n_tiles):
    i = pl.program_id(1)
    first = i == 0
    last = i == n_tiles - 1
    cwid = yc_ref.shape[-1]
    kwid = q_ref.shape[-1]

    pc = pc_ref[0].astype(F32)
    u = pc[:, cwid:2 * cwid] * pc[:, 2 * cwid:]
    pcp = pcp_ref[0].astype(F32)
    pcn = pcn_ref[0].astype(F32)
    up, un = _shift_rows(u, pcp[:, cwid:2 * cwid] * pcp[:, 2 * cwid:],
                         pcn[:, cwid:2 * cwid] * pcn[:, 2 * cwid:], conv_dist, first, last)
    cw = cw_ref[...]
    yc_ref[0] = (pc[:, :cwid] * (up * cw[0:1] + u * cw[1:2] + un * cw[2:3])).astype(yc_ref.dtype)

    pq = pq_ref[0].astype(F32)
    qp, qn = _shift_rows(pq, pqp_ref[0].astype(F32), pqn_ref[0].astype(F32), 1, first, last)
    dw = dw_ref[...]
    qkv = _silu(qp * dw[0:1] + pq * dw[1:2] + qn * dw[2:3])
    for h in range(N_HEADS):
        sl = slice(h * HEAD_DIM, (h + 1) * HEAD_DIM)
        qh = qkv[:, sl]
        kh = qkv[:, kwid + h * HEAD_DIM: kwid + (h + 1) * HEAD_DIM]
        qn_h = qh * (lax.rsqrt(jnp.sum(qh * qh, axis=-1, keepdims=True) + EPS) * HEAD_DIM ** -0.5)
        q_ref[0, :, sl] = qn_h.astype(q_ref.dtype)
        k_ref[0, :, sl] = (kh * lax.rsqrt(jnp.sum(kh * kh, axis=-1, keepdims=True) + EPS)).astype(k_ref.dtype)
    v_ref[0] = qkv[:, 2 * kwid:].astype(v_ref.dtype)

    ab = pab_ref[0]
    lane = lax.broadcasted_iota(jnp.int32, ab.shape, 1)
    is_gf = lane < N_HEADS
    is_gb = (lane >= 2 * N_HEADS) & (lane < 3 * N_HEADS)
    xg = ab + dtb_ref[...]
    softplus = jnp.maximum(xg, 0.0) + jnp.log1p(jnp.exp(-jnp.abs(xg)))
    gates = jnp.where(is_gf | is_gb, -jnp.exp(alog_ref[...]) * softplus, _sigmoid(ab))
    r = lax.broadcasted_iota(jnp.int32, (CHUNK, CHUNK), 0)
    c = lax.broadcasted_iota(jnp.int32, (CHUNK, CHUNK), 1)
    tril = (r >= c).astype(BF16)
    triu = (r <= c).astype(BF16)
    g_hi = gates.astype(BF16)
    rem = gates - g_hi.astype(F32)
    g_mid = rem.astype(BF16)
    g_lo = (rem - g_mid.astype(F32)).astype(BF16)
    d = lambda m, x: jnp.dot(m, x, preferred_element_type=F32)
    lane_c = lax.broadcasted_iota(jnp.int32, (CHUNK, ab.shape[1]), 1)
    is_gf_c = lane_c < N_HEADS
    is_gb_c = (lane_c >= 2 * N_HEADS) & (lane_c < 3 * N_HEADS)
    for t in range(ab.shape[0] // CHUNK):
        rs = slice(t * CHUNK, (t + 1) * CHUNK)
        parts = (g_hi[rs], g_mid[rs], g_lo[rs])
        pre = d(tril, parts[0]) + (d(tril, parts[1]) + d(tril, parts[2]))
        suf = d(triu, parts[0]) + (d(triu, parts[1]) + d(triu, parts[2]))
        gb_ref[0, rs, :] = jnp.where(is_gf_c, pre, jnp.where(is_gb_c, suf, gates[rs]))


def _prep(pc, pq, pab, conv_w, dn_conv_w, alog_row, dtb_row, conv_dist, tm):
    bsz, seq, wc3 = pc.shape
    wq3 = pq.shape[-1]
    cwid, kwid = wc3 // 3, wq3 // 3
    n_tiles = seq // tm
    hq = BF16_ROWS
    hc = max(conv_dist, hq)
    tok = lambda b, i: (b, i, 0)
    const2 = lambda b, i: (0, 0)

    def prev_map(hrows):
        return lambda b, i: (b, jnp.maximum(i * (tm // hrows) - 1, 0), 0)

    def next_map(hrows):
        return lambda b, i: (b, jnp.minimum((i + 1) * (tm // hrows), seq // hrows - 1), 0)

    outs = [jax.ShapeDtypeStruct((bsz, seq, w), dt)
            for w, dt in ((cwid, BF16), (kwid, BF16), (kwid, BF16), (kwid, BF16), (pab.shape[-1], F32))]
    return pl.pallas_call(
        functools.partial(_prep_kernel, conv_dist=conv_dist, n_tiles=n_tiles),
        grid=(bsz, n_tiles),
        in_specs=[pl.BlockSpec((1, tm, wc3), tok),
                  pl.BlockSpec((1, hc, wc3), prev_map(hc)),
                  pl.BlockSpec((1, hc, wc3), next_map(hc)),
                  pl.BlockSpec((1, tm, wq3), tok),
                  pl.BlockSpec((1, hq, wq3), prev_map(hq)),
                  pl.BlockSpec((1, hq, wq3), next_map(hq)),
                  pl.BlockSpec((1, tm, pab.shape[-1]), tok),
                  pl.BlockSpec(conv_w.shape, const2),
                  pl.BlockSpec(dn_conv_w.shape, const2),
                  pl.BlockSpec(alog_row.shape, const2),
                  pl.BlockSpec(dtb_row.shape, const2)],
        out_specs=[pl.BlockSpec((1, tm, o.shape[-1]), tok) for o in outs],
        out_shape=outs,
        compiler_params=_cparams("parallel", "arbitrary"),
        name="mixer_prep",
    )(pc, pc, pc, pq, pq, pq, pab, conv_w, dn_conv_w, alog_row, dtb_row)


def _pair_block_diag(y16, keep_left, keep_right):
    return jnp.concatenate([y16 * keep_left, y16 * keep_right], axis=0)


INV_BASE = 8


def _pair_matmul_x3(lhs_list, rhs, keep_left, keep_right):
    d = lambda x, y: jnp.dot(x, y, preferred_element_type=F32)
    r_hi, r_lo = _split_bf16(rhs)
    r_hi = _pair_block_diag(r_hi, keep_left, keep_right)
    r_lo = _pair_block_diag(r_lo, keep_left, keep_right)
    parts = [_split_bf16(x) for x in lhs_list]
    l1 = jnp.concatenate([jnp.concatenate([hi, lo], axis=1) for hi, lo in parts], axis=0)
    l2 = jnp.concatenate([hi for hi, _ in parts], axis=0)
    res = d(l1, jnp.concatenate([r_hi, r_hi], axis=0)) + d(l2, r_lo)
    return [res[i * CHUNK:(i + 1) * CHUNK] for i in range(len(lhs_list))]


def _pair_inverses(a_list, row, col, eye_pair, keep_left, keep_right):
    mm = lambda lhs_list, rhs: _pair_matmul_x3(lhs_list, rhs, keep_left, keep_right)
    shift = INV_BASE.bit_length() - 1
    same_base = (row >> shift) == (col >> shift)
    ps = [jnp.where(same_base, -a, 0.0) for a in a_list]
    ts = [eye_pair + p for p in ps]
    n_sq = shift - 1
    for i in range(n_sq + 1):
        new_ps, new_ts = [], []
        for p, t in zip(ps, ts):
            if i == 0:
                new_ps.append(mm([p], p)[0])
                new_ts.append(t)
            elif i == n_sq:
                new_ps.append(p)
                new_ts.append(t + mm([t], p)[0])
            else:
                p2, tp = mm([p, t], p)
                new_ps.append(p2)
                new_ts.append(t + tp)
        ps, ts = new_ps, new_ts
    size = INV_BASE
    while size < CHUNK:
        s1 = size.bit_length() - 1
        off = ((row >> (s1 + 1)) == (col >> (s1 + 1))) & ((row >> s1) != (col >> s1))
        es = [jnp.where(off, a, 0.0) for a in a_list]
        xs = [mm([e], t)[0] for e, t in zip(es, ts)]
        ts = [t - mm([t], x)[0] for t, x in zip(ts, xs)]
        size *= 2
    return ts


def _chunk_prep_kernel(q_ref, k_ref, v_ref, g_ref,
                       uf_ref, ub_ref, wf_ref, wb_ref, qgf_ref, qgb_ref, kgf_ref, kgb_ref,
                       qkf_ref, qkb_ref, glf_ref, glb_ref):
    u_refs, w_refs, qg_refs = (uf_ref, ub_ref), (wf_ref, wb_ref), (qgf_ref, qgb_ref)
    kg_refs, qk_refs, gl_refs = (kgf_ref, kgb_ref), (qkf_ref, qkb_ref), (glf_ref, glb_ref)
    row = lax.broadcasted_iota(jnp.int32, (CHUNK, 2 * CHUNK), 0)
    lane = lax.broadcasted_iota(jnp.int32, (CHUNK, 2 * CHUNK), 1)
    col = lane & (CHUNK - 1)
    left = lane < CHUNK
    keep_left = jnp.where(left, 1.0, 0.0).astype(BF16)
    keep_right = jnp.where(left, 0.0, 1.0).astype(BF16)
    eye_pair = jnp.where(row == col, 1.0, 0.0)
    zeros16 = jnp.zeros((CHUNK, HEAD_DIM), BF16)
    dot = lambda x, y: jnp.dot(x, y, preferred_element_type=F32)

    chains, a_list = [], []
    for c in range(PREP_CHUNKS):
        rs = slice(c * CHUNK, (c + 1) * CHUNK)
        gates = g_ref[0, rs, :]
        for p in range(N_PAIRS):
            sl = slice(2 * p * HEAD_DIM, 2 * (p + 1) * HEAD_DIM)
            q16, k16 = q_ref[0, rs, sl], k_ref[0, rs, sl]
            q2, k2, v2 = q16.astype(F32), k16.astype(F32), v_ref[0, rs, sl].astype(F32)
            k_bd = jnp.concatenate([jnp.concatenate([k16[:, :HEAD_DIM], zeros16], axis=1),
                                    jnp.concatenate([zeros16, k16[:, HEAD_DIM:]], axis=1)], axis=0)
            kq = jnp.concatenate([k16, q16], axis=0)
            gram = lax.dot_general(kq, k_bd, (((1,), (1,)), ((), ())), preferred_element_type=F32)
            for d in range(N_DIRS):
                gcol, bcol = 2 * d * N_HEADS, (2 * d + 1) * N_HEADS
                gc = [gates[:, gcol + 2 * p + e:gcol + 2 * p + e + 1] for e in range(2)]
                beta = [gates[:, bcol + 2 * p + e:bcol + 2 * p + e + 1] for e in range(2)]
                gc_pair = jnp.where(left, gc[0], gc[1])
                beta_pair = jnp.where(left, beta[0], beta[1])
                gr_pair = jnp.sum(jnp.where(row == col, gc_pair, 0.0), axis=0, keepdims=True)
                if d == 0:
                    incl, strict, last = row >= col, row > col, CHUNK - 1
                else:
                    incl, strict, last = row <= col, row < col, 0
                decay = jnp.where(incl, jnp.exp(jnp.where(incl, gc_pair - gr_pair, 0.0)), 0.0)
                a_list.append(jnp.where(strict, beta_pair * gram[:CHUNK] * decay, 0.0))
                qk_refs[d][0, rs, p * HEAD_DIM:(p + 1) * HEAD_DIM] = (
                    jnp.where(incl, gram[CHUNK:] * decay, 0.0).astype(BF16))
                chains.append((c, rs, p, d, q2, k2, v2, gc, beta, last))

    t_list = _pair_inverses(a_list, row, col, eye_pair, keep_left, keep_right)

    for (c, rs, p, d, q2, k2, v2, gc, beta, last), t in zip(chains, t_list):
        t16 = t.astype(BF16)
        for e in range(2):
            h = 2 * p + e
            sl = slice(e * HEAD_DIM, (e + 1) * HEAD_DIM)
            hs = slice(h * HEAD_DIM, (h + 1) * HEAD_DIM)
            eg = jnp.exp(gc[e])
            g_last = gc[e][last:last + 1, :]
            rhs = jnp.concatenate([v2[:, sl] * beta[e], k2[:, sl] * (beta[e] * eg)], axis=1).astype(BF16)
            zpad = jnp.zeros_like(rhs)
            uw = dot(t16, jnp.concatenate([rhs, zpad] if e == 0 else [zpad, rhs], axis=0))
            u_refs[d][0, rs, hs] = uw[:, :HEAD_DIM]
            w_refs[d][0, rs, hs] = uw[:, HEAD_DIM:].astype(BF16)
            qg_refs[d][0, rs, hs] = (q2[:, sl] * eg).astype(BF16)
            kg_refs[d][0, rs, hs] = (k2[:, sl] * jnp.exp(g_last - gc[e])).astype(BF16)
            gl_refs[d][0, c, h:h + 1, :] = jnp.broadcast_to(jnp.exp(g_last), (1, HEAD_DIM))


def _chunk_prep(q, k, v, gb):
    bsz, seq, kwid = q.shape
    rows = PREP_CHUNKS * CHUNK
    n = seq // CHUNK
    tok = lambda b, i: (b, i, 0)
    f32_full = jax.ShapeDtypeStruct(q.shape, F32)
    bf_full = jax.ShapeDtypeStruct(q.shape, BF16)
    bf_half = jax.ShapeDtypeStruct((bsz, seq, kwid // 2), BF16)
    gl = jax.ShapeDtypeStruct((bsz, n, N_HEADS, HEAD_DIM), F32)
    outs = [f32_full] * 2 + [bf_full] * 6 + [bf_half] * 2 + [gl] * 2
    out_specs = ([pl.BlockSpec((1, rows, kwid), tok)] * 8 + [pl.BlockSpec((1, rows, kwid // 2), tok)] * 2
                 + [pl.BlockSpec((1, PREP_CHUNKS, N_HEADS, HEAD_DIM), lambda b, i: (b, i, 0, 0))] * 2)
    return pl.pallas_call(
        _chunk_prep_kernel,
        grid=(bsz, seq // rows),
        in_specs=[pl.BlockSpec((1, rows, kwid), tok)] * 3 + [pl.BlockSpec((1, rows, gb.shape[-1]), tok)],
        out_specs=out_specs,
        out_shape=outs,
        compiler_params=_cparams("parallel", "parallel"),
        name="delta_chunk_prep",
    )(q, k, v, gb)


def _scan_kernel(uf_ref, wf_ref, qgf_ref, kgf_ref, qkf_ref, glf_ref,
                 ub_ref, wb_ref, qgb_ref, kgb_ref, qkb_ref, glb_ref, s0f_ref, s0b_ref,
                 of_ref, ob_ref, sff_ref, sfb_ref, s_ref, *, n_chunks, has_s0):
    j = pl.program_id(1)
    n_hd = N_DIRS * N_HEADS

    @pl.when(j == 0)
    def _():
        for r in range(SCAN_ROWS):
            if has_s0:
                s_ref[r * n_hd:r * n_hd + N_HEADS] = s0f_ref[r]
                s_ref[r * n_hd + N_HEADS:(r + 1) * n_hd] = s0b_ref[r]
            else:
                s_ref[r * n_hd:(r + 1) * n_hd] = jnp.zeros((n_hd, HEAD_DIM, HEAD_DIM), F32)

    dot = lambda x, y: jnp.dot(x, y, preferred_element_type=F32)
    zeros16 = jnp.zeros((CHUNK, HEAD_DIM), BF16)
    ins = ((uf_ref, wf_ref, qgf_ref, kgf_ref, qkf_ref, glf_ref), (ub_ref, wb_ref, qgb_ref, kgb_ref, qkb_ref, glb_ref))
    outs = (of_ref, ob_ref)
    hds = [(r, d, h) for r in range(SCAN_ROWS) for d in range(N_DIRS) for h in range(N_HEADS)]
    hsl = lambda h: slice(h * HEAD_DIM, (h + 1) * HEAD_DIM)
    sidx = lambda r, d, h: r * n_hd + d * N_HEADS + h

    s_old, ws_qs = [], []
    for r, d, h in hds:
        s = s_ref[sidx(r, d, h)]
        s_old.append(s)
        lhs = jnp.concatenate([ins[d][1][r, :, hsl(h)], ins[d][2][r, :, hsl(h)]], axis=0)
        ws_qs.append(dot(lhs, s.astype(BF16)))
    v_new = []
    for i, (r, d, h) in enumerate(hds):
        vn = (ins[d][0][r, :, hsl(h)] - ws_qs[i][:CHUNK]).astype(BF16)
        v_new.append(vn)
        v_pad = jnp.concatenate([vn, zeros16] if h % 2 == 0 else [zeros16, vn], axis=0)
        qk_pair = ins[d][4][r, :, hsl(h // 2)]
        outs[d][r, :, hsl(h)] = ws_qs[i][CHUNK:] + dot(qk_pair, v_pad)
    for i, (r, d, h) in enumerate(hds):
        kg = ins[d][3][r, :, hsl(h)]
        upd = lax.dot_general(kg, v_new[i], (((0,), (0,)), ((), ())), preferred_element_type=F32)
        s_ref[sidx(r, d, h)] = s_old[i] * ins[d][5][r, 0, h:h + 1, :] + upd

    @pl.when(j == n_chunks - 1)
    def _():
        for r in range(SCAN_ROWS):
            sff_ref[r] = s_ref[r * n_hd:r * n_hd + N_HEADS]
            sfb_ref[r] = s_ref[r * n_hd + N_HEADS:(r + 1) * n_hd]


def _scan(prep, s0_f, s0_b):
    (u_f, u_b, w_f, w_b, qg_f, qg_b, kg_f, kg_b, qk_f, qk_b, gl_f, gl_b) = prep
    bsz, seq, kwid = u_f.shape
    assert bsz % SCAN_ROWS == 0
    n = seq // CHUNK
    has_s0 = s0_f is not None
    sblk = (SCAN_ROWS, N_HEADS, HEAD_DIM, HEAD_DIM)
    if not has_s0:
        s0_f = s0_b = jnp.zeros(sblk, F32)
    fwd = lambda b, j: (b, j, 0)
    bwd = lambda b, j: (b, n - 1 - j, 0)
    fwd4 = lambda b, j: (b, j, 0, 0)
    bwd4 = lambda b, j: (b, n - 1 - j, 0, 0)
    smap = (lambda b, j: (b, 0, 0, 0)) if has_s0 else (lambda b, j: (0, 0, 0, 0))

    def dir_specs(tok, tok4):
        return ([pl.BlockSpec((SCAN_ROWS, CHUNK, kwid), tok)] * 4
                + [pl.BlockSpec((SCAN_ROWS, CHUNK, kwid // 2), tok),
                   pl.BlockSpec((SCAN_ROWS, 1, N_HEADS, HEAD_DIM), tok4)])

    return pl.pallas_call(
        functools.partial(_scan_kernel, n_chunks=n, has_s0=has_s0),
        grid=(bsz // SCAN_ROWS, n),
        in_specs=dir_specs(fwd, fwd4) + dir_specs(bwd, bwd4) + [pl.BlockSpec(sblk, smap)] * 2,
        out_specs=[pl.BlockSpec((SCAN_ROWS, CHUNK, kwid), fwd),
                   pl.BlockSpec((SCAN_ROWS, CHUNK, kwid), bwd),
                   pl.BlockSpec(sblk, lambda b, j: (b, 0, 0, 0)),
                   pl.BlockSpec(sblk, lambda b, j: (b, 0, 0, 0))],
        out_shape=[jax.ShapeDtypeStruct(u_f.shape, F32)] * 2
                  + [jax.ShapeDtypeStruct((bsz,) + sblk[1:], F32)] * 2,
        scratch_shapes=[pltpu.VMEM((SCAN_ROWS * N_DIRS * N_HEADS, HEAD_DIM, HEAD_DIM), F32)],
        compiler_params=_cparams("parallel", "arbitrary"),
        name="delta_scan",
    )(u_f, w_f, qg_f, kg_f, qk_f, gl_f, u_b, w_b, qg_b, kg_b, qk_b, gl_b, s0_f, s0_b)


def _layer(x, mod, per_batch_mod, s0_f, s0_b, on_grid, lw, norm_final, final_norm, tm, fold):
    (g1, win1, wo1, gm, wc, wq, wz, wab, conv_w, dn_conv_w, alog_row, dtb_row, gn, wt, wbo,
     g2, win2, wo2) = lw
    bsz, seq, d = x.shape
    assert fold == 1 or not per_batch_mod
    folded = lambda a: a.reshape(bsz // fold, fold * seq, a.shape[-1])
    unfolded = lambda a: a.reshape(bsz, seq, a.shape[-1])
    x = _ffn1(folded(x), mod, per_batch_mod, g1, win1, wo1, tm)
    pc, pq, pz, pab = _proj(x, mod, per_batch_mod, gm, wc, wq, wz, wab, tm)
    yc, q, k, v, gb = _prep(unfolded(pc), unfolded(pq), unfolded(pab), conv_w, dn_conv_w, alog_row, dtb_row,
                            GRID_W if on_grid else 1, min(tm, seq))
    o_f, o_b, s_f, s_b = _scan(_chunk_prep(q, k, v, gb), s0_f, s0_b)
    y = _mix_ffn2(x, mod, per_batch_mod, folded(yc), folded(o_f), folded(o_b), pz, gn, wt, wbo,
                  g2, win2, wo2, norm_final, final_norm, tm)
    return unfolded(y), s_f, s_b


def kernel(x_prompt, x_sample, state_dn_fwd, state_dn_bwd, c, c_ctx, w_ada, b_ada, norm_ffn1, w_ffn1_in,
           w_ffn1_out, norm_mix, w_mix_in, conv_w, dn_conv_w, dn_a_log, dn_dt_bias, dn_norm, w_mix_out,
           norm_ffn2, w_ffn2_in, w_ffn2_out, norm_final):
    depth = w_ada.shape[0]
    d = x_prompt.shape[-1]
    n_lat = c.shape[0]
    cwid = conv_w.shape[-1]
    kwid = N_HEADS * HEAD_DIM
    n_gate = 4 * N_HEADS
    row = lambda a: a.reshape(1, -1)

    cvec = jnp.concatenate([c_ctx[None, :], c, jnp.zeros((16 - 1 - n_lat, d), F32)], axis=0)

    xp, xs = x_prompt, x_sample
    new_f, new_b = [], []
    for l in range(depth):
        mod = _modulation(cvec, w_ada[l], b_ada[l]).reshape(16, N_MOD, d)
        mod_ctx, mod_lat = mod[0:1], mod[1:1 + n_lat]

        win1, wo1 = w_ffn1_in[l].astype(BF16), w_ffn1_out[l].astype(BF16)
        win2, wo2 = w_ffn2_in[l].astype(BF16), w_ffn2_out[l].astype(BF16)
        wm = w_mix_in[l].astype(BF16)
        wc = wm[:, :3 * cwid]
        wq = wm[:, 3 * cwid:3 * cwid + 3 * kwid]
        wz = wm[:, 3 * cwid + 3 * kwid:3 * cwid + 4 * kwid]
        wab = jnp.pad(wm[:, 3 * cwid + 4 * kwid:], ((0, 0), (0, HEAD_DIM - n_gate)))
        zpad = jnp.zeros((N_HEADS,), F32)
        alog_row = jnp.concatenate([dn_a_log[l, 0], zpad, dn_a_log[l, 1], zpad,
                                    jnp.zeros((HEAD_DIM - n_gate,), F32)])[None, :]
        dtb_row = jnp.concatenate([dn_dt_bias[l, 0], zpad, dn_dt_bias[l, 1], zpad,
                                   jnp.zeros((HEAD_DIM - n_gate,), F32)])[None, :]
        wo = w_mix_out[l].astype(BF16)
        lw = (row(norm_ffn1[l]), win1, wo1, row(norm_mix[l]), wc, wq, wz, wab, conv_w[l], dn_conv_w[l],
              alog_row, dtb_row, row(dn_norm[l]), wo[:cwid], wo[cwid:], row(norm_ffn2[l]), win2, wo2)
        last = l == depth - 1
        xp, sf, sb = _layer(xp, mod_ctx, False, None, None, False, lw, row(norm_final), last,
                            TOKEN_TILE, TOKEN_TILE // xp.shape[1])
        new_f.append(sf)
        new_b.append(sb)
        xs, _, _ = _layer(xs, mod_lat, True, state_dn_fwd[:, l], state_dn_bwd[:, l], True, lw,
                          row(norm_final), last, TOKEN_TILE, 1)
    return xp, xs, jnp.stack(new_f, axis=1), jnp.stack(new_b, axis=1)
```

```python
import functools

import jax
import jax.numpy as jnp
from jax import lax
from jax.experimental import pallas as pl
from jax.experimental.pallas import tpu as pltpu

F32 = jnp.float32
BF16 = jnp.bfloat16

EPS = 1e-6
CHUNK = 64
GRID_W = 64
N_HEADS = 4
HEAD_DIM = 128
N_PAIRS = N_HEADS // 2
N_DIRS = 2
N_MOD = 9
FF_CHUNK = 256
TOKEN_TILE = 512
PREP_CHUNKS = 4
SCAN_ROWS = 4
MM_W = 0
MM_QG = MM_W + N_HEADS * HEAD_DIM
MM_KG = MM_QG + N_HEADS * HEAD_DIM
MM_QK = MM_KG + N_HEADS * HEAD_DIM
MM_WIDTH = MM_QK + N_PAIRS * HEAD_DIM
VMEM_LIMIT = 56 * 1024 * 1024


def _cparams(*sem):
    return pltpu.CompilerParams(dimension_semantics=sem, vmem_limit_bytes=VMEM_LIMIT)


def _bdot(a, b):
    return jnp.dot(a.astype(BF16), b.astype(BF16), preferred_element_type=F32)


def _split_bf16(a):
    hi = a.astype(BF16)
    lo = (a - hi.astype(F32)).astype(BF16)
    return hi, lo


def _dot_x3(a, b):
    a_hi, a_lo = _split_bf16(a)
    b_hi, b_lo = _split_bf16(b)
    d = lambda x, y: jnp.dot(x, y, preferred_element_type=F32)
    return d(a_hi, b_hi) + (d(a_hi, b_lo) + d(a_lo, b_hi))


def _sigmoid(x):
    return 1.0 / (1.0 + jnp.exp(-x))


def _silu(x):
    return x * _sigmoid(x)


def _rms(x):
    return x * lax.rsqrt(jnp.mean(x * x, axis=-1, keepdims=True) + EPS)


def _mod_kernel(c_ref, w_ref, b_ref, o_ref):
    s = _silu(c_ref[...])
    o_ref[...] = _dot_x3(s, w_ref[...]) + b_ref[...]


def _modulation(cvec, w_ada, b_ada):
    rows, d = cvec.shape
    n = w_ada.shape[1]
    tn = 1536
    return pl.pallas_call(
        _mod_kernel,
        grid=(n // tn,),
        in_specs=[pl.BlockSpec((rows, d), lambda i: (0, 0)),
                  pl.BlockSpec((d, tn), lambda i: (0, i)),
                  pl.BlockSpec((1, tn), lambda i: (0, i))],
        out_specs=pl.BlockSpec((rows, tn), lambda i: (0, i)),
        out_shape=jax.ShapeDtypeStruct((rows, n), F32),
        compiler_params=_cparams("arbitrary"),
        name="modulation",
    )(cvec, w_ada, b_ada.reshape(1, n))


def _swiglu_residual(x, x_keep_ref, mod_ref, mod_row, g_ref, win_ref, wo_ref, h_ref, acc_ref):
    sh = mod_ref[0, mod_row:mod_row + 1, :]
    sc = mod_ref[0, mod_row + 1:mod_row + 2, :]
    gt = mod_ref[0, mod_row + 2:mod_row + 3, :]
    h_ref[...] = (_rms(x) * g_ref[...] * (1.0 + sc) + sh).astype(BF16)
    acc_ref[...] = jnp.zeros_like(acc_ref)
    ff = wo_ref.shape[0]
    for c in range(ff // FF_CHUNK):
        lo, hi = c * FF_CHUNK, (c + 1) * FF_CHUNK
        h = h_ref[...]
        a = jnp.dot(h, win_ref[:, lo:hi], preferred_element_type=F32)
        b = jnp.dot(h, win_ref[:, ff + lo:ff + hi], preferred_element_type=F32)
        act = (_silu(a) * b).astype(BF16)
        acc_ref[...] += jnp.dot(act, wo_ref[lo:hi, :], preferred_element_type=F32)
    return x_keep_ref[...] + 0.5 * gt * acc_ref[...]


def _ffn1_kernel(x_ref, mod_ref, g_ref, win_ref, wo_ref, o_ref, h_ref, acc_ref):
    o_ref[0] = _swiglu_residual(x_ref[0], x_ref.at[0], mod_ref, 0, g_ref, win_ref, wo_ref, h_ref, acc_ref)


def _mix_ffn2_kernel(x_ref, mod_ref, yc_ref, of_ref, ob_ref, z_ref, gn_ref, wt_ref, wbm_ref,
                     g_ref, win_ref, wo_ref, gfin_ref, o_ref, h_ref, acc_ref, x2_ref, *, final_norm):
    o = of_ref[0] + ob_ref[0]
    gate = _silu(z_ref[0])
    gn = gn_ref[...]
    ys = []
    for h in range(N_HEADS):
        sl = slice(h * HEAD_DIM, (h + 1) * HEAD_DIM)
        ys.append(_rms(o[:, sl]) * gn * gate[:, sl])
    y_dn = jnp.concatenate(ys, axis=1)
    m = _bdot(yc_ref[0], wt_ref[...]) + _bdot(y_dn, wbm_ref[...])
    x2_ref[...] = x_ref[0] + mod_ref[0, 5:6, :] * m
    y = _swiglu_residual(x2_ref[...], x2_ref, mod_ref, 6, g_ref, win_ref, wo_ref, h_ref, acc_ref)
    if final_norm:
        y = _rms(y) * gfin_ref[...]
    o_ref[0] = y


def _token_specs(d, tm, per_batch_mod):
    mod_map = (lambda b, i: (b, 0, 0)) if per_batch_mod else (lambda b, i: (0, 0, 0))
    return pl.BlockSpec((1, tm, d), lambda b, i: (b, i, 0)), pl.BlockSpec((1, N_MOD, d), mod_map)


def _ffn_weight_specs(win, wo):
    assert win.shape[1] == 2 * wo.shape[0] and wo.shape[0] % FF_CHUNK == 0
    const2 = lambda b, i: (0, 0)
    return [pl.BlockSpec(win.shape, const2, pipeline_mode=pl.Buffered(1)),
            pl.BlockSpec(wo.shape, const2, pipeline_mode=pl.Buffered(1))]


def _ffn1(x, mod, per_batch_mod, g, win, wo, tm):
    bsz, seq, d = x.shape
    const2 = lambda b, i: (0, 0)
    x_spec, mod_spec = _token_specs(d, tm, per_batch_mod)
    return pl.pallas_call(
        _ffn1_kernel,
        grid=(bsz, seq // tm),
        in_specs=[x_spec, mod_spec, pl.BlockSpec((1, d), const2)] + _ffn_weight_specs(win, wo),
        out_specs=x_spec,
        out_shape=jax.ShapeDtypeStruct(x.shape, F32),
        scratch_shapes=[pltpu.VMEM((tm, d), BF16), pltpu.VMEM((tm, d), F32)],
        compiler_params=_cparams("parallel", "arbitrary"),
        name="ffn1",
    )(x, mod, g, win, wo)


def _mix_ffn2(x, mod, per_batch_mod, yc, o_f, o_b, z, gn, wt, wbm, g, win, wo, gfin, final_norm, tm):
    bsz, seq, d = x.shape
    half = yc.shape[-1]
    const2 = lambda b, i: (0, 0)
    x_spec, mod_spec = _token_specs(d, tm, per_batch_mod)
    half_spec = pl.BlockSpec((1, tm, half), lambda b, i: (b, i, 0))
    return pl.pallas_call(
        functools.partial(_mix_ffn2_kernel, final_norm=final_norm),
        grid=(bsz, seq // tm),
        in_specs=[x_spec, mod_spec] + [half_spec] * 4
                 + [pl.BlockSpec((1, HEAD_DIM), const2),
                    pl.BlockSpec((half, d), const2, pipeline_mode=pl.Buffered(1)),
                    pl.BlockSpec((half, d), const2, pipeline_mode=pl.Buffered(1)),
                    pl.BlockSpec((1, d), const2)]
                 + _ffn_weight_specs(win, wo) + [pl.BlockSpec((1, d), const2)],
        out_specs=x_spec,
        out_shape=jax.ShapeDtypeStruct(x.shape, F32),
        scratch_shapes=[pltpu.VMEM((tm, d), BF16), pltpu.VMEM((tm, d), F32), pltpu.VMEM((tm, d), F32)],
        compiler_params=_cparams("parallel", "arbitrary"),
        name="mix_out_ffn2",
    )(x, mod, yc, o_f, o_b, z, gn, wt, wbm, g, win, wo, gfin)


def _proj_kernel(x_ref, mod_ref, g_ref, wc_ref, wq_ref, wz_ref, wab_ref, pc_ref, pq_ref, pz_ref, pab_ref):
    x = x_ref[0]
    sh = mod_ref[0, 3:4, :]
    sc = mod_ref[0, 4:5, :]
    h = (_rms(x) * g_ref[...] * (1.0 + sc) + sh).astype(BF16)
    pc_ref[0] = jnp.dot(h, wc_ref[...], preferred_element_type=F32)
    pq_ref[0] = jnp.dot(h, wq_ref[...], preferred_element_type=F32)
    pz_ref[0] = jnp.dot(h, wz_ref[...], preferred_element_type=F32)
    pab_ref[0] = jnp.dot(h, wab_ref[...], preferred_element_type=F32)


def _proj(x, mod, per_batch_mod, g, wc, wq, wz, wab, tm):
    bsz, seq, d = x.shape
    mod_map = (lambda b, i: (b, 0, 0)) if per_batch_mod else (lambda b, i: (0, 0, 0))
    const2 = lambda b, i: (0, 0)
    tok = lambda b, i: (b, i, 0)
    widths = (wc.shape[1], wq.shape[1], wz.shape[1], wab.shape[1])
    return pl.pallas_call(
        _proj_kernel,
        grid=(bsz, seq // tm),
        in_specs=[pl.BlockSpec((1, tm, d), tok),
                  pl.BlockSpec((1, N_MOD, d), mod_map),
                  pl.BlockSpec((1, d), const2)]
                 + [pl.BlockSpec((d, w), const2, pipeline_mode=pl.Buffered(1)) for w in widths],
        out_specs=[pl.BlockSpec((1, tm, w), tok) for w in widths],
        out_shape=[jax.ShapeDtypeStruct((bsz, seq, w), F32) for w in widths],
        compiler_params=_cparams("parallel", "arbitrary"),
        name="mix_in_proj",
    )(x, mod, g, wc, wq, wz, wab)


def _shift_rows(cur, prev_halo, next_halo, dist, first, last):
    tm = cur.shape[0]
    hp = jnp.where(first, 0.0, prev_halo)
    hn = jnp.where(last, 0.0, next_halo)
    if dist % 8 == 0:
        prev = jnp.concatenate([hp[hp.shape[0] - dist:], cur[:tm - dist]], axis=0)
        nxt = jnp.concatenate([cur[dist:], hn[:dist]], axis=0)
        return prev, nxt
    assert dist == 1
    row = lax.broadcasted_iota(jnp.int32, cur.shape, 0)
    prev = jnp.where(row == 0, hp[hp.shape[0] - 1:], pltpu.roll(cur, 1, axis=0))
    nxt = jnp.where(row == tm - 1, hn[:1], pltpu.roll(cur, tm - 1, axis=0))
    return prev, nxt


def _prep_kernel(pc_ref, pcp_ref, pcn_ref, pq_ref, pqp_ref, pqn_ref, pab_ref,
                 cw_ref, dw_ref, alog_ref, dtb_ref,
                 yc_ref, q_ref, k_ref, v_ref, gb_ref, *, conv_dist, n_tiles):
    i = pl.program_id(1)
    first = i == 0
    last = i == n_tiles - 1
    cwid = yc_ref.shape[-1]
    kwid = q_ref.shape[-1]

    pc = pc_ref[0]
    u = pc[:, cwid:2 * cwid] * pc[:, 2 * cwid:]
    pcp = pcp_ref[0]
    pcn = pcn_ref[0]
    up, un = _shift_rows(u, pcp[:, cwid:2 * cwid] * pcp[:, 2 * cwid:],
                         pcn[:, cwid:2 * cwid] * pcn[:, 2 * cwid:], conv_dist, first, last)
    cw = cw_ref[...]
    yc_ref[0] = pc[:, :cwid] * (up * cw[0:1] + u * cw[1:2] + un * cw[2:3])

    pq = pq_ref[0]
    qp, qn = _shift_rows(pq, pqp_ref[0], pqn_ref[0], 1, first, last)
    dw = dw_ref[...]
    qkv = _silu(qp * dw[0:1] + pq * dw[1:2] + qn * dw[2:3])
    for h in range(N_HEADS):
        sl = slice(h * HEAD_DIM, (h + 1) * HEAD_DIM)
        qh = qkv[:, sl]
        kh = qkv[:, kwid + h * HEAD_DIM: kwid + (h + 1) * HEAD_DIM]
        q_ref[0, :, sl] = qh * (lax.rsqrt(jnp.sum(qh * qh, axis=-1, keepdims=True) + EPS) * HEAD_DIM ** -0.5)
        k_ref[0, :, sl] = kh * lax.rsqrt(jnp.sum(kh * kh, axis=-1, keepdims=True) + EPS)
    v_ref[0] = qkv[:, 2 * kwid:]

    ab = pab_ref[0]
    lane = lax.broadcasted_iota(jnp.int32, ab.shape, 1)
    is_gf = lane < N_HEADS
    is_gb = (lane >= 2 * N_HEADS) & (lane < 3 * N_HEADS)
    xg = ab + dtb_ref[...]
    softplus = jnp.maximum(xg, 0.0) + jnp.log1p(jnp.exp(-jnp.abs(xg)))
    gates = jnp.where(is_gf | is_gb, -jnp.exp(alog_ref[...]) * softplus, _sigmoid(ab))
    r = lax.broadcasted_iota(jnp.int32, (CHUNK, CHUNK), 0)
    c = lax.broadcasted_iota(jnp.int32, (CHUNK, CHUNK), 1)
    tril = (r >= c).astype(BF16)
    triu = (r <= c).astype(BF16)
    g_hi = gates.astype(BF16)
    rem = gates - g_hi.astype(F32)
    g_mid = rem.astype(BF16)
    g_lo = (rem - g_mid.astype(F32)).astype(BF16)
    d = lambda m, x: jnp.dot(m, x, preferred_element_type=F32)
    lane_c = lax.broadcasted_iota(jnp.int32, (CHUNK, ab.shape[1]), 1)
    is_gf_c = lane_c < N_HEADS
    is_gb_c = (lane_c >= 2 * N_HEADS) & (lane_c < 3 * N_HEADS)
    for t in range(ab.shape[0] // CHUNK):
        rs = slice(t * CHUNK, (t + 1) * CHUNK)
        parts = (g_hi[rs], g_mid[rs], g_lo[rs])
        pre = d(tril, parts[0]) + (d(tril, parts[1]) + d(tril, parts[2]))
        suf = d(triu, parts[0]) + (d(triu, parts[1]) + d(triu, parts[2]))
        gb_ref[0, rs, :] = jnp.where(is_gf_c, pre, jnp.where(is_gb_c, suf, gates[rs]))


def _prep(pc, pq, pab, conv_w, dn_conv_w, alog_row, dtb_row, conv_dist, tm):
    bsz, seq, wc3 = pc.shape
    wq3 = pq.shape[-1]
    cwid, kwid = wc3 // 3, wq3 // 3
    n_tiles = seq // tm
    hc = max(conv_dist, 8)
    hq = 8
    tok = lambda b, i: (b, i, 0)
    const2 = lambda b, i: (0, 0)

    def prev_map(hrows):
        return lambda b, i: (b, jnp.maximum(i * (tm // hrows) - 1, 0), 0)

    def next_map(hrows):
        return lambda b, i: (b, jnp.minimum((i + 1) * (tm // hrows), seq // hrows - 1), 0)

    outs = [jax.ShapeDtypeStruct((bsz, seq, w), F32) for w in (cwid, kwid, kwid, kwid, pab.shape[-1])]
    return pl.pallas_call(
        functools.partial(_prep_kernel, conv_dist=conv_dist, n_tiles=n_tiles),
        grid=(bsz, n_tiles),
        in_specs=[pl.BlockSpec((1, tm, wc3), tok),
                  pl.BlockSpec((1, hc, wc3), prev_map(hc)),
                  pl.BlockSpec((1, hc, wc3), next_map(hc)),
                  pl.BlockSpec((1, tm, wq3), tok),
                  pl.BlockSpec((1, hq, wq3), prev_map(hq)),
                  pl.BlockSpec((1, hq, wq3), next_map(hq)),
                  pl.BlockSpec((1, tm, pab.shape[-1]), tok),
                  pl.BlockSpec(conv_w.shape, const2),
                  pl.BlockSpec(dn_conv_w.shape, const2),
                  pl.BlockSpec(alog_row.shape, const2),
                  pl.BlockSpec(dtb_row.shape, const2)],
        out_specs=[pl.BlockSpec((1, tm, o.shape[-1]), tok) for o in outs],
        out_shape=outs,
        compiler_params=_cparams("parallel", "arbitrary"),
        name="mixer_prep",
    )(pc, pc, pc, pq, pq, pq, pab, conv_w, dn_conv_w, alog_row, dtb_row)


def _pair_block_diag(y16, keep_left, keep_right):
    return jnp.concatenate([y16 * keep_left, y16 * keep_right], axis=0)


INV_BASE = 8


def _pair_matmul_x3(lhs_list, rhs, keep_left, keep_right):
    d = lambda x, y: jnp.dot(x, y, preferred_element_type=F32)
    r_hi, r_lo = _split_bf16(rhs)
    r_hi = _pair_block_diag(r_hi, keep_left, keep_right)
    r_lo = _pair_block_diag(r_lo, keep_left, keep_right)
    parts = [_split_bf16(x) for x in lhs_list]
    l1 = jnp.concatenate([jnp.concatenate([hi, lo], axis=1) for hi, lo in parts], axis=0)
    l2 = jnp.concatenate([hi for hi, _ in parts], axis=0)
    res = d(l1, jnp.concatenate([r_hi, r_hi], axis=0)) + d(l2, r_lo)
    return [res[i * CHUNK:(i + 1) * CHUNK] for i in range(len(lhs_list))]


def _pair_inverses(a_list, row, col, eye_pair, keep_left, keep_right):
    mm = lambda lhs_list, rhs: _pair_matmul_x3(lhs_list, rhs, keep_left, keep_right)
    shift = INV_BASE.bit_length() - 1
    same_base = (row >> shift) == (col >> shift)
    ps = [jnp.where(same_base, -a, 0.0) for a in a_list]
    ts = [eye_pair + p for p in ps]
    n_sq = shift - 1
    for i in range(n_sq + 1):
        new_ps, new_ts = [], []
        for p, t in zip(ps, ts):
            if i == 0:
                new_ps.append(mm([p], p)[0])
                new_ts.append(t)
            elif i == n_sq:
                new_ps.append(p)
                new_ts.append(t + mm([t], p)[0])
            else:
                p2, tp = mm([p, t], p)
                new_ps.append(p2)
                new_ts.append(t + tp)
        ps, ts = new_ps, new_ts
    size = INV_BASE
    while size < CHUNK:
        s1 = size.bit_length() - 1
        off = ((row >> (s1 + 1)) == (col >> (s1 + 1))) & ((row >> s1) != (col >> s1))
        es = [jnp.where(off, a, 0.0) for a in a_list]
        xs = [mm([e], t)[0] for e, t in zip(es, ts)]
        ts = [t - mm([t], x)[0] for t, x in zip(ts, xs)]
        size *= 2
    return ts


def _chunk_prep_kernel(q_ref, k_ref, v_ref, g_ref, uf_ref, ub_ref, mf_ref, mb_ref, glf_ref, glb_ref):
    u_refs, mm_refs, gl_refs = (uf_ref, ub_ref), (mf_ref, mb_ref), (glf_ref, glb_ref)
    row = lax.broadcasted_iota(jnp.int32, (CHUNK, 2 * CHUNK), 0)
    lane = lax.broadcasted_iota(jnp.int32, (CHUNK, 2 * CHUNK), 1)
    col = lane & (CHUNK - 1)
    left = lane < CHUNK
    keep_left = jnp.where(left, 1.0, 0.0).astype(BF16)
    keep_right = jnp.where(left, 0.0, 1.0).astype(BF16)
    eye_pair = jnp.where(row == col, 1.0, 0.0)
    zeros16 = jnp.zeros((CHUNK, HEAD_DIM), BF16)
    dot = lambda x, y: jnp.dot(x, y, preferred_element_type=F32)

    chains, a_list = [], []
    for c in range(PREP_CHUNKS):
        rs = slice(c * CHUNK, (c + 1) * CHUNK)
        gates = g_ref[0, rs, :]
        for p in range(N_PAIRS):
            sl = slice(2 * p * HEAD_DIM, 2 * (p + 1) * HEAD_DIM)
            q2, k2, v2 = q_ref[0, rs, sl], k_ref[0, rs, sl], v_ref[0, rs, sl]
            k16 = k2.astype(BF16)
            k_bd = jnp.concatenate([jnp.concatenate([k16[:, :HEAD_DIM], zeros16], axis=1),
                                    jnp.concatenate([zeros16, k16[:, HEAD_DIM:]], axis=1)], axis=0)
            kq = jnp.concatenate([k16, q2.astype(BF16)], axis=0)
            gram = lax.dot_general(kq, k_bd, (((1,), (1,)), ((), ())), preferred_element_type=F32)
            for d in range(N_DIRS):
                gcol, bcol = 2 * d * N_HEADS, (2 * d + 1) * N_HEADS
                gc = [gates[:, gcol + 2 * p + e:gcol + 2 * p + e + 1] for e in range(2)]
                beta = [gates[:, bcol + 2 * p + e:bcol + 2 * p + e + 1] for e in range(2)]
                gc_pair = jnp.where(left, gc[0], gc[1])
                beta_pair = jnp.where(left, beta[0], beta[1])
                gr_pair = jnp.sum(jnp.where(row == col, gc_pair, 0.0), axis=0, keepdims=True)
                if d == 0:
                    incl, strict, last = row >= col, row > col, CHUNK - 1
                else:
                    incl, strict, last = row <= col, row < col, 0
                decay = jnp.where(incl, jnp.exp(jnp.where(incl, gc_pair - gr_pair, 0.0)), 0.0)
                a_list.append(jnp.where(strict, beta_pair * gram[:CHUNK] * decay, 0.0))
                mm_refs[d][0, rs, MM_QK + p * HEAD_DIM:MM_QK + (p + 1) * HEAD_DIM] = (
                    jnp.where(incl, gram[CHUNK:] * decay, 0.0).astype(BF16))
                chains.append((c, rs, p, d, q2, k2, v2, gc, beta, last))

    t_list = _pair_inverses(a_list, row, col, eye_pair, keep_left, keep_right)

    for (c, rs, p, d, q2, k2, v2, gc, beta, last), t in zip(chains, t_list):
        t16 = t.astype(BF16)
        for e in range(2):
            h = 2 * p + e
            sl = slice(e * HEAD_DIM, (e + 1) * HEAD_DIM)
            hs = slice(h * HEAD_DIM, (h + 1) * HEAD_DIM)
            hcol = lambda base: slice(base + h * HEAD_DIM, base + (h + 1) * HEAD_DIM)
            eg = jnp.exp(gc[e])
            g_last = gc[e][last:last + 1, :]
            rhs = jnp.concatenate([v2[:, sl] * beta[e], k2[:, sl] * (beta[e] * eg)], axis=1).astype(BF16)
            zpad = jnp.zeros_like(rhs)
            uw = dot(t16, jnp.concatenate([rhs, zpad] if e == 0 else [zpad, rhs], axis=0))
            u_refs[d][0, rs, hs] = uw[:, :HEAD_DIM]
            mm_refs[d][0, rs, hcol(MM_W)] = uw[:, HEAD_DIM:].astype(BF16)
            mm_refs[d][0, rs, hcol(MM_QG)] = (q2[:, sl] * eg).astype(BF16)
            mm_refs[d][0, rs, hcol(MM_KG)] = (k2[:, sl] * jnp.exp(g_last - gc[e])).astype(BF16)
            gl_refs[d][0, c, h:h + 1, :] = jnp.broadcast_to(jnp.exp(g_last), (1, HEAD_DIM))


def _chunk_prep(q, k, v, gb):
    bsz, seq, kwid = q.shape
    rows = PREP_CHUNKS * CHUNK
    n = seq // CHUNK
    tok = lambda b, i: (b, i, 0)
    assert kwid == N_HEADS * HEAD_DIM
    u_shape = jax.ShapeDtypeStruct(q.shape, F32)
    mm_shape = jax.ShapeDtypeStruct((bsz, seq, MM_WIDTH), BF16)
    gl = jax.ShapeDtypeStruct((bsz, n, N_HEADS, HEAD_DIM), F32)
    outs = [u_shape] * 2 + [mm_shape] * 2 + [gl] * 2
    out_specs = ([pl.BlockSpec((1, rows, kwid), tok)] * 2 + [pl.BlockSpec((1, rows, MM_WIDTH), tok)] * 2
                 + [pl.BlockSpec((1, PREP_CHUNKS, N_HEADS, HEAD_DIM), lambda b, i: (b, i, 0, 0))] * 2)
    return pl.pallas_call(
        _chunk_prep_kernel,
        grid=(bsz, seq // rows),
        in_specs=[pl.BlockSpec((1, rows, kwid), tok)] * 3 + [pl.BlockSpec((1, rows, gb.shape[-1]), tok)],
        out_specs=out_specs,
        out_shape=outs,
        compiler_params=_cparams("parallel", "parallel"),
        name="delta_chunk_prep",
    )(q, k, v, gb)


def _scan_kernel(uf_ref, mf_ref, glf_ref, ub_ref, mb_ref, glb_ref, s0f_ref, s0b_ref,
                 of_ref, ob_ref, sff_ref, sfb_ref, s_ref, sbd_ref, vbd_ref, *, n_chunks, has_s0):
    j = pl.program_id(1)
    n_pd = N_DIRS * N_PAIRS
    s0_refs = (s0f_ref, s0b_ref)
    sfin_refs = (sff_ref, sfb_ref)
    chains = [(r, d, p) for r in range(SCAN_ROWS) for d in range(N_DIRS) for p in range(N_PAIRS)]
    sidx = lambda r, d, p: r * n_pd + d * N_PAIRS + p
    lo, hi = slice(0, HEAD_DIM), slice(HEAD_DIM, 2 * HEAD_DIM)

    def store_state(i, s):
        s_ref[i] = s
        sbd_ref[i, lo, lo] = s[:, lo].astype(BF16)
        sbd_ref[i, hi, hi] = s[:, hi].astype(BF16)

    @pl.when(j == 0)
    def _():
        sbd_ref[...] = jnp.zeros_like(sbd_ref)
        vbd_ref[...] = jnp.zeros_like(vbd_ref)
        for r, d, p in chains:
            if has_s0:
                s = jnp.concatenate([s0_refs[d][r, 2 * p], s0_refs[d][r, 2 * p + 1]], axis=1)
            else:
                s = jnp.zeros(s_ref.shape[1:], F32)
            store_state(sidx(r, d, p), s)

    dot = lambda x, y: jnp.dot(x, y, preferred_element_type=F32)
    ins = ((uf_ref, mf_ref, glf_ref), (ub_ref, mb_ref, glb_ref))
    outs = (of_ref, ob_ref)
    hsl = lambda base, h: slice(base + h * HEAD_DIM, base + (h + 1) * HEAD_DIM)
    psl = lambda base, p: slice(base + 2 * p * HEAD_DIM, base + 2 * (p + 1) * HEAD_DIM)

    ws_qs = []
    for r, d, p in chains:
        u_ref, m_ref, _ = ins[d]
        lhs = jnp.concatenate([m_ref[r, :, psl(MM_W, p)], m_ref[r, :, psl(MM_QG, p)]], axis=0)
        ws_qs.append(dot(lhs, sbd_ref[sidx(r, d, p)]))
    for i, (r, d, p) in enumerate(chains):
        u_ref, m_ref, _ = ins[d]
        vn = (u_ref[r, :, psl(0, p)] - ws_qs[i][:CHUNK]).astype(BF16)
        vbd_ref[i, 0:CHUNK, lo] = vn[:, lo]
        vbd_ref[i, CHUNK:, hi] = vn[:, hi]
        outs[d][r, :, psl(0, p)] = ws_qs[i][CHUNK:] + dot(m_ref[r, :, hsl(MM_QK, p)], vbd_ref[i])
    for i, (r, d, p) in enumerate(chains):
        _, m_ref, gl_ref = ins[d]
        kg = jnp.concatenate([m_ref[r, :, hsl(MM_KG, 2 * p)], m_ref[r, :, hsl(MM_KG, 2 * p + 1)]], axis=0)
        upd = lax.dot_general(kg, vbd_ref[i], (((0,), (0,)), ((), ())), preferred_element_type=F32)
        gl = jnp.concatenate([gl_ref[r, 0, 2 * p:2 * p + 1, :], gl_ref[r, 0, 2 * p + 1:2 * p + 2, :]], axis=1)
        store_state(i, s_ref[i] * gl + upd)

    @pl.when(j == n_chunks - 1)
    def _():
        for r, d, p in chains:
            s = s_ref[sidx(r, d, p)]
            sfin_refs[d][r, 2 * p] = s[:, lo]
            sfin_refs[d][r, 2 * p + 1] = s[:, hi]


def _scan(prep, s0_f, s0_b):
    u_f, u_b, m_f, m_b, gl_f, gl_b = prep
    bsz, seq, kwid = u_f.shape
    assert bsz % SCAN_ROWS == 0
    n = seq // CHUNK
    has_s0 = s0_f is not None
    sblk = (SCAN_ROWS, N_HEADS, HEAD_DIM, HEAD_DIM)
    if not has_s0:
        s0_f = s0_b = jnp.zeros(sblk, F32)
    fwd = lambda b, j: (b, j, 0)
    bwd = lambda b, j: (b, n - 1 - j, 0)
    fwd4 = lambda b, j: (b, j, 0, 0)
    bwd4 = lambda b, j: (b, n - 1 - j, 0, 0)
    smap = (lambda b, j: (b, 0, 0, 0)) if has_s0 else (lambda b, j: (0, 0, 0, 0))

    def dir_specs(tok, tok4):
        return [pl.BlockSpec((SCAN_ROWS, CHUNK, kwid), tok),
                pl.BlockSpec((SCAN_ROWS, CHUNK, MM_WIDTH), tok),
                pl.BlockSpec((SCAN_ROWS, 1, N_HEADS, HEAD_DIM), tok4)]

    n_chains = SCAN_ROWS * N_DIRS * N_PAIRS

    return pl.pallas_call(
        functools.partial(_scan_kernel, n_chunks=n, has_s0=has_s0),
        grid=(bsz // SCAN_ROWS, n),
        in_specs=dir_specs(fwd, fwd4) + dir_specs(bwd, bwd4) + [pl.BlockSpec(sblk, smap)] * 2,
        out_specs=[pl.BlockSpec((SCAN_ROWS, CHUNK, kwid), fwd),
                   pl.BlockSpec((SCAN_ROWS, CHUNK, kwid), bwd),
                   pl.BlockSpec(sblk, lambda b, j: (b, 0, 0, 0)),
                   pl.BlockSpec(sblk, lambda b, j: (b, 0, 0, 0))],
        out_shape=[jax.ShapeDtypeStruct(u_f.shape, F32)] * 2
                  + [jax.ShapeDtypeStruct((bsz,) + sblk[1:], F32)] * 2,
        scratch_shapes=[pltpu.VMEM((n_chains, HEAD_DIM, 2 * HEAD_DIM), F32),
                        pltpu.VMEM((n_chains, 2 * HEAD_DIM, 2 * HEAD_DIM), BF16),
                        pltpu.VMEM((n_chains, 2 * CHUNK, 2 * HEAD_DIM), BF16)],
        compiler_params=_cparams("parallel", "arbitrary"),
        name="delta_scan",
    )(u_f, m_f, gl_f, u_b, m_b, gl_b, s0_f, s0_b)


def _layer(x, mod, per_batch_mod, s0_f, s0_b, on_grid, lw, norm_final, final_norm, tm, fold):
    (g1, win1, wo1, gm, wc, wq, wz, wab, conv_w, dn_conv_w, alog_row, dtb_row, gn, wt, wbo,
     g2, win2, wo2) = lw
    bsz, seq, d = x.shape
    assert fold == 1 or not per_batch_mod
    folded = lambda a: a.reshape(bsz // fold, fold * seq, a.shape[-1])
    unfolded = lambda a: a.reshape(bsz, seq, a.shape[-1])
    x = _ffn1(folded(x), mod, per_batch_mod, g1, win1, wo1, tm)
    pc, pq, pz, pab = _proj(x, mod, per_batch_mod, gm, wc, wq, wz, wab, tm)
    yc, q, k, v, gb = _prep(unfolded(pc), unfolded(pq), unfolded(pab), conv_w, dn_conv_w, alog_row, dtb_row,
                            GRID_W if on_grid else 1, min(tm, seq))
    o_f, o_b, s_f, s_b = _scan(_chunk_prep(q, k, v, gb), s0_f, s0_b)
    y = _mix_ffn2(x, mod, per_batch_mod, folded(yc), folded(o_f), folded(o_b), pz, gn, wt, wbo,
                  g2, win2, wo2, norm_final, final_norm, tm)
    return unfolded(y), s_f, s_b


def kernel(x_prompt, x_sample, state_dn_fwd, state_dn_bwd, c, c_ctx, w_ada, b_ada, norm_ffn1, w_ffn1_in,
           w_ffn1_out, norm_mix, w_mix_in, conv_w, dn_conv_w, dn_a_log, dn_dt_bias, dn_norm, w_mix_out,
           norm_ffn2, w_ffn2_in, w_ffn2_out, norm_final):
    depth = w_ada.shape[0]
    d = x_prompt.shape[-1]
    n_lat = c.shape[0]
    cwid = conv_w.shape[-1]
    kwid = N_HEADS * HEAD_DIM
    n_gate = 4 * N_HEADS
    row = lambda a: a.reshape(1, -1)

    cvec = jnp.concatenate([c_ctx[None, :], c, jnp.zeros((16 - 1 - n_lat, d), F32)], axis=0)

    xp, xs = x_prompt, x_sample
    new_f, new_b = [], []
    for l in range(depth):
        mod = _modulation(cvec, w_ada[l], b_ada[l]).reshape(16, N_MOD, d)
        mod_ctx, mod_lat = mod[0:1], mod[1:1 + n_lat]

        win1, wo1 = w_ffn1_in[l].astype(BF16), w_ffn1_out[l].astype(BF16)
        win2, wo2 = w_ffn2_in[l].astype(BF16), w_ffn2_out[l].astype(BF16)
        wm = w_mix_in[l].astype(BF16)
        wc = wm[:, :3 * cwid]
        wq = wm[:, 3 * cwid:3 * cwid + 3 * kwid]
        wz = wm[:, 3 * cwid + 3 * kwid:3 * cwid + 4 * kwid]
        wab = jnp.pad(wm[:, 3 * cwid + 4 * kwid:], ((0, 0), (0, HEAD_DIM - n_gate)))
        zpad = jnp.zeros((N_HEADS,), F32)
        alog_row = jnp.concatenate([dn_a_log[l, 0], zpad, dn_a_log[l, 1], zpad,
                                    jnp.zeros((HEAD_DIM - n_gate,), F32)])[None, :]
        dtb_row = jnp.concatenate([dn_dt_bias[l, 0], zpad, dn_dt_bias[l, 1], zpad,
                                   jnp.zeros((HEAD_DIM - n_gate,), F32)])[None, :]
        wo = w_mix_out[l].astype(BF16)
        lw = (row(norm_ffn1[l]), win1, wo1, row(norm_mix[l]), wc, wq, wz, wab, conv_w[l], dn_conv_w[l],
              alog_row, dtb_row, row(dn_norm[l]), wo[:cwid], wo[cwid:], row(norm_ffn2[l]), win2, wo2)
        last = l == depth - 1
        xp, sf, sb = _layer(xp, mod_ctx, False, None, None, False, lw, row(norm_final), last,
                            TOKEN_TILE, TOKEN_TILE // xp.shape[1])
        new_f.append(sf)
        new_b.append(sb)
        xs, _, _ = _layer(xs, mod_lat, True, state_dn_fwd[:, l], state_dn_bwd[:, l], True, lw,
                          row(norm_final), last, TOKEN_TILE, 1)
    return xp, xs, jnp.stack(new_f, axis=1), jnp.stack(new_b, axis=1)
```

```python
import functools

import jax
import jax.numpy as jnp
from jax import lax
from jax.experimental import pallas as pl
from jax.experimental.pallas import tpu as pltpu

F32 = jnp.float32
BF16 = jnp.bfloat16

EPS = 1e-6
CHUNK = 64
GRID_W = 64
N_HEADS = 4
HEAD_DIM = 128
N_PAIRS = N_HEADS // 2
N_DIRS = 2
N_MOD = 9
SUBLANES = 8
FF_CHUNK = 256
TOKEN_TILE = 512
PREP_CHUNKS = 4
SCAN_ROWS = 4
VMEM_LIMIT = 56 * 1024 * 1024


def _cparams(*sem):
    return pltpu.CompilerParams(dimension_semantics=sem, vmem_limit_bytes=VMEM_LIMIT)


def _bdot(a, b):
    return jnp.dot(a.astype(BF16), b.astype(BF16), preferred_element_type=F32)


def _split_bf16(a):
    hi = a.astype(BF16)
    lo = (a - hi.astype(F32)).astype(BF16)
    return hi, lo


def _dot_x3(a, b):
    a_hi, a_lo = _split_bf16(a)
    b_hi, b_lo = _split_bf16(b)
    d = lambda x, y: jnp.dot(x, y, preferred_element_type=F32)
    return d(a_hi, b_hi) + (d(a_hi, b_lo) + d(a_lo, b_hi))


def _sigmoid(x):
    return 1.0 / (1.0 + jnp.exp(-x))


def _silu(x):
    return x * _sigmoid(x)


def _rms(x):
    return x * lax.rsqrt(jnp.mean(x * x, axis=-1, keepdims=True) + EPS)


def _mod_kernel(c_ref, w_ref, b_ref, o_ref):
    s = _silu(c_ref[...])
    o_ref[...] = _dot_x3(s, w_ref[...]) + b_ref[...]


def _modulation(cvec, w_ada, b_ada):
    rows, d = cvec.shape
    n = w_ada.shape[1]
    tn = 1536
    return pl.pallas_call(
        _mod_kernel,
        grid=(n // tn,),
        in_specs=[pl.BlockSpec((rows, d), lambda i: (0, 0)),
                  pl.BlockSpec((d, tn), lambda i: (0, i)),
                  pl.BlockSpec((1, tn), lambda i: (0, i))],
        out_specs=pl.BlockSpec((rows, tn), lambda i: (0, i)),
        out_shape=jax.ShapeDtypeStruct((rows, n), F32),
        compiler_params=_cparams("arbitrary"),
        name="modulation",
    )(cvec, w_ada, b_ada.reshape(1, n))


def _swiglu_residual(x, x_keep_ref, mod_ref, mod_row, g_ref, win_ref, wo_ref, h_ref, acc_ref):
    sh = mod_ref[0, mod_row:mod_row + 1, :]
    sc = mod_ref[0, mod_row + 1:mod_row + 2, :]
    gt = mod_ref[0, mod_row + 2:mod_row + 3, :]
    h_ref[...] = (_rms(x) * g_ref[...] * (1.0 + sc) + sh).astype(BF16)
    acc_ref[...] = jnp.zeros_like(acc_ref)
    ff = wo_ref.shape[0]
    for c in range(ff // FF_CHUNK):
        lo, hi = c * FF_CHUNK, (c + 1) * FF_CHUNK
        h = h_ref[...]
        a = jnp.dot(h, win_ref[:, lo:hi], preferred_element_type=F32)
        b = jnp.dot(h, win_ref[:, ff + lo:ff + hi], preferred_element_type=F32)
        act = (_silu(a) * b).astype(BF16)
        acc_ref[...] += jnp.dot(act, wo_ref[lo:hi, :], preferred_element_type=F32)
    return x_keep_ref[...] + 0.5 * gt * acc_ref[...]


def _ffn1_kernel(x_ref, mod_ref, g_ref, win_ref, wo_ref, o_ref, h_ref, acc_ref):
    o_ref[0] = _swiglu_residual(x_ref[0], x_ref.at[0], mod_ref, 0, g_ref, win_ref, wo_ref, h_ref, acc_ref)


def _mix_ffn2_kernel(x_ref, mod_ref, yc_ref, of_ref, ob_ref, z_ref, gn_ref, wt_ref, wbm_ref,
                     g_ref, win_ref, wo_ref, gfin_ref, o_ref, h_ref, acc_ref, x2_ref, *, final_norm):
    o = of_ref[0] + ob_ref[0]
    gate = _silu(z_ref[0])
    gn = gn_ref[...]
    ys = []
    for h in range(N_HEADS):
        sl = slice(h * HEAD_DIM, (h + 1) * HEAD_DIM)
        ys.append(_rms(o[:, sl]) * gn * gate[:, sl])
    y_dn = jnp.concatenate(ys, axis=1)
    m = _bdot(yc_ref[0], wt_ref[...]) + _bdot(y_dn, wbm_ref[...])
    x2_ref[...] = x_ref[0] + mod_ref[0, 5:6, :] * m
    y = _swiglu_residual(x2_ref[...], x2_ref, mod_ref, 6, g_ref, win_ref, wo_ref, h_ref, acc_ref)
    if final_norm:
        y = _rms(y) * gfin_ref[...]
    o_ref[0] = y


def _token_specs(d, tm, per_batch_mod):
    mod_map = (lambda b, i: (b, 0, 0)) if per_batch_mod else (lambda b, i: (0, 0, 0))
    return pl.BlockSpec((1, tm, d), lambda b, i: (b, i, 0)), pl.BlockSpec((1, N_MOD, d), mod_map)


def _ffn_weight_specs(win, wo):
    assert win.shape[1] == 2 * wo.shape[0] and wo.shape[0] % FF_CHUNK == 0
    const2 = lambda b, i: (0, 0)
    return [pl.BlockSpec(win.shape, const2, pipeline_mode=pl.Buffered(1)),
            pl.BlockSpec(wo.shape, const2, pipeline_mode=pl.Buffered(1))]


def _ffn1(x, mod, per_batch_mod, g, win, wo, tm):
    bsz, seq, d = x.shape
    const2 = lambda b, i: (0, 0)
    x_spec, mod_spec = _token_specs(d, tm, per_batch_mod)
    return pl.pallas_call(
        _ffn1_kernel,
        grid=(bsz, seq // tm),
        in_specs=[x_spec, mod_spec, pl.BlockSpec((1, d), const2)] + _ffn_weight_specs(win, wo),
        out_specs=x_spec,
        out_shape=jax.ShapeDtypeStruct(x.shape, F32),
        scratch_shapes=[pltpu.VMEM((tm, d), BF16), pltpu.VMEM((tm, d), F32)],
        compiler_params=_cparams("parallel", "arbitrary"),
        name="ffn1",
    )(x, mod, g, win, wo)


def _mix_ffn2(x, mod, per_batch_mod, yc, o_f, o_b, z, gn, wt, wbm, g, win, wo, gfin, final_norm, tm):
    bsz, seq, d = x.shape
    half = yc.shape[-1]
    const2 = lambda b, i: (0, 0)
    x_spec, mod_spec = _token_specs(d, tm, per_batch_mod)
    half_spec = pl.BlockSpec((1, tm, half), lambda b, i: (b, i, 0))
    return pl.pallas_call(
        functools.partial(_mix_ffn2_kernel, final_norm=final_norm),
        grid=(bsz, seq // tm),
        in_specs=[x_spec, mod_spec] + [half_spec] * 4
                 + [pl.BlockSpec((1, HEAD_DIM), const2),
                    pl.BlockSpec((half, d), const2, pipeline_mode=pl.Buffered(1)),
                    pl.BlockSpec((half, d), const2, pipeline_mode=pl.Buffered(1)),
                    pl.BlockSpec((1, d), const2)]
                 + _ffn_weight_specs(win, wo) + [pl.BlockSpec((1, d), const2)],
        out_specs=x_spec,
        out_shape=jax.ShapeDtypeStruct(x.shape, F32),
        scratch_shapes=[pltpu.VMEM((tm, d), BF16), pltpu.VMEM((tm, d), F32), pltpu.VMEM((tm, d), F32)],
        compiler_params=_cparams("parallel", "arbitrary"),
        name="mix_out_ffn2",
    )(x, mod, yc, o_f, o_b, z, gn, wt, wbm, g, win, wo, gfin)


def _neighbour_rows(ext, halo, tm, dist, first, last):
    prev = ext[halo - dist:halo - dist + tm]
    nxt = ext[halo + dist:halo + dist + tm]
    row = lax.broadcasted_iota(jnp.int32, prev.shape, 0)
    prev = jnp.where(first & (row < dist), 0.0, prev)
    nxt = jnp.where(last & (row >= tm - dist), 0.0, nxt)
    return prev, nxt


def _proj_prep_kernel(x_ref, xp_ref, xn_ref, mod_ref, g_ref, wc_ref, wq_ref, wz_ref, wab_ref,
                      cw_ref, dw_ref, alog_ref, dtb_ref,
                      yc_ref, q_ref, k_ref, v_ref, gb_ref, pz_ref, *, conv_dist, n_tiles):
    i = pl.program_id(1)
    first = i == 0
    last = i == n_tiles - 1
    tm = x_ref.shape[1]
    halo = xp_ref.shape[1]
    cwid = yc_ref.shape[-1]
    kwid = q_ref.shape[-1]
    sh = mod_ref[0, 3:4, :]
    sc = mod_ref[0, 4:5, :]
    norm = lambda xv: (_rms(xv) * g_ref[...] * (1.0 + sc) + sh).astype(BF16)
    h = norm(x_ref[0])
    h_ext = jnp.concatenate([norm(xp_ref[0]), h, norm(xn_ref[0])], axis=0)
    dot = lambda a, b: jnp.dot(a, b, preferred_element_type=F32)

    qh = SUBLANES
    pq_ext = dot(h_ext[halo - qh:halo + tm + qh], wq_ref[...])
    pch_ext = dot(h_ext, wc_ref[:, cwid:])
    pq = pq_ext[qh:qh + tm]
    qp, qn = _neighbour_rows(pq_ext, qh, tm, 1, first, last)
    dw = dw_ref[...]
    qkv = _silu(qp * dw[0:1] + pq * dw[1:2] + qn * dw[2:3])
    for hd in range(N_HEADS):
        sl = slice(hd * HEAD_DIM, (hd + 1) * HEAD_DIM)
        qv = qkv[:, sl]
        kv = qkv[:, kwid + hd * HEAD_DIM: kwid + (hd + 1) * HEAD_DIM]
        q_ref[0, :, sl] = qv * (lax.rsqrt(jnp.sum(qv * qv, axis=-1, keepdims=True) + EPS) * HEAD_DIM ** -0.5)
        k_ref[0, :, sl] = kv * lax.rsqrt(jnp.sum(kv * kv, axis=-1, keepdims=True) + EPS)
    v_ref[0] = qkv[:, 2 * kwid:]

    pcb = dot(h, wc_ref[:, :cwid])
    pz_ref[0] = dot(h, wz_ref[...])
    ab = dot(h, wab_ref[...])
    u_ext = pch_ext[:, :cwid] * pch_ext[:, cwid:]
    u = u_ext[halo:halo + tm]
    up, un = _neighbour_rows(u_ext, halo, tm, conv_dist, first, last)
    cw = cw_ref[...]
    yc_ref[0] = pcb * (up * cw[0:1] + u * cw[1:2] + un * cw[2:3])

    lane = lax.broadcasted_iota(jnp.int32, ab.shape, 1)
    is_g = (lane < N_HEADS) | ((lane >= 2 * N_HEADS) & (lane < 3 * N_HEADS))
    xg = ab + dtb_ref[...]
    softplus = jnp.maximum(xg, 0.0) + jnp.log1p(jnp.exp(-jnp.abs(xg)))
    gates = jnp.where(is_g, -jnp.exp(alog_ref[...]) * softplus, _sigmoid(ab))
    r = lax.broadcasted_iota(jnp.int32, (CHUNK, CHUNK), 0)
    c = lax.broadcasted_iota(jnp.int32, (CHUNK, CHUNK), 1)
    tril = (r >= c).astype(BF16)
    triu = (r <= c).astype(BF16)
    g_hi = gates.astype(BF16)
    rem = gates - g_hi.astype(F32)
    g_mid = rem.astype(BF16)
    g_lo = (rem - g_mid.astype(F32)).astype(BF16)
    lane_c = lax.broadcasted_iota(jnp.int32, (CHUNK, ab.shape[1]), 1)
    is_gf_c = lane_c < N_HEADS
    is_gb_c = (lane_c >= 2 * N_HEADS) & (lane_c < 3 * N_HEADS)
    for t in range(tm // CHUNK):
        rs = slice(t * CHUNK, (t + 1) * CHUNK)
        parts = (g_hi[rs], g_mid[rs], g_lo[rs])
        pre = dot(tril, parts[0]) + (dot(tril, parts[1]) + dot(tril, parts[2]))
        suf = dot(triu, parts[0]) + (dot(triu, parts[1]) + dot(triu, parts[2]))
        gb_ref[0, rs, :] = jnp.where(is_gf_c, pre, jnp.where(is_gb_c, suf, gates[rs]))


def _proj_prep(x, mod, per_batch_mod, g, wc, wq, wz, wab, conv_w, dn_conv_w, alog_row, dtb_row, conv_dist, tm):
    bsz, seq, d = x.shape
    cwid, kwid = wc.shape[1] // 3, wq.shape[1] // 3
    n_tiles = seq // tm
    halo = max(conv_dist, SUBLANES)
    assert tm % halo == 0 and halo % SUBLANES == 0
    const2 = lambda b, i: (0, 0)
    tok = lambda b, i: (b, i, 0)
    x_spec, mod_spec = _token_specs(d, tm, per_batch_mod)
    prev_map = lambda b, i: (b, jnp.maximum(i * (tm // halo) - 1, 0), 0)
    next_map = lambda b, i: (b, jnp.minimum((i + 1) * (tm // halo), seq // halo - 1), 0)
    widths = (cwid, kwid, kwid, kwid, wab.shape[1], wz.shape[1])
    return pl.pallas_call(
        functools.partial(_proj_prep_kernel, conv_dist=conv_dist, n_tiles=n_tiles),
        grid=(bsz, n_tiles),
        in_specs=[x_spec, pl.BlockSpec((1, halo, d), prev_map), pl.BlockSpec((1, halo, d), next_map),
                  mod_spec, pl.BlockSpec((1, d), const2)]
                 + [pl.BlockSpec(w.shape, const2, pipeline_mode=pl.Buffered(1)) for w in (wc, wq, wz, wab)]
                 + [pl.BlockSpec(a.shape, const2) for a in (conv_w, dn_conv_w, alog_row, dtb_row)],
        out_specs=[pl.BlockSpec((1, tm, w), tok) for w in widths],
        out_shape=[jax.ShapeDtypeStruct((bsz, seq, w), F32) for w in widths],
        compiler_params=_cparams("parallel", "arbitrary"),
        name="mix_in_proj_prep",
    )(x, x, x, mod, g, wc, wq, wz, wab, conv_w, dn_conv_w, alog_row, dtb_row)


def _pair_block_diag(y16, keep_left, keep_right):
    return jnp.concatenate([y16 * keep_left, y16 * keep_right], axis=0)


INV_BASE = 8


def _pair_matmul_x3(lhs_list, rhs, keep_left, keep_right):
    d = lambda x, y: jnp.dot(x, y, preferred_element_type=F32)
    r_hi, r_lo = _split_bf16(rhs)
    r_hi = _pair_block_diag(r_hi, keep_left, keep_right)
    r_lo = _pair_block_diag(r_lo, keep_left, keep_right)
    parts = [_split_bf16(x) for x in lhs_list]
    l1 = jnp.concatenate([jnp.concatenate([hi, lo], axis=1) for hi, lo in parts], axis=0)
    l2 = jnp.concatenate([hi for hi, _ in parts], axis=0)
    res = d(l1, jnp.concatenate([r_hi, r_hi], axis=0)) + d(l2, r_lo)
    return [res[i * CHUNK:(i + 1) * CHUNK] for i in range(len(lhs_list))]


def _pair_inverses(a_list, row, col, eye_pair, keep_left, keep_right):
    mm = lambda lhs_list, rhs: _pair_matmul_x3(lhs_list, rhs, keep_left, keep_right)
    shift = INV_BASE.bit_length() - 1
    same_base = (row >> shift) == (col >> shift)
    ps = [jnp.where(same_base, -a, 0.0) for a in a_list]
    ts = [eye_pair + p for p in ps]
    n_sq = shift - 1
    for i in range(n_sq + 1):
        new_ps, new_ts = [], []
        for p, t in zip(ps, ts):
            if i == 0:
                new_ps.append(mm([p], p)[0])
                new_ts.append(t)
            elif i == n_sq:
                new_ps.append(p)
                new_ts.append(t + mm([t], p)[0])
            else:
                p2, tp = mm([p, t], p)
                new_ps.append(p2)
                new_ts.append(t + tp)
        ps, ts = new_ps, new_ts
    size = INV_BASE
    while size < CHUNK:
        s1 = size.bit_length() - 1
        off = ((row >> (s1 + 1)) == (col >> (s1 + 1))) & ((row >> s1) != (col >> s1))
        es = [jnp.where(off, a, 0.0) for a in a_list]
        xs = [mm([e], t)[0] for e, t in zip(es, ts)]
        ts = [t - mm([t], x)[0] for t, x in zip(ts, xs)]
        size *= 2
    return ts


def _chunk_prep_kernel(q_ref, k_ref, v_ref, g_ref,
                       uf_ref, ub_ref, wf_ref, wb_ref, qgf_ref, qgb_ref, kgf_ref, kgb_ref,
                       qkf_ref, qkb_ref, glf_ref, glb_ref):
    u_refs, w_refs, qg_refs = (uf_ref, ub_ref), (wf_ref, wb_ref), (qgf_ref, qgb_ref)
    kg_refs, qk_refs, gl_refs = (kgf_ref, kgb_ref), (qkf_ref, qkb_ref), (glf_ref, glb_ref)
    row = lax.broadcasted_iota(jnp.int32, (CHUNK, 2 * CHUNK), 0)
    lane = lax.broadcasted_iota(jnp.int32, (CHUNK, 2 * CHUNK), 1)
    col = lane & (CHUNK - 1)
    left = lane < CHUNK
    keep_left = jnp.where(left, 1.0, 0.0).astype(BF16)
    keep_right = jnp.where(left, 0.0, 1.0).astype(BF16)
    eye_pair = jnp.where(row == col, 1.0, 0.0)
    zeros16 = jnp.zeros((CHUNK, HEAD_DIM), BF16)
    dot = lambda x, y: jnp.dot(x, y, preferred_element_type=F32)

    chains, a_list = [], []
    for c in range(PREP_CHUNKS):
        rs = slice(c * CHUNK, (c + 1) * CHUNK)
        gates = g_ref[0, rs, :]
        for p in range(N_PAIRS):
            sl = slice(2 * p * HEAD_DIM, 2 * (p + 1) * HEAD_DIM)
            q2, k2, v2 = q_ref[0, rs, sl], k_ref[0, rs, sl], v_ref[0, rs, sl]
            k16 = k2.astype(BF16)
            k_bd = jnp.concatenate([jnp.concatenate([k16[:, :HEAD_DIM], zeros16], axis=1),
                                    jnp.concatenate([zeros16, k16[:, HEAD_DIM:]], axis=1)], axis=0)
            kq = jnp.concatenate([k16, q2.astype(BF16)], axis=0)
            gram = lax.dot_general(kq, k_bd, (((1,), (1,)), ((), ())), preferred_element_type=F32)
            for d in range(N_DIRS):
                gcol, bcol = 2 * d * N_HEADS, (2 * d + 1) * N_HEADS
                gc = [gates[:, gcol + 2 * p + e:gcol + 2 * p + e + 1] for e in range(2)]
                beta = [gates[:, bcol + 2 * p + e:bcol + 2 * p + e + 1] for e in range(2)]
                gc_pair = jnp.where(left, gc[0], gc[1])
                beta_pair = jnp.where(left, beta[0], beta[1])
                gr_pair = jnp.sum(jnp.where(row == col, gc_pair, 0.0), axis=0, keepdims=True)
                if d == 0:
                    incl, strict, last = row >= col, row > col, CHUNK - 1
                else:
                    incl, strict, last = row <= col, row < col, 0
                decay = jnp.where(incl, jnp.exp(jnp.where(incl, gc_pair - gr_pair, 0.0)), 0.0)
                a_list.append(jnp.where(strict, beta_pair * gram[:CHUNK] * decay, 0.0))
                qk_refs[d][0, rs, p * HEAD_DIM:(p + 1) * HEAD_DIM] = (
                    jnp.where(incl, gram[CHUNK:] * decay, 0.0).astype(BF16))
                chains.append((c, rs, p, d, q2, k2, v2, gc, beta, last))

    t_list = _pair_inverses(a_list, row, col, eye_pair, keep_left, keep_right)

    for (c, rs, p, d, q2, k2, v2, gc, beta, last), t in zip(chains, t_list):
        t16 = t.astype(BF16)
        for e in range(2):
            h = 2 * p + e
            sl = slice(e * HEAD_DIM, (e + 1) * HEAD_DIM)
            hs = slice(h * HEAD_DIM, (h + 1) * HEAD_DIM)
            eg = jnp.exp(gc[e])
            g_last = gc[e][last:last + 1, :]
            rhs = jnp.concatenate([v2[:, sl] * beta[e], k2[:, sl] * (beta[e] * eg)], axis=1).astype(BF16)
            zpad = jnp.zeros_like(rhs)
            uw = dot(t16, jnp.concatenate([rhs, zpad] if e == 0 else [zpad, rhs], axis=0))
            u_refs[d][0, rs, hs] = uw[:, :HEAD_DIM]
            w_refs[d][0, rs, hs] = uw[:, HEAD_DIM:].astype(BF16)
            qg_refs[d][0, rs, hs] = (q2[:, sl] * eg).astype(BF16)
            kg_refs[d][0, rs, hs] = (k2[:, sl] * jnp.exp(g_last - gc[e])).astype(BF16)
            gl_refs[d][0, c, h:h + 1, :] = jnp.broadcast_to(jnp.exp(g_last), (1, HEAD_DIM))


def _chunk_prep(q, k, v, gb):
    bsz, seq, kwid = q.shape
    rows = PREP_CHUNKS * CHUNK
    n = seq // CHUNK
    tok = lambda b, i: (b, i, 0)
    f32_full = jax.ShapeDtypeStruct(q.shape, F32)
    bf_full = jax.ShapeDtypeStruct(q.shape, BF16)
    bf_half = jax.ShapeDtypeStruct((bsz, seq, kwid // 2), BF16)
    gl = jax.ShapeDtypeStruct((bsz, n, N_HEADS, HEAD_DIM), F32)
    outs = [f32_full] * 2 + [bf_full] * 6 + [bf_half] * 2 + [gl] * 2
    out_specs = ([pl.BlockSpec((1, rows, kwid), tok)] * 8 + [pl.BlockSpec((1, rows, kwid // 2), tok)] * 2
                 + [pl.BlockSpec((1, PREP_CHUNKS, N_HEADS, HEAD_DIM), lambda b, i: (b, i, 0, 0))] * 2)
    return pl.pallas_call(
        _chunk_prep_kernel,
        grid=(bsz, seq // rows),
        in_specs=[pl.BlockSpec((1, rows, kwid), tok)] * 3 + [pl.BlockSpec((1, rows, gb.shape[-1]), tok)],
        out_specs=out_specs,
        out_shape=outs,
        compiler_params=_cparams("parallel", "parallel"),
        name="delta_chunk_prep",
    )(q, k, v, gb)


def _scan_kernel(uf_ref, wf_ref, qgf_ref, kgf_ref, qkf_ref, glf_ref,
                 ub_ref, wb_ref, qgb_ref, kgb_ref, qkb_ref, glb_ref, s0f_ref, s0b_ref,
                 of_ref, ob_ref, sff_ref, sfb_ref, s_ref, *, n_chunks, has_s0):
    j = pl.program_id(1)
    n_hd = N_DIRS * N_HEADS

    @pl.when(j == 0)
    def _():
        for r in range(SCAN_ROWS):
            if has_s0:
                s_ref[r * n_hd:r * n_hd + N_HEADS] = s0f_ref[r]
                s_ref[r * n_hd + N_HEADS:(r + 1) * n_hd] = s0b_ref[r]
            else:
                s_ref[r * n_hd:(r + 1) * n_hd] = jnp.zeros((n_hd, HEAD_DIM, HEAD_DIM), F32)

    dot = lambda x, y: jnp.dot(x, y, preferred_element_type=F32)
    zeros16 = jnp.zeros((CHUNK, HEAD_DIM), BF16)
    ins = ((uf_ref, wf_ref, qgf_ref, kgf_ref, qkf_ref, glf_ref), (ub_ref, wb_ref, qgb_ref, kgb_ref, qkb_ref, glb_ref))
    outs = (of_ref, ob_ref)
    hds = [(r, d, h) for r in range(SCAN_ROWS) for d in range(N_DIRS) for h in range(N_HEADS)]
    hsl = lambda h: slice(h * HEAD_DIM, (h + 1) * HEAD_DIM)
    sidx = lambda r, d, h: r * n_hd + d * N_HEADS + h

    s_old, ws_qs = [], []
    for r, d, h in hds:
        s = s_ref[sidx(r, d, h)]
        s_old.append(s)
        lhs = jnp.concatenate([ins[d][1][r, :, hsl(h)], ins[d][2][r, :, hsl(h)]], axis=0)
        ws_qs.append(dot(lhs, s.astype(BF16)))
    v_new = []
    for i, (r, d, h) in enumerate(hds):
        vn = (ins[d][0][r, :, hsl(h)] - ws_qs[i][:CHUNK]).astype(BF16)
        v_new.append(vn)
        v_pad = jnp.concatenate([vn, zeros16] if h % 2 == 0 else [zeros16, vn], axis=0)
        qk_pair = ins[d][4][r, :, hsl(h // 2)]
        outs[d][r, :, hsl(h)] = ws_qs[i][CHUNK:] + dot(qk_pair, v_pad)
    for i, (r, d, h) in enumerate(hds):
        kg = ins[d][3][r, :, hsl(h)]
        upd = lax.dot_general(kg, v_new[i], (((0,), (0,)), ((), ())), preferred_element_type=F32)
        s_ref[sidx(r, d, h)] = s_old[i] * ins[d][5][r, 0, h:h + 1, :] + upd

    @pl.when(j == n_chunks - 1)
    def _():
        for r in range(SCAN_ROWS):
            sff_ref[r] = s_ref[r * n_hd:r * n_hd + N_HEADS]
            sfb_ref[r] = s_ref[r * n_hd + N_HEADS:(r + 1) * n_hd]


def _scan(prep, s0_f, s0_b):
    (u_f, u_b, w_f, w_b, qg_f, qg_b, kg_f, kg_b, qk_f, qk_b, gl_f, gl_b) = prep
    bsz, seq, kwid = u_f.shape
    assert bsz % SCAN_ROWS == 0
    n = seq // CHUNK
    has_s0 = s0_f is not None
    sblk = (SCAN_ROWS, N_HEADS, HEAD_DIM, HEAD_DIM)
    if not has_s0:
        s0_f = s0_b = jnp.zeros(sblk, F32)
    fwd = lambda b, j: (b, j, 0)
    bwd = lambda b, j: (b, n - 1 - j, 0)
    fwd4 = lambda b, j: (b, j, 0, 0)
    bwd4 = lambda b, j: (b, n - 1 - j, 0, 0)
    smap = (lambda b, j: (b, 0, 0, 0)) if has_s0 else (lambda b, j: (0, 0, 0, 0))

    def dir_specs(tok, tok4):
        return ([pl.BlockSpec((SCAN_ROWS, CHUNK, kwid), tok)] * 4
                + [pl.BlockSpec((SCAN_ROWS, CHUNK, kwid // 2), tok),
                   pl.BlockSpec((SCAN_ROWS, 1, N_HEADS, HEAD_DIM), tok4)])

    return pl.pallas_call(
        functools.partial(_scan_kernel, n_chunks=n, has_s0=has_s0),
        grid=(bsz // SCAN_ROWS, n),
        in_specs=dir_specs(fwd, fwd4) + dir_specs(bwd, bwd4) + [pl.BlockSpec(sblk, smap)] * 2,
        out_specs=[pl.BlockSpec((SCAN_ROWS, CHUNK, kwid), fwd),
                   pl.BlockSpec((SCAN_ROWS, CHUNK, kwid), bwd),
                   pl.BlockSpec(sblk, lambda b, j: (b, 0, 0, 0)),
                   pl.BlockSpec(sblk, lambda b, j: (b, 0, 0, 0))],
        out_shape=[jax.ShapeDtypeStruct(u_f.shape, F32)] * 2
                  + [jax.ShapeDtypeStruct((bsz,) + sblk[1:], F32)] * 2,
        scratch_shapes=[pltpu.VMEM((SCAN_ROWS * N_DIRS * N_HEADS, HEAD_DIM, HEAD_DIM), F32)],
        compiler_params=_cparams("parallel", "arbitrary"),
        name="delta_scan",
    )(u_f, w_f, qg_f, kg_f, qk_f, gl_f, u_b, w_b, qg_b, kg_b, qk_b, gl_b, s0_f, s0_b)


def _layer(x, mod, per_batch_mod, s0_f, s0_b, on_grid, lw, norm_final, final_norm, tm, fold):
    (g1, win1, wo1, gm, wc, wq, wz, wab, conv_w, dn_conv_w, alog_row, dtb_row, gn, wt, wbo,
     g2, win2, wo2) = lw
    bsz, seq, d = x.shape
    assert fold == 1 or not per_batch_mod
    folded = lambda a: a.reshape(bsz // fold, fold * seq, a.shape[-1])
    unfolded = lambda a: a.reshape(bsz, seq, a.shape[-1])
    x = _ffn1(folded(x), mod, per_batch_mod, g1, win1, wo1, tm)
    yc, q, k, v, gb, pz = _proj_prep(unfolded(x), mod, per_batch_mod, gm, wc, wq, wz, wab, conv_w, dn_conv_w,
                                     alog_row, dtb_row, GRID_W if on_grid else 1, min(tm, seq))
    o_f, o_b, s_f, s_b = _scan(_chunk_prep(q, k, v, gb), s0_f, s0_b)
    y = _mix_ffn2(x, mod, per_batch_mod, folded(yc), folded(o_f), folded(o_b), folded(pz), gn, wt, wbo,
                  g2, win2, wo2, norm_final, final_norm, tm)
    return unfolded(y), s_f, s_b


def kernel(x_prompt, x_sample, state_dn_fwd, state_dn_bwd, c, c_ctx, w_ada, b_ada, norm_ffn1, w_ffn1_in,
           w_ffn1_out, norm_mix, w_mix_in, conv_w, dn_conv_w, dn_a_log, dn_dt_bias, dn_norm, w_mix_out,
           norm_ffn2, w_ffn2_in, w_ffn2_out, norm_final):
    depth = w_ada.shape[0]
    d = x_prompt.shape[-1]
    n_lat = c.shape[0]
    cwid = conv_w.shape[-1]
    kwid = N_HEADS * HEAD_DIM
    n_gate = 4 * N_HEADS
    row = lambda a: a.reshape(1, -1)

    cvec = jnp.concatenate([c_ctx[None, :], c, jnp.zeros((16 - 1 - n_lat, d), F32)], axis=0)

    xp, xs = x_prompt, x_sample
    new_f, new_b = [], []
    for l in range(depth):
        mod = _modulation(cvec, w_ada[l], b_ada[l]).reshape(16, N_MOD, d)
        mod_ctx, mod_lat = mod[0:1], mod[1:1 + n_lat]

        win1, wo1 = w_ffn1_in[l].astype(BF16), w_ffn1_out[l].astype(BF16)
        win2, wo2 = w_ffn2_in[l].astype(BF16), w_ffn2_out[l].astype(BF16)
        wm = w_mix_in[l].astype(BF16)
        wc = wm[:, :3 * cwid]
        wq = wm[:, 3 * cwid:3 * cwid + 3 * kwid]
        wz = wm[:, 3 * cwid + 3 * kwid:3 * cwid + 4 * kwid]
        wab = jnp.pad(wm[:, 3 * cwid + 4 * kwid:], ((0, 0), (0, HEAD_DIM - n_gate)))
        zpad = jnp.zeros((N_HEADS,), F32)
        alog_row = jnp.concatenate([dn_a_log[l, 0], zpad, dn_a_log[l, 1], zpad,
                                    jnp.zeros((HEAD_DIM - n_gate,), F32)])[None, :]
        dtb_row = jnp.concatenate([dn_dt_bias[l, 0], zpad, dn_dt_bias[l, 1], zpad,
                                   jnp.zeros((HEAD_DIM - n_gate,), F32)])[None, :]
        wo = w_mix_out[l].astype(BF16)
        lw = (row(norm_ffn1[l]), win1, wo1, row(norm_mix[l]), wc, wq, wz, wab, conv_w[l], dn_conv_w[l],
              alog_row, dtb_row, row(dn_norm[l]), wo[:cwid], wo[cwid:], row(norm_ffn2[l]), win2, wo2)
        last = l == depth - 1
        xp, sf, sb = _layer(xp, mod_ctx, False, None, None, False, lw, row(norm_final), last,
                            TOKEN_TILE, TOKEN_TILE // xp.shape[1])
        new_f.append(sf)
        new_b.append(sb)
        xs, _, _ = _layer(xs, mod_lat, True, state_dn_fwd[:, l], state_dn_bwd[:, l], True, lw,
                          row(norm_final), last, TOKEN_TILE, 1)
    return xp, xs, jnp.stack(new_f, axis=1), jnp.stack(new_b, axis=1)
```

```python
import functools

import jax
import jax.numpy as jnp
from jax import lax
from jax.experimental import pallas as pl
from jax.experimental.pallas import tpu as pltpu

F32 = jnp.float32
BF16 = jnp.bfloat16

EPS = 1e-6
CHUNK = 64
GRID_W = 64
N_HEADS = 4
HEAD_DIM = 128
N_PAIRS = N_HEADS // 2
N_DIRS = 2
N_MOD = 9
SUBLANES = 8
FF_CHUNK = 256
TOKEN_TILE = 512
PREP_CHUNKS = 4
SCAN_ROWS = 8
VMEM_LIMIT = 56 * 1024 * 1024


def _cparams(*sem):
    return pltpu.CompilerParams(dimension_semantics=sem, vmem_limit_bytes=VMEM_LIMIT)


def _bdot(a, b):
    return jnp.dot(a.astype(BF16), b.astype(BF16), preferred_element_type=F32)


def _split_bf16(a):
    hi = a.astype(BF16)
    lo = (a - hi.astype(F32)).astype(BF16)
    return hi, lo


def _dot_x3(a, b):
    a_hi, a_lo = _split_bf16(a)
    b_hi, b_lo = _split_bf16(b)
    d = lambda x, y: jnp.dot(x, y, preferred_element_type=F32)
    return d(a_hi, b_hi) + (d(a_hi, b_lo) + d(a_lo, b_hi))


def _sigmoid(x):
    return 1.0 / (1.0 + jnp.exp(-x))


def _silu(x):
    return x * _sigmoid(x)


def _rms(x):
    return x * lax.rsqrt(jnp.mean(x * x, axis=-1, keepdims=True) + EPS)


def _mod_kernel(c_ref, w_ref, b_ref, o_ref):
    s = _silu(c_ref[...])
    o_ref[...] = _dot_x3(s, w_ref[...]) + b_ref[...]


def _modulation(cvec, w_ada, b_ada):
    rows, d = cvec.shape
    n = w_ada.shape[1]
    tn = 1536
    return pl.pallas_call(
        _mod_kernel,
        grid=(n // tn,),
        in_specs=[pl.BlockSpec((rows, d), lambda i: (0, 0)),
                  pl.BlockSpec((d, tn), lambda i: (0, i)),
                  pl.BlockSpec((1, tn), lambda i: (0, i))],
        out_specs=pl.BlockSpec((rows, tn), lambda i: (0, i)),
        out_shape=jax.ShapeDtypeStruct((rows, n), F32),
        compiler_params=_cparams("arbitrary"),
        name="modulation",
    )(cvec, w_ada, b_ada.reshape(1, n))


def _swiglu_residual(x, x_keep_ref, mod_ref, mod_row, g_ref, win_ref, wo_ref, h_ref, acc_ref):
    sh = mod_ref[0, mod_row:mod_row + 1, :]
    sc = mod_ref[0, mod_row + 1:mod_row + 2, :]
    gt = mod_ref[0, mod_row + 2:mod_row + 3, :]
    h_ref[...] = (_rms(x) * g_ref[...] * (1.0 + sc) + sh).astype(BF16)
    acc_ref[...] = jnp.zeros_like(acc_ref)
    ff = wo_ref.shape[0]
    for c in range(ff // FF_CHUNK):
        lo, hi = c * FF_CHUNK, (c + 1) * FF_CHUNK
        h = h_ref[...]
        a = jnp.dot(h, win_ref[:, lo:hi], preferred_element_type=F32)
        b = jnp.dot(h, win_ref[:, ff + lo:ff + hi], preferred_element_type=F32)
        act = (_silu(a) * b).astype(BF16)
        acc_ref[...] += jnp.dot(act, wo_ref[lo:hi, :], preferred_element_type=F32)
    return x_keep_ref[...] + 0.5 * gt * acc_ref[...]


def _ffn1_kernel(x_ref, mod_ref, g_ref, win_ref, wo_ref, o_ref, h_ref, acc_ref):
    o_ref[0] = _swiglu_residual(x_ref[0], x_ref.at[0], mod_ref, 0, g_ref, win_ref, wo_ref, h_ref, acc_ref)


def _mix_ffn2_kernel(x_ref, mod_ref, yc_ref, of_ref, ob_ref, z_ref, gn_ref, wt_ref, wbm_ref,
                     g_ref, win_ref, wo_ref, gfin_ref, o_ref, h_ref, acc_ref, x2_ref, *, final_norm):
    o = of_ref[0] + ob_ref[0]
    gate = _silu(z_ref[0])
    gn = gn_ref[...]
    ys = []
    for h in range(N_HEADS):
        sl = slice(h * HEAD_DIM, (h + 1) * HEAD_DIM)
        ys.append(_rms(o[:, sl]) * gn * gate[:, sl])
    y_dn = jnp.concatenate(ys, axis=1)
    m = _bdot(yc_ref[0], wt_ref[...]) + _bdot(y_dn, wbm_ref[...])
    x2_ref[...] = x_ref[0] + mod_ref[0, 5:6, :] * m
    y = _swiglu_residual(x2_ref[...], x2_ref, mod_ref, 6, g_ref, win_ref, wo_ref, h_ref, acc_ref)
    if final_norm:
        y = _rms(y) * gfin_ref[...]
    o_ref[0] = y


def _token_specs(d, tm, per_batch_mod):
    mod_map = (lambda b, i: (b, 0, 0)) if per_batch_mod else (lambda b, i: (0, 0, 0))
    return pl.BlockSpec((1, tm, d), lambda b, i: (b, i, 0)), pl.BlockSpec((1, N_MOD, d), mod_map)


def _ffn_weight_specs(win, wo):
    assert win.shape[1] == 2 * wo.shape[0] and wo.shape[0] % FF_CHUNK == 0
    const2 = lambda b, i: (0, 0)
    return [pl.BlockSpec(win.shape, const2, pipeline_mode=pl.Buffered(1)),
            pl.BlockSpec(wo.shape, const2, pipeline_mode=pl.Buffered(1))]


def _ffn1(x, mod, per_batch_mod, g, win, wo, tm):
    bsz, seq, d = x.shape
    const2 = lambda b, i: (0, 0)
    x_spec, mod_spec = _token_specs(d, tm, per_batch_mod)
    return pl.pallas_call(
        _ffn1_kernel,
        grid=(bsz, seq // tm),
        in_specs=[x_spec, mod_spec, pl.BlockSpec((1, d), const2)] + _ffn_weight_specs(win, wo),
        out_specs=x_spec,
        out_shape=jax.ShapeDtypeStruct(x.shape, F32),
        scratch_shapes=[pltpu.VMEM((tm, d), BF16), pltpu.VMEM((tm, d), F32)],
        compiler_params=_cparams("parallel", "arbitrary"),
        name="ffn1",
    )(x, mod, g, win, wo)


def _mix_ffn2(x, mod, per_batch_mod, yc, o_f, o_b, z, gn, wt, wbm, g, win, wo, gfin, final_norm, tm):
    bsz, seq, d = x.shape
    half = yc.shape[-1]
    const2 = lambda b, i: (0, 0)
    x_spec, mod_spec = _token_specs(d, tm, per_batch_mod)
    half_spec = pl.BlockSpec((1, tm, half), lambda b, i: (b, i, 0))
    return pl.pallas_call(
        functools.partial(_mix_ffn2_kernel, final_norm=final_norm),
        grid=(bsz, seq // tm),
        in_specs=[x_spec, mod_spec] + [half_spec] * 4
                 + [pl.BlockSpec((1, HEAD_DIM), const2),
                    pl.BlockSpec((half, d), const2, pipeline_mode=pl.Buffered(1)),
                    pl.BlockSpec((half, d), const2, pipeline_mode=pl.Buffered(1)),
                    pl.BlockSpec((1, d), const2)]
                 + _ffn_weight_specs(win, wo) + [pl.BlockSpec((1, d), const2)],
        out_specs=x_spec,
        out_shape=jax.ShapeDtypeStruct(x.shape, F32),
        scratch_shapes=[pltpu.VMEM((tm, d), BF16), pltpu.VMEM((tm, d), F32), pltpu.VMEM((tm, d), F32)],
        compiler_params=_cparams("parallel", "arbitrary"),
        name="mix_out_ffn2",
    )(x, mod, yc, o_f, o_b, z, gn, wt, wbm, g, win, wo, gfin)


def _neighbour_rows(ext, halo, tm, dist, first, last):
    prev = ext[halo - dist:halo - dist + tm]
    nxt = ext[halo + dist:halo + dist + tm]
    row = lax.broadcasted_iota(jnp.int32, prev.shape, 0)
    prev = jnp.where(first & (row < dist), 0.0, prev)
    nxt = jnp.where(last & (row >= tm - dist), 0.0, nxt)
    return prev, nxt


def _proj_prep_kernel(x_ref, xp_ref, xn_ref, mod_ref, g_ref, wc_ref, wq_ref, wz_ref, wab_ref,
                      cw_ref, dw_ref, alog_ref, dtb_ref,
                      yc_ref, q_ref, k_ref, v_ref, gb_ref, pz_ref, *, conv_dist, n_tiles):
    i = pl.program_id(1)
    first = i == 0
    last = i == n_tiles - 1
    tm = x_ref.shape[1]
    halo = xp_ref.shape[1]
    cwid = yc_ref.shape[-1]
    kwid = q_ref.shape[-1]
    sh = mod_ref[0, 3:4, :]
    sc = mod_ref[0, 4:5, :]
    norm = lambda xv: (_rms(xv) * g_ref[...] * (1.0 + sc) + sh).astype(BF16)
    h = norm(x_ref[0])
    h_ext = jnp.concatenate([norm(xp_ref[0]), h, norm(xn_ref[0])], axis=0)
    dot = lambda a, b: jnp.dot(a, b, preferred_element_type=F32)

    qh = SUBLANES
    pq_ext = dot(h_ext[halo - qh:halo + tm + qh], wq_ref[...])
    pch_ext = dot(h_ext, wc_ref[:, cwid:])
    pq = pq_ext[qh:qh + tm]
    qp, qn = _neighbour_rows(pq_ext, qh, tm, 1, first, last)
    dw = dw_ref[...]
    qkv = _silu(qp * dw[0:1] + pq * dw[1:2] + qn * dw[2:3])
    for hd in range(N_HEADS):
        sl = slice(hd * HEAD_DIM, (hd + 1) * HEAD_DIM)
        qv = qkv[:, sl]
        kv = qkv[:, kwid + hd * HEAD_DIM: kwid + (hd + 1) * HEAD_DIM]
        q_ref[0, :, sl] = qv * (lax.rsqrt(jnp.sum(qv * qv, axis=-1, keepdims=True) + EPS) * HEAD_DIM ** -0.5)
        k_ref[0, :, sl] = kv * lax.rsqrt(jnp.sum(kv * kv, axis=-1, keepdims=True) + EPS)
    v_ref[0] = qkv[:, 2 * kwid:]

    pcb = dot(h, wc_ref[:, :cwid])
    pz_ref[0] = dot(h, wz_ref[...])
    ab = dot(h, wab_ref[...])
    u_ext = pch_ext[:, :cwid] * pch_ext[:, cwid:]
    u = u_ext[halo:halo + tm]
    up, un = _neighbour_rows(u_ext, halo, tm, conv_dist, first, last)
    cw = cw_ref[...]
    yc_ref[0] = pcb * (up * cw[0:1] + u * cw[1:2] + un * cw[2:3])

    lane = lax.broadcasted_iota(jnp.int32, ab.shape, 1)
    is_g = (lane < N_HEADS) | ((lane >= 2 * N_HEADS) & (lane < 3 * N_HEADS))
    xg = ab + dtb_ref[...]
    softplus = jnp.maximum(xg, 0.0) + jnp.log1p(jnp.exp(-jnp.abs(xg)))
    gates = jnp.where(is_g, -jnp.exp(alog_ref[...]) * softplus, _sigmoid(ab))
    r = lax.broadcasted_iota(jnp.int32, (CHUNK, CHUNK), 0)
    c = lax.broadcasted_iota(jnp.int32, (CHUNK, CHUNK), 1)
    tril = (r >= c).astype(BF16)
    triu = (r <= c).astype(BF16)
    g_hi = gates.astype(BF16)
    rem = gates - g_hi.astype(F32)
    g_mid = rem.astype(BF16)
    g_lo = (rem - g_mid.astype(F32)).astype(BF16)
    lane_c = lax.broadcasted_iota(jnp.int32, (CHUNK, ab.shape[1]), 1)
    is_gf_c = lane_c < N_HEADS
    is_gb_c = (lane_c >= 2 * N_HEADS) & (lane_c < 3 * N_HEADS)
    for t in range(tm // CHUNK):
        rs = slice(t * CHUNK, (t + 1) * CHUNK)
        parts = (g_hi[rs], g_mid[rs], g_lo[rs])
        pre = dot(tril, parts[0]) + (dot(tril, parts[1]) + dot(tril, parts[2]))
        suf = dot(triu, parts[0]) + (dot(triu, parts[1]) + dot(triu, parts[2]))
        gb_ref[0, rs, :] = jnp.where(is_gf_c, pre, jnp.where(is_gb_c, suf, gates[rs]))


def _proj_prep(x, mod, per_batch_mod, g, wc, wq, wz, wab, conv_w, dn_conv_w, alog_row, dtb_row, conv_dist, tm):
    bsz, seq, d = x.shape
    cwid, kwid = wc.shape[1] // 3, wq.shape[1] // 3
    n_tiles = seq // tm
    halo = max(conv_dist, SUBLANES)
    assert tm % halo == 0 and halo % SUBLANES == 0
    const2 = lambda b, i: (0, 0)
    tok = lambda b, i: (b, i, 0)
    x_spec, mod_spec = _token_specs(d, tm, per_batch_mod)
    prev_map = lambda b, i: (b, jnp.maximum(i * (tm // halo) - 1, 0), 0)
    next_map = lambda b, i: (b, jnp.minimum((i + 1) * (tm // halo), seq // halo - 1), 0)
    widths = (cwid, kwid, kwid, kwid, wab.shape[1], wz.shape[1])
    return pl.pallas_call(
        functools.partial(_proj_prep_kernel, conv_dist=conv_dist, n_tiles=n_tiles),
        grid=(bsz, n_tiles),
        in_specs=[x_spec, pl.BlockSpec((1, halo, d), prev_map), pl.BlockSpec((1, halo, d), next_map),
                  mod_spec, pl.BlockSpec((1, d), const2)]
                 + [pl.BlockSpec(w.shape, const2, pipeline_mode=pl.Buffered(1)) for w in (wc, wq, wz, wab)]
                 + [pl.BlockSpec(a.shape, const2) for a in (conv_w, dn_conv_w, alog_row, dtb_row)],
        out_specs=[pl.BlockSpec((1, tm, w), tok) for w in widths],
        out_shape=[jax.ShapeDtypeStruct((bsz, seq, w), F32) for w in widths],
        compiler_params=_cparams("parallel", "arbitrary"),
        name="mix_in_proj_prep",
    )(x, x, x, mod, g, wc, wq, wz, wab, conv_w, dn_conv_w, alog_row, dtb_row)


def _pair_block_diag(y16, keep_left, keep_right):
    return jnp.concatenate([y16 * keep_left, y16 * keep_right], axis=0)


INV_BASE = 8


def _pair_matmul_x3(lhs_list, rhs, keep_left, keep_right):
    d = lambda x, y: jnp.dot(x, y, preferred_element_type=F32)
    r_hi, r_lo = _split_bf16(rhs)
    r_hi = _pair_block_diag(r_hi, keep_left, keep_right)
    r_lo = _pair_block_diag(r_lo, keep_left, keep_right)
    parts = [_split_bf16(x) for x in lhs_list]
    l1 = jnp.concatenate([jnp.concatenate([hi, lo], axis=1) for hi, lo in parts], axis=0)
    l2 = jnp.concatenate([hi for hi, _ in parts], axis=0)
    res = d(l1, jnp.concatenate([r_hi, r_hi], axis=0)) + d(l2, r_lo)
    return [res[i * CHUNK:(i + 1) * CHUNK] for i in range(len(lhs_list))]


def _pair_inverses(a_list, row, col, eye_pair, keep_left, keep_right):
    mm = lambda lhs_list, rhs: _pair_matmul_x3(lhs_list, rhs, keep_left, keep_right)
    shift = INV_BASE.bit_length() - 1
    same_base = (row >> shift) == (col >> shift)
    ps = [jnp.where(same_base, -a, 0.0) for a in a_list]
    ts = [eye_pair + p for p in ps]
    n_sq = shift - 1
    for i in range(n_sq + 1):
        new_ps, new_ts = [], []
        for p, t in zip(ps, ts):
            if i == 0:
                new_ps.append(mm([p], p)[0])
                new_ts.append(t)
            elif i == n_sq:
                new_ps.append(p)
                new_ts.append(t + mm([t], p)[0])
            else:
                p2, tp = mm([p, t], p)
                new_ps.append(p2)
                new_ts.append(t + tp)
        ps, ts = new_ps, new_ts
    size = INV_BASE
    while size < CHUNK:
        s1 = size.bit_length() - 1
        off = ((row >> (s1 + 1)) == (col >> (s1 + 1))) & ((row >> s1) != (col >> s1))
        es = [jnp.where(off, a, 0.0) for a in a_list]
        xs = [mm([e], t)[0] for e, t in zip(es, ts)]
        ts = [t - mm([t], x)[0] for t, x in zip(ts, xs)]
        size *= 2
    return ts


def _chunk_prep_kernel(q_ref, k_ref, v_ref, g_ref,
                       uf_ref, ub_ref, wf_ref, wb_ref, qgf_ref, qgb_ref, kgf_ref, kgb_ref,
                       qkf_ref, qkb_ref, glf_ref, glb_ref):
    u_refs, w_refs, qg_refs = (uf_ref, ub_ref), (wf_ref, wb_ref), (qgf_ref, qgb_ref)
    kg_refs, qk_refs, gl_refs = (kgf_ref, kgb_ref), (qkf_ref, qkb_ref), (glf_ref, glb_ref)
    row = lax.broadcasted_iota(jnp.int32, (CHUNK, 2 * CHUNK), 0)
    lane = lax.broadcasted_iota(jnp.int32, (CHUNK, 2 * CHUNK), 1)
    col = lane & (CHUNK - 1)
    left = lane < CHUNK
    keep_left = jnp.where(left, 1.0, 0.0).astype(BF16)
    keep_right = jnp.where(left, 0.0, 1.0).astype(BF16)
    eye_pair = jnp.where(row == col, 1.0, 0.0)
    zeros16 = jnp.zeros((CHUNK, HEAD_DIM), BF16)
    dot = lambda x, y: jnp.dot(x, y, preferred_element_type=F32)

    chains, a_list = [], []
    for c in range(PREP_CHUNKS):
        rs = slice(c * CHUNK, (c + 1) * CHUNK)
        gates = g_ref[0, rs, :]
        for p in range(N_PAIRS):
            sl = slice(2 * p * HEAD_DIM, 2 * (p + 1) * HEAD_DIM)
            q2, k2, v2 = q_ref[0, rs, sl], k_ref[0, rs, sl], v_ref[0, rs, sl]
            k16 = k2.astype(BF16)
            k_bd = jnp.concatenate([jnp.concatenate([k16[:, :HEAD_DIM], zeros16], axis=1),
                                    jnp.concatenate([zeros16, k16[:, HEAD_DIM:]], axis=1)], axis=0)
            kq = jnp.concatenate([k16, q2.astype(BF16)], axis=0)
            gram = lax.dot_general(kq, k_bd, (((1,), (1,)), ((), ())), preferred_element_type=F32)
            for d in range(N_DIRS):
                gcol, bcol = 2 * d * N_HEADS, (2 * d + 1) * N_HEADS
                gc = [gates[:, gcol + 2 * p + e:gcol + 2 * p + e + 1] for e in range(2)]
                beta = [gates[:, bcol + 2 * p + e:bcol + 2 * p + e + 1] for e in range(2)]
                gc_pair = jnp.where(left, gc[0], gc[1])
                beta_pair = jnp.where(left, beta[0], beta[1])
                gr_pair = jnp.sum(jnp.where(row == col, gc_pair, 0.0), axis=0, keepdims=True)
                if d == 0:
                    incl, strict, last = row >= col, row > col, CHUNK - 1
                else:
                    incl, strict, last = row <= col, row < col, 0
                decay = jnp.where(incl, jnp.exp(jnp.where(incl, gc_pair - gr_pair, 0.0)), 0.0)
                a_list.append(jnp.where(strict, beta_pair * gram[:CHUNK] * decay, 0.0))
                qk_refs[d][0, rs, p * HEAD_DIM:(p + 1) * HEAD_DIM] = (
                    jnp.where(incl, gram[CHUNK:] * decay, 0.0).astype(BF16))
                chains.append((c, rs, p, d, q2, k2, v2, gc, beta, last))

    t_list = _pair_inverses(a_list, row, col, eye_pair, keep_left, keep_right)

    for (c, rs, p, d, q2, k2, v2, gc, beta, last), t in zip(chains, t_list):
        t16 = t.astype(BF16)
        for e in range(2):
            h = 2 * p + e
            sl = slice(e * HEAD_DIM, (e + 1) * HEAD_DIM)
            hs = slice(h * HEAD_DIM, (h + 1) * HEAD_DIM)
            eg = jnp.exp(gc[e])
            g_last = gc[e][last:last + 1, :]
            rhs = jnp.concatenate([v2[:, sl] * beta[e], k2[:, sl] * (beta[e] * eg)], axis=1).astype(BF16)
            zpad = jnp.zeros_like(rhs)
            uw = dot(t16, jnp.concatenate([rhs, zpad] if e == 0 else [zpad, rhs], axis=0))
            u_refs[d][0, rs, hs] = uw[:, :HEAD_DIM]
            w_refs[d][0, rs, hs] = uw[:, HEAD_DIM:].astype(BF16)
            qg_refs[d][0, rs, hs] = (q2[:, sl] * eg).astype(BF16)
            kg_refs[d][0, rs, hs] = (k2[:, sl] * jnp.exp(g_last - gc[e])).astype(BF16)
            gl_refs[d][0, c, h:h + 1, :] = jnp.broadcast_to(jnp.exp(g_last), (1, HEAD_DIM))


def _chunk_prep(q, k, v, gb):
    bsz, seq, kwid = q.shape
    rows = PREP_CHUNKS * CHUNK
    n = seq // CHUNK
    tok = lambda b, i: (b, i, 0)
    f32_full = jax.ShapeDtypeStruct(q.shape, F32)
    bf_full = jax.ShapeDtypeStruct(q.shape, BF16)
    bf_half = jax.ShapeDtypeStruct((bsz, seq, kwid // 2), BF16)
    gl = jax.ShapeDtypeStruct((bsz, n, N_HEADS, HEAD_DIM), F32)
    outs = [f32_full] * 2 + [bf_full] * 6 + [bf_half] * 2 + [gl] * 2
    out_specs = ([pl.BlockSpec((1, rows, kwid), tok)] * 8 + [pl.BlockSpec((1, rows, kwid // 2), tok)] * 2
                 + [pl.BlockSpec((1, PREP_CHUNKS, N_HEADS, HEAD_DIM), lambda b, i: (b, i, 0, 0))] * 2)
    return pl.pallas_call(
        _chunk_prep_kernel,
        grid=(bsz, seq // rows),
        in_specs=[pl.BlockSpec((1, rows, kwid), tok)] * 3 + [pl.BlockSpec((1, rows, gb.shape[-1]), tok)],
        out_specs=out_specs,
        out_shape=outs,
        compiler_params=_cparams("parallel", "parallel"),
        name="delta_chunk_prep",
    )(q, k, v, gb)


def _scan_kernel(uf_ref, wf_ref, qgf_ref, kgf_ref, qkf_ref, glf_ref,
                 ub_ref, wb_ref, qgb_ref, kgb_ref, qkb_ref, glb_ref, s0f_ref, s0b_ref,
                 of_ref, ob_ref, sff_ref, sfb_ref, s_ref, *, n_chunks, has_s0):
    j = pl.program_id(1)
    n_hd = N_DIRS * N_HEADS

    @pl.when(j == 0)
    def _():
        for r in range(SCAN_ROWS):
            if has_s0:
                s_ref[r * n_hd:r * n_hd + N_HEADS] = s0f_ref[r]
                s_ref[r * n_hd + N_HEADS:(r + 1) * n_hd] = s0b_ref[r]
            else:
                s_ref[r * n_hd:(r + 1) * n_hd] = jnp.zeros((n_hd, HEAD_DIM, HEAD_DIM), F32)

    dot = lambda x, y: jnp.dot(x, y, preferred_element_type=F32)
    zeros16 = jnp.zeros((CHUNK, HEAD_DIM), BF16)
    ins = ((uf_ref, wf_ref, qgf_ref, kgf_ref, qkf_ref, glf_ref), (ub_ref, wb_ref, qgb_ref, kgb_ref, qkb_ref, glb_ref))
    outs = (of_ref, ob_ref)
    hds = [(r, d, h) for r in range(SCAN_ROWS) for d in range(N_DIRS) for h in range(N_HEADS)]
    hsl = lambda h: slice(h * HEAD_DIM, (h + 1) * HEAD_DIM)
    sidx = lambda r, d, h: r * n_hd + d * N_HEADS + h

    s_old, ws_qs = [], []
    for r, d, h in hds:
        s = s_ref[sidx(r, d, h)]
        s_old.append(s)
        lhs = jnp.concatenate([ins[d][1][r, :, hsl(h)], ins[d][2][r, :, hsl(h)]], axis=0)
        ws_qs.append(dot(lhs, s.astype(BF16)))
    v_new = []
    for i, (r, d, h) in enumerate(hds):
        vn = (ins[d][0][r, :, hsl(h)] - ws_qs[i][:CHUNK]).astype(BF16)
        v_new.append(vn)
        v_pad = jnp.concatenate([vn, zeros16] if h % 2 == 0 else [zeros16, vn], axis=0)
        qk_pair = ins[d][4][r, :, hsl(h // 2)]
        outs[d][r, :, hsl(h)] = ws_qs[i][CHUNK:] + dot(qk_pair, v_pad)
    for i, (r, d, h) in enumerate(hds):
        kg = ins[d][3][r, :, hsl(h)]
        upd = lax.dot_general(kg, v_new[i], (((0,), (0,)), ((), ())), preferred_element_type=F32)
        s_ref[sidx(r, d, h)] = s_old[i] * ins[d][5][r, 0, h:h + 1, :] + upd

    @pl.when(j == n_chunks - 1)
    def _():
        for r in range(SCAN_ROWS):
            sff_ref[r] = s_ref[r * n_hd:r * n_hd + N_HEADS]
            sfb_ref[r] = s_ref[r * n_hd + N_HEADS:(r + 1) * n_hd]


def _scan(prep, s0_f, s0_b):
    (u_f, u_b, w_f, w_b, qg_f, qg_b, kg_f, kg_b, qk_f, qk_b, gl_f, gl_b) = prep
    bsz, seq, kwid = u_f.shape
    assert bsz % SCAN_ROWS == 0
    n = seq // CHUNK
    has_s0 = s0_f is not None
    sblk = (SCAN_ROWS, N_HEADS, HEAD_DIM, HEAD_DIM)
    if not has_s0:
        s0_f = s0_b = jnp.zeros(sblk, F32)
    fwd = lambda b, j: (b, j, 0)
    bwd = lambda b, j: (b, n - 1 - j, 0)
    fwd4 = lambda b, j: (b, j, 0, 0)
    bwd4 = lambda b, j: (b, n - 1 - j, 0, 0)
    smap = (lambda b, j: (b, 0, 0, 0)) if has_s0 else (lambda b, j: (0, 0, 0, 0))

    def dir_specs(tok, tok4):
        return ([pl.BlockSpec((SCAN_ROWS, CHUNK, kwid), tok)] * 4
                + [pl.BlockSpec((SCAN_ROWS, CHUNK, kwid // 2), tok),
                   pl.BlockSpec((SCAN_ROWS, 1, N_HEADS, HEAD_DIM), tok4)])

    return pl.pallas_call(
        functools.partial(_scan_kernel, n_chunks=n, has_s0=has_s0),
        grid=(bsz // SCAN_ROWS, n),
        in_specs=dir_specs(fwd, fwd4) + dir_specs(bwd, bwd4) + [pl.BlockSpec(sblk, smap)] * 2,
        out_specs=[pl.BlockSpec((SCAN_ROWS, CHUNK, kwid), fwd),
                   pl.BlockSpec((SCAN_ROWS, CHUNK, kwid), bwd),
                   pl.BlockSpec(sblk, lambda b, j: (b, 0, 0, 0)),
                   pl.BlockSpec(sblk, lambda b, j: (b, 0, 0, 0))],
        out_shape=[jax.ShapeDtypeStruct(u_f.shape, F32)] * 2
                  + [jax.ShapeDtypeStruct((bsz,) + sblk[1:], F32)] * 2,
        scratch_shapes=[pltpu.VMEM((SCAN_ROWS * N_DIRS * N_HEADS, HEAD_DIM, HEAD_DIM), F32)],
        compiler_params=_cparams("parallel", "arbitrary"),
        name="delta_scan",
    )(u_f, w_f, qg_f, kg_f, qk_f, gl_f, u_b, w_b, qg_b, kg_b, qk_b, gl_b, s0_f, s0_b)


def _layer(x, mod, per_batch_mod, s0_f, s0_b, on_grid, lw, norm_final, final_norm, tm, fold):
    (g1, win1, wo1, gm, wc, wq, wz, wab, conv_w, dn_conv_w, alog_row, dtb_row, gn, wt, wbo,
     g2, win2, wo2) = lw
    bsz, seq, d = x.shape
    assert fold == 1 or not per_batch_mod
    folded = lambda a: a.reshape(bsz // fold, fold * seq, a.shape[-1])
    unfolded = lambda a: a.reshape(bsz, seq, a.shape[-1])
    x = _ffn1(folded(x), mod, per_batch_mod, g1, win1, wo1, tm)
    yc, q, k, v, gb, pz = _proj_prep(unfolded(x), mod, per_batch_mod, gm, wc, wq, wz, wab, conv_w, dn_conv_w,
                                     alog_row, dtb_row, GRID_W if on_grid else 1, min(tm, seq))
    o_f, o_b, s_f, s_b = _scan(_chunk_prep(q, k, v, gb), s0_f, s0_b)
    y = _mix_ffn2(x, mod, per_batch_mod, folded(yc), folded(o_f), folded(o_b), folded(pz), gn, wt, wbo,
                  g2, win2, wo2, norm_final, final_norm, tm)
    return unfolded(y), s_f, s_b


def kernel(x_prompt, x_sample, state_dn_fwd, state_dn_bwd, c, c_ctx, w_ada, b_ada, norm_ffn1, w_ffn1_in,
           w_ffn1_out, norm_mix, w_mix_in, conv_w, dn_conv_w, dn_a_log, dn_dt_bias, dn_norm, w_mix_out,
           norm_ffn2, w_ffn2_in, w_ffn2_out, norm_final):
    depth = w_ada.shape[0]
    d = x_prompt.shape[-1]
    n_lat = c.shape[0]
    cwid = conv_w.shape[-1]
    kwid = N_HEADS * HEAD_DIM
    n_gate = 4 * N_HEADS
    row = lambda a: a.reshape(1, -1)

    cvec = jnp.concatenate([c_ctx[None, :], c, jnp.zeros((16 - 1 - n_lat, d), F32)], axis=0)

    xp, xs = x_prompt, x_sample
    new_f, new_b = [], []
    for l in range(depth):
        mod = _modulation(cvec, w_ada[l], b_ada[l]).reshape(16, N_MOD, d)
        mod_ctx, mod_lat = mod[0:1], mod[1:1 + n_lat]

        win1, wo1 = w_ffn1_in[l].astype(BF16), w_ffn1_out[l].astype(BF16)
        win2, wo2 = w_ffn2_in[l].astype(BF16), w_ffn2_out[l].astype(BF16)
        wm = w_mix_in[l].astype(BF16)
        wc = wm[:, :3 * cwid]
        wq = wm[:, 3 * cwid:3 * cwid + 3 * kwid]
        wz = wm[:, 3 * cwid + 3 * kwid:3 * cwid + 4 * kwid]
        wab = jnp.pad(wm[:, 3 * cwid + 4 * kwid:], ((0, 0), (0, HEAD_DIM - n_gate)))
        zpad = jnp.zeros((N_HEADS,), F32)
        alog_row = jnp.concatenate([dn_a_log[l, 0], zpad, dn_a_log[l, 1], zpad,
                                    jnp.zeros((HEAD_DIM - n_gate,), F32)])[None, :]
        dtb_row = jnp.concatenate([dn_dt_bias[l, 0], zpad, dn_dt_bias[l, 1], zpad,
                                   jnp.zeros((HEAD_DIM - n_gate,), F32)])[None, :]
        wo = w_mix_out[l].astype(BF16)
        lw = (row(norm_ffn1[l]), win1, wo1, row(norm_mix[l]), wc, wq, wz, wab, conv_w[l], dn_conv_w[l],
              alog_row, dtb_row, row(dn_norm[l]), wo[:cwid], wo[cwid:], row(norm_ffn2[l]), win2, wo2)
        last = l == depth - 1
        xp, sf, sb = _layer(xp, mod_ctx, False, None, None, False, lw, row(norm_final), last,
                            TOKEN_TILE, TOKEN_TILE // xp.shape[1])
        new_f.append(sf)
        new_b.append(sb)
        xs, _, _ = _layer(xs, mod_lat, True, state_dn_fwd[:, l], state_dn_bwd[:, l], True, lw,
                          row(norm_final), last, TOKEN_TILE, 1)
    return xp, xs, jnp.stack(new_f, axis=1), jnp.stack(new_b, axis=1)
```

```python
import functools

import jax
import jax.numpy as jnp
from jax import lax
from jax.experimental import pallas as pl
from jax.experimental.pallas import tpu as pltpu

F32 = jnp.float32
BF16 = jnp.bfloat16

EPS = 1e-6
CHUNK = 64
GRID_W = 64
N_HEADS = 4
HEAD_DIM = 128
N_PAIRS = N_HEADS // 2
N_DIRS = 2
N_MOD = 9
SUBLANES = 8
FF_CHUNK = 256
TOKEN_TILE = 512
PREP_CHUNKS = 8
SCAN_ROWS = 8
VMEM_LIMIT = 56 * 1024 * 1024


def _cparams(*sem):
    return pltpu.CompilerParams(dimension_semantics=sem, vmem_limit_bytes=VMEM_LIMIT)


def _bdot(a, b):
    return jnp.dot(a.astype(BF16), b.astype(BF16), preferred_element_type=F32)


def _split_bf16(a):
    hi = a.astype(BF16)
    lo = (a - hi.astype(F32)).astype(BF16)
    return hi, lo


def _dot_x3(a, b):
    a_hi, a_lo = _split_bf16(a)
    b_hi, b_lo = _split_bf16(b)
    d = lambda x, y: jnp.dot(x, y, preferred_element_type=F32)
    return d(a_hi, b_hi) + (d(a_hi, b_lo) + d(a_lo, b_hi))


def _sigmoid(x):
    return 1.0 / (1.0 + jnp.exp(-x))


def _silu(x):
    return x * _sigmoid(x)


def _rms(x):
    return x * lax.rsqrt(jnp.mean(x * x, axis=-1, keepdims=True) + EPS)


def _mod_kernel(c_ref, w_ref, b_ref, o_ref):
    s = _silu(c_ref[...])
    o_ref[...] = _dot_x3(s, w_ref[...]) + b_ref[...]


def _modulation(cvec, w_ada, b_ada):
    rows, d = cvec.shape
    n = w_ada.shape[1]
    tn = 1536
    return pl.pallas_call(
        _mod_kernel,
        grid=(n // tn,),
        in_specs=[pl.BlockSpec((rows, d), lambda i: (0, 0)),
                  pl.BlockSpec((d, tn), lambda i: (0, i)),
                  pl.BlockSpec((1, tn), lambda i: (0, i))],
        out_specs=pl.BlockSpec((rows, tn), lambda i: (0, i)),
        out_shape=jax.ShapeDtypeStruct((rows, n), F32),
        compiler_params=_cparams("arbitrary"),
        name="modulation",
    )(cvec, w_ada, b_ada.reshape(1, n))


def _swiglu_residual(x, x_keep_ref, mod_ref, mod_row, g_ref, win_ref, wo_ref, h_ref, acc_ref):
    sh = mod_ref[0, mod_row:mod_row + 1, :]
    sc = mod_ref[0, mod_row + 1:mod_row + 2, :]
    gt = mod_ref[0, mod_row + 2:mod_row + 3, :]
    h_ref[...] = (_rms(x) * g_ref[...] * (1.0 + sc) + sh).astype(BF16)
    acc_ref[...] = jnp.zeros_like(acc_ref)
    ff = wo_ref.shape[0]
    for c in range(ff // FF_CHUNK):
        lo, hi = c * FF_CHUNK, (c + 1) * FF_CHUNK
        h = h_ref[...]
        a = jnp.dot(h, win_ref[:, lo:hi], preferred_element_type=F32)
        b = jnp.dot(h, win_ref[:, ff + lo:ff + hi], preferred_element_type=F32)
        act = (_silu(a) * b).astype(BF16)
        acc_ref[...] += jnp.dot(act, wo_ref[lo:hi, :], preferred_element_type=F32)
    return x_keep_ref[...] + 0.5 * gt * acc_ref[...]


def _ffn1_kernel(x_ref, mod_ref, g_ref, win_ref, wo_ref, o_ref, h_ref, acc_ref):
    o_ref[0] = _swiglu_residual(x_ref[0], x_ref.at[0], mod_ref, 0, g_ref, win_ref, wo_ref, h_ref, acc_ref)


def _mix_ffn2_kernel(x_ref, mod_ref, yc_ref, of_ref, ob_ref, z_ref, gn_ref, wt_ref, wbm_ref,
                     g_ref, win_ref, wo_ref, gfin_ref, o_ref, h_ref, acc_ref, x2_ref, *, final_norm):
    o = of_ref[0] + ob_ref[0]
    gate = _silu(z_ref[0])
    gn = gn_ref[...]
    ys = []
    for h in range(N_HEADS):
        sl = slice(h * HEAD_DIM, (h + 1) * HEAD_DIM)
        ys.append(_rms(o[:, sl]) * gn * gate[:, sl])
    y_dn = jnp.concatenate(ys, axis=1)
    m = _bdot(yc_ref[0], wt_ref[...]) + _bdot(y_dn, wbm_ref[...])
    x2_ref[...] = x_ref[0] + mod_ref[0, 5:6, :] * m
    y = _swiglu_residual(x2_ref[...], x2_ref, mod_ref, 6, g_ref, win_ref, wo_ref, h_ref, acc_ref)
    if final_norm:
        y = _rms(y) * gfin_ref[...]
    o_ref[0] = y


def _token_specs(d, tm, per_batch_mod):
    mod_map = (lambda b, i: (b, 0, 0)) if per_batch_mod else (lambda b, i: (0, 0, 0))
    return pl.BlockSpec((1, tm, d), lambda b, i: (b, i, 0)), pl.BlockSpec((1, N_MOD, d), mod_map)


def _ffn_weight_specs(win, wo):
    assert win.shape[1] == 2 * wo.shape[0] and wo.shape[0] % FF_CHUNK == 0
    const2 = lambda b, i: (0, 0)
    return [pl.BlockSpec(win.shape, const2, pipeline_mode=pl.Buffered(1)),
            pl.BlockSpec(wo.shape, const2, pipeline_mode=pl.Buffered(1))]


def _ffn1(x, mod, per_batch_mod, g, win, wo, tm):
    bsz, seq, d = x.shape
    const2 = lambda b, i: (0, 0)
    x_spec, mod_spec = _token_specs(d, tm, per_batch_mod)
    return pl.pallas_call(
        _ffn1_kernel,
        grid=(bsz, seq // tm),
        in_specs=[x_spec, mod_spec, pl.BlockSpec((1, d), const2)] + _ffn_weight_specs(win, wo),
        out_specs=x_spec,
        out_shape=jax.ShapeDtypeStruct(x.shape, F32),
        scratch_shapes=[pltpu.VMEM((tm, d), BF16), pltpu.VMEM((tm, d), F32)],
        compiler_params=_cparams("parallel", "arbitrary"),
        name="ffn1",
    )(x, mod, g, win, wo)


def _mix_ffn2(x, mod, per_batch_mod, yc, o_f, o_b, z, gn, wt, wbm, g, win, wo, gfin, final_norm, tm):
    bsz, seq, d = x.shape
    half = yc.shape[-1]
    const2 = lambda b, i: (0, 0)
    x_spec, mod_spec = _token_specs(d, tm, per_batch_mod)
    half_spec = pl.BlockSpec((1, tm, half), lambda b, i: (b, i, 0))
    return pl.pallas_call(
        functools.partial(_mix_ffn2_kernel, final_norm=final_norm),
        grid=(bsz, seq // tm),
        in_specs=[x_spec, mod_spec] + [half_spec] * 4
                 + [pl.BlockSpec((1, HEAD_DIM), const2),
                    pl.BlockSpec((half, d), const2, pipeline_mode=pl.Buffered(1)),
                    pl.BlockSpec((half, d), const2, pipeline_mode=pl.Buffered(1)),
                    pl.BlockSpec((1, d), const2)]
                 + _ffn_weight_specs(win, wo) + [pl.BlockSpec((1, d), const2)],
        out_specs=x_spec,
        out_shape=jax.ShapeDtypeStruct(x.shape, F32),
        scratch_shapes=[pltpu.VMEM((tm, d), BF16), pltpu.VMEM((tm, d), F32), pltpu.VMEM((tm, d), F32)],
        compiler_params=_cparams("parallel", "arbitrary"),
        name="mix_out_ffn2",
    )(x, mod, yc, o_f, o_b, z, gn, wt, wbm, g, win, wo, gfin)


def _neighbour_rows(ext, halo, tm, dist, first, last):
    prev = ext[halo - dist:halo - dist + tm]
    nxt = ext[halo + dist:halo + dist + tm]
    row = lax.broadcasted_iota(jnp.int32, prev.shape, 0)
    prev = jnp.where(first & (row < dist), 0.0, prev)
    nxt = jnp.where(last & (row >= tm - dist), 0.0, nxt)
    return prev, nxt


def _proj_prep_kernel(x_ref, xp_ref, xn_ref, mod_ref, g_ref, wc_ref, wq_ref, wz_ref, wab_ref,
                      cw_ref, dw_ref, alog_ref, dtb_ref,
                      yc_ref, q_ref, k_ref, v_ref, gb_ref, pz_ref, *, conv_dist, n_tiles):
    i = pl.program_id(1)
    first = i == 0
    last = i == n_tiles - 1
    tm = x_ref.shape[1]
    halo = xp_ref.shape[1]
    cwid = yc_ref.shape[-1]
    kwid = q_ref.shape[-1]
    sh = mod_ref[0, 3:4, :]
    sc = mod_ref[0, 4:5, :]
    norm = lambda xv: (_rms(xv) * g_ref[...] * (1.0 + sc) + sh).astype(BF16)
    h = norm(x_ref[0])
    h_ext = jnp.concatenate([norm(xp_ref[0]), h, norm(xn_ref[0])], axis=0)
    dot = lambda a, b: jnp.dot(a, b, preferred_element_type=F32)

    qh = SUBLANES
    pq_ext = dot(h_ext[halo - qh:halo + tm + qh], wq_ref[...])
    pch_ext = dot(h_ext, wc_ref[:, cwid:])
    pq = pq_ext[qh:qh + tm]
    qp, qn = _neighbour_rows(pq_ext, qh, tm, 1, first, last)
    dw = dw_ref[...]
    qkv = _silu(qp * dw[0:1] + pq * dw[1:2] + qn * dw[2:3])
    for hd in range(N_HEADS):
        sl = slice(hd * HEAD_DIM, (hd + 1) * HEAD_DIM)
        qv = qkv[:, sl]
        kv = qkv[:, kwid + hd * HEAD_DIM: kwid + (hd + 1) * HEAD_DIM]
        q_ref[0, :, sl] = qv * (lax.rsqrt(jnp.sum(qv * qv, axis=-1, keepdims=True) + EPS) * HEAD_DIM ** -0.5)
        k_ref[0, :, sl] = kv * lax.rsqrt(jnp.sum(kv * kv, axis=-1, keepdims=True) + EPS)
    v_ref[0] = qkv[:, 2 * kwid:]

    pcb = dot(h, wc_ref[:, :cwid])
    pz_ref[0] = dot(h, wz_ref[...])
    ab = dot(h, wab_ref[...])
    u_ext = pch_ext[:, :cwid] * pch_ext[:, cwid:]
    u = u_ext[halo:halo + tm]
    up, un = _neighbour_rows(u_ext, halo, tm, conv_dist, first, last)
    cw = cw_ref[...]
    yc_ref[0] = pcb * (up * cw[0:1] + u * cw[1:2] + un * cw[2:3])

    lane = lax.broadcasted_iota(jnp.int32, ab.shape, 1)
    is_g = (lane < N_HEADS) | ((lane >= 2 * N_HEADS) & (lane < 3 * N_HEADS))
    xg = ab + dtb_ref[...]
    softplus = jnp.maximum(xg, 0.0) + jnp.log1p(jnp.exp(-jnp.abs(xg)))
    gates = jnp.where(is_g, -jnp.exp(alog_ref[...]) * softplus, _sigmoid(ab))
    r = lax.broadcasted_iota(jnp.int32, (CHUNK, CHUNK), 0)
    c = lax.broadcasted_iota(jnp.int32, (CHUNK, CHUNK), 1)
    tril = (r >= c).astype(BF16)
    triu = (r <= c).astype(BF16)
    g_hi = gates.astype(BF16)
    rem = gates - g_hi.astype(F32)
    g_mid = rem.astype(BF16)
    g_lo = (rem - g_mid.astype(F32)).astype(BF16)
    lane_c = lax.broadcasted_iota(jnp.int32, (CHUNK, ab.shape[1]), 1)
    is_gf_c = lane_c < N_HEADS
    is_gb_c = (lane_c >= 2 * N_HEADS) & (lane_c < 3 * N_HEADS)
    for t in range(tm // CHUNK):
        rs = slice(t * CHUNK, (t + 1) * CHUNK)
        parts = (g_hi[rs], g_mid[rs], g_lo[rs])
        pre = dot(tril, parts[0]) + (dot(tril, parts[1]) + dot(tril, parts[2]))
        suf = dot(triu, parts[0]) + (dot(triu, parts[1]) + dot(triu, parts[2]))
        gb_ref[0, rs, :] = jnp.where(is_gf_c, pre, jnp.where(is_gb_c, suf, gates[rs]))


def _proj_prep(x, mod, per_batch_mod, g, wc, wq, wz, wab, conv_w, dn_conv_w, alog_row, dtb_row, conv_dist, tm):
    bsz, seq, d = x.shape
    cwid, kwid = wc.shape[1] // 3, wq.shape[1] // 3
    n_tiles = seq // tm
    halo = max(conv_dist, SUBLANES)
    assert tm % halo == 0 and halo % SUBLANES == 0
    const2 = lambda b, i: (0, 0)
    tok = lambda b, i: (b, i, 0)
    x_spec, mod_spec = _token_specs(d, tm, per_batch_mod)
    prev_map = lambda b, i: (b, jnp.maximum(i * (tm // halo) - 1, 0), 0)
    next_map = lambda b, i: (b, jnp.minimum((i + 1) * (tm // halo), seq // halo - 1), 0)
    widths = (cwid, kwid, kwid, kwid, wab.shape[1], wz.shape[1])
    return pl.pallas_call(
        functools.partial(_proj_prep_kernel, conv_dist=conv_dist, n_tiles=n_tiles),
        grid=(bsz, n_tiles),
        in_specs=[x_spec, pl.BlockSpec((1, halo, d), prev_map), pl.BlockSpec((1, halo, d), next_map),
                  mod_spec, pl.BlockSpec((1, d), const2)]
                 + [pl.BlockSpec(w.shape, const2, pipeline_mode=pl.Buffered(1)) for w in (wc, wq, wz, wab)]
                 + [pl.BlockSpec(a.shape, const2) for a in (conv_w, dn_conv_w, alog_row, dtb_row)],
        out_specs=[pl.BlockSpec((1, tm, w), tok) for w in widths],
        out_shape=[jax.ShapeDtypeStruct((bsz, seq, w), F32) for w in widths],
        compiler_params=_cparams("parallel", "arbitrary"),
        name="mix_in_proj_prep",
    )(x, x, x, mod, g, wc, wq, wz, wab, conv_w, dn_conv_w, alog_row, dtb_row)


def _pair_block_diag(y16, keep_left, keep_right):
    return jnp.concatenate([y16 * keep_left, y16 * keep_right], axis=0)


INV_BASE = 8


def _pair_matmul_x3(lhs_list, rhs, keep_left, keep_right):
    d = lambda x, y: jnp.dot(x, y, preferred_element_type=F32)
    r_hi, r_lo = _split_bf16(rhs)
    r_hi = _pair_block_diag(r_hi, keep_left, keep_right)
    r_lo = _pair_block_diag(r_lo, keep_left, keep_right)
    parts = [_split_bf16(x) for x in lhs_list]
    l1 = jnp.concatenate([jnp.concatenate([hi, lo], axis=1) for hi, lo in parts], axis=0)
    l2 = jnp.concatenate([hi for hi, _ in parts], axis=0)
    res = d(l1, jnp.concatenate([r_hi, r_hi], axis=0)) + d(l2, r_lo)
    return [res[i * CHUNK:(i + 1) * CHUNK] for i in range(len(lhs_list))]


def _pair_inverses(a_list, row, col, eye_pair, keep_left, keep_right):
    mm = lambda lhs_list, rhs: _pair_matmul_x3(lhs_list, rhs, keep_left, keep_right)
    shift = INV_BASE.bit_length() - 1
    same_base = (row >> shift) == (col >> shift)
    ps = [jnp.where(same_base, -a, 0.0) for a in a_list]
    ts = [eye_pair + p for p in ps]
    n_sq = shift - 1
    for i in range(n_sq + 1):
        new_ps, new_ts = [], []
        for p, t in zip(ps, ts):
            if i == 0:
                new_ps.append(mm([p], p)[0])
                new_ts.append(t)
            elif i == n_sq:
                new_ps.append(p)
                new_ts.append(t + mm([t], p)[0])
            else:
                p2, tp = mm([p, t], p)
                new_ps.append(p2)
                new_ts.append(t + tp)
        ps, ts = new_ps, new_ts
    size = INV_BASE
    while size < CHUNK:
        s1 = size.bit_length() - 1
        off = ((row >> (s1 + 1)) == (col >> (s1 + 1))) & ((row >> s1) != (col >> s1))
        es = [jnp.where(off, a, 0.0) for a in a_list]
        xs = [mm([e], t)[0] for e, t in zip(es, ts)]
        ts = [t - mm([t], x)[0] for t, x in zip(ts, xs)]
        size *= 2
    return ts


def _chunk_prep_kernel(q_ref, k_ref, v_ref, g_ref,
                       uf_ref, ub_ref, wf_ref, wb_ref, qgf_ref, qgb_ref, kgf_ref, kgb_ref,
                       qkf_ref, qkb_ref, glf_ref, glb_ref):
    u_refs, w_refs, qg_refs = (uf_ref, ub_ref), (wf_ref, wb_ref), (qgf_ref, qgb_ref)
    kg_refs, qk_refs, gl_refs = (kgf_ref, kgb_ref), (qkf_ref, qkb_ref), (glf_ref, glb_ref)
    row = lax.broadcasted_iota(jnp.int32, (CHUNK, 2 * CHUNK), 0)
    lane = lax.broadcasted_iota(jnp.int32, (CHUNK, 2 * CHUNK), 1)
    col = lane & (CHUNK - 1)
    left = lane < CHUNK
    keep_left = jnp.where(left, 1.0, 0.0).astype(BF16)
    keep_right = jnp.where(left, 0.0, 1.0).astype(BF16)
    eye_pair = jnp.where(row == col, 1.0, 0.0)
    zeros16 = jnp.zeros((CHUNK, HEAD_DIM), BF16)
    dot = lambda x, y: jnp.dot(x, y, preferred_element_type=F32)

    chains, a_list = [], []
    for c in range(q_ref.shape[1] // CHUNK):
        rs = slice(c * CHUNK, (c + 1) * CHUNK)
        gates = g_ref[0, rs, :]
        for p in range(N_PAIRS):
            sl = slice(2 * p * HEAD_DIM, 2 * (p + 1) * HEAD_DIM)
            q2, k2, v2 = q_ref[0, rs, sl], k_ref[0, rs, sl], v_ref[0, rs, sl]
            k16 = k2.astype(BF16)
            k_bd = jnp.concatenate([jnp.concatenate([k16[:, :HEAD_DIM], zeros16], axis=1),
                                    jnp.concatenate([zeros16, k16[:, HEAD_DIM:]], axis=1)], axis=0)
            kq = jnp.concatenate([k16, q2.astype(BF16)], axis=0)
            gram = lax.dot_general(kq, k_bd, (((1,), (1,)), ((), ())), preferred_element_type=F32)
            for d in range(N_DIRS):
                gcol, bcol = 2 * d * N_HEADS, (2 * d + 1) * N_HEADS
                gc = [gates[:, gcol + 2 * p + e:gcol + 2 * p + e + 1] for e in range(2)]
                beta = [gates[:, bcol + 2 * p + e:bcol + 2 * p + e + 1] for e in range(2)]
                gc_pair = jnp.where(left, gc[0], gc[1])
                beta_pair = jnp.where(left, beta[0], beta[1])
                gr_pair = jnp.sum(jnp.where(row == col, gc_pair, 0.0), axis=0, keepdims=True)
                if d == 0:
                    incl, strict, last = row >= col, row > col, CHUNK - 1
                else:
                    incl, strict, last = row <= col, row < col, 0
                decay = jnp.where(incl, jnp.exp(jnp.where(incl, gc_pair - gr_pair, 0.0)), 0.0)
                a_list.append(jnp.where(strict, beta_pair * gram[:CHUNK] * decay, 0.0))
                qk_refs[d][0, rs, p * HEAD_DIM:(p + 1) * HEAD_DIM] = (
                    jnp.where(incl, gram[CHUNK:] * decay, 0.0).astype(BF16))
                chains.append((c, rs, p, d, q2, k2, v2, gc, beta, last))

    t_list = _pair_inverses(a_list, row, col, eye_pair, keep_left, keep_right)

    for (c, rs, p, d, q2, k2, v2, gc, beta, last), t in zip(chains, t_list):
        t16 = t.astype(BF16)
        for e in range(2):
            h = 2 * p + e
            sl = slice(e * HEAD_DIM, (e + 1) * HEAD_DIM)
            hs = slice(h * HEAD_DIM, (h + 1) * HEAD_DIM)
            eg = jnp.exp(gc[e])
            g_last = gc[e][last:last + 1, :]
            rhs = jnp.concatenate([v2[:, sl] * beta[e], k2[:, sl] * (beta[e] * eg)], axis=1).astype(BF16)
            zpad = jnp.zeros_like(rhs)
            uw = dot(t16, jnp.concatenate([rhs, zpad] if e == 0 else [zpad, rhs], axis=0))
            u_refs[d][0, rs, hs] = uw[:, :HEAD_DIM]
            w_refs[d][0, rs, hs] = uw[:, HEAD_DIM:].astype(BF16)
            qg_refs[d][0, rs, hs] = (q2[:, sl] * eg).astype(BF16)
            kg_refs[d][0, rs, hs] = (k2[:, sl] * jnp.exp(g_last - gc[e])).astype(BF16)
            gl_refs[d][0, c, h:h + 1, :] = jnp.broadcast_to(jnp.exp(g_last), (1, HEAD_DIM))


def _chunk_prep(q, k, v, gb):
    bsz, seq, kwid = q.shape
    n = seq // CHUNK
    step_chunks = min(PREP_CHUNKS, n)
    rows = step_chunks * CHUNK
    tok = lambda b, i: (b, i, 0)
    f32_full = jax.ShapeDtypeStruct(q.shape, F32)
    bf_full = jax.ShapeDtypeStruct(q.shape, BF16)
    bf_half = jax.ShapeDtypeStruct((bsz, seq, kwid // 2), BF16)
    gl = jax.ShapeDtypeStruct((bsz, n, N_HEADS, HEAD_DIM), F32)
    outs = [f32_full] * 2 + [bf_full] * 6 + [bf_half] * 2 + [gl] * 2
    out_specs = ([pl.BlockSpec((1, rows, kwid), tok)] * 8 + [pl.BlockSpec((1, rows, kwid // 2), tok)] * 2
                 + [pl.BlockSpec((1, step_chunks, N_HEADS, HEAD_DIM), lambda b, i: (b, i, 0, 0))] * 2)
    return pl.pallas_call(
        _chunk_prep_kernel,
        grid=(bsz, seq // rows),
        in_specs=[pl.BlockSpec((1, rows, kwid), tok)] * 3 + [pl.BlockSpec((1, rows, gb.shape[-1]), tok)],
        out_specs=out_specs,
        out_shape=outs,
        compiler_params=_cparams("parallel", "parallel"),
        name="delta_chunk_prep",
    )(q, k, v, gb)


def _scan_kernel(uf_ref, wf_ref, qgf_ref, kgf_ref, qkf_ref, glf_ref,
                 ub_ref, wb_ref, qgb_ref, kgb_ref, qkb_ref, glb_ref, s0f_ref, s0b_ref,
                 of_ref, ob_ref, sff_ref, sfb_ref, s_ref, *, n_chunks, has_s0):
    j = pl.program_id(1)
    n_hd = N_DIRS * N_HEADS

    @pl.when(j == 0)
    def _():
        for r in range(SCAN_ROWS):
            if has_s0:
                s_ref[r * n_hd:r * n_hd + N_HEADS] = s0f_ref[r]
                s_ref[r * n_hd + N_HEADS:(r + 1) * n_hd] = s0b_ref[r]
            else:
                s_ref[r * n_hd:(r + 1) * n_hd] = jnp.zeros((n_hd, HEAD_DIM, HEAD_DIM), F32)

    dot = lambda x, y: jnp.dot(x, y, preferred_element_type=F32)
    zeros16 = jnp.zeros((CHUNK, HEAD_DIM), BF16)
    ins = ((uf_ref, wf_ref, qgf_ref, kgf_ref, qkf_ref, glf_ref), (ub_ref, wb_ref, qgb_ref, kgb_ref, qkb_ref, glb_ref))
    outs = (of_ref, ob_ref)
    hds = [(r, d, h) for r in range(SCAN_ROWS) for d in range(N_DIRS) for h in range(N_HEADS)]
    hsl = lambda h: slice(h * HEAD_DIM, (h + 1) * HEAD_DIM)
    sidx = lambda r, d, h: r * n_hd + d * N_HEADS + h

    s_old, ws_qs = [], []
    for r, d, h in hds:
        s = s_ref[sidx(r, d, h)]
        s_old.append(s)
        lhs = jnp.concatenate([ins[d][1][r, :, hsl(h)], ins[d][2][r, :, hsl(h)]], axis=0)
        ws_qs.append(dot(lhs, s.astype(BF16)))
    v_new = []
    for i, (r, d, h) in enumerate(hds):
        vn = (ins[d][0][r, :, hsl(h)] - ws_qs[i][:CHUNK]).astype(BF16)
        v_new.append(vn)
        v_pad = jnp.concatenate([vn, zeros16] if h % 2 == 0 else [zeros16, vn], axis=0)
        qk_pair = ins[d][4][r, :, hsl(h // 2)]
        outs[d][r, :, hsl(h)] = ws_qs[i][CHUNK:] + dot(qk_pair, v_pad)
    for i, (r, d, h) in enumerate(hds):
        kg = ins[d][3][r, :, hsl(h)]
        upd = lax.dot_general(kg, v_new[i], (((0,), (0,)), ((), ())), preferred_element_type=F32)
        s_ref[sidx(r, d, h)] = s_old[i] * ins[d][5][r, 0, h:h + 1, :] + upd

    @pl.when(j == n_chunks - 1)
    def _():
        for r in range(SCAN_ROWS):
            sff_ref[r] = s_ref[r * n_hd:r * n_hd + N_HEADS]
            sfb_ref[r] = s_ref[r * n_hd + N_HEADS:(r + 1) * n_hd]


def _scan(prep, s0_f, s0_b):
    (u_f, u_b, w_f, w_b, qg_f, qg_b, kg_f, kg_b, qk_f, qk_b, gl_f, gl_b) = prep
    bsz, seq, kwid = u_f.shape
    assert bsz % SCAN_ROWS == 0
    n = seq // CHUNK
    has_s0 = s0_f is not None
    sblk = (SCAN_ROWS, N_HEADS, HEAD_DIM, HEAD_DIM)
    if not has_s0:
        s0_f = s0_b = jnp.zeros(sblk, F32)
    fwd = lambda b, j: (b, j, 0)
    bwd = lambda b, j: (b, n - 1 - j, 0)
    fwd4 = lambda b, j: (b, j, 0, 0)
    bwd4 = lambda b, j: (b, n - 1 - j, 0, 0)
    smap = (lambda b, j: (b, 0, 0, 0)) if has_s0 else (lambda b, j: (0, 0, 0, 0))

    def dir_specs(tok, tok4):
        return ([pl.BlockSpec((SCAN_ROWS, CHUNK, kwid), tok)] * 4
                + [pl.BlockSpec((SCAN_ROWS, CHUNK, kwid // 2), tok),
                   pl.BlockSpec((SCAN_ROWS, 1, N_HEADS, HEAD_DIM), tok4)])

    return pl.pallas_call(
        functools.partial(_scan_kernel, n_chunks=n, has_s0=has_s0),
        grid=(bsz // SCAN_ROWS, n),
        in_specs=dir_specs(fwd, fwd4) + dir_specs(bwd, bwd4) + [pl.BlockSpec(sblk, smap)] * 2,
        out_specs=[pl.BlockSpec((SCAN_ROWS, CHUNK, kwid), fwd),
                   pl.BlockSpec((SCAN_ROWS, CHUNK, kwid), bwd),
                   pl.BlockSpec(sblk, lambda b, j: (b, 0, 0, 0)),
                   pl.BlockSpec(sblk, lambda b, j: (b, 0, 0, 0))],
        out_shape=[jax.ShapeDtypeStruct(u_f.shape, F32)] * 2
                  + [jax.ShapeDtypeStruct((bsz,) + sblk[1:], F32)] * 2,
        scratch_shapes=[pltpu.VMEM((SCAN_ROWS * N_DIRS * N_HEADS, HEAD_DIM, HEAD_DIM), F32)],
        compiler_params=_cparams("parallel", "arbitrary"),
        name="delta_scan",
    )(u_f, w_f, qg_f, kg_f, qk_f, gl_f, u_b, w_b, qg_b, kg_b, qk_b, gl_b, s0_f, s0_b)


def _layer(x, mod, per_batch_mod, s0_f, s0_b, on_grid, lw, norm_final, final_norm, tm, fold):
    (g1, win1, wo1, gm, wc, wq, wz, wab, conv_w, dn_conv_w, alog_row, dtb_row, gn, wt, wbo,
     g2, win2, wo2) = lw
    bsz, seq, d = x.shape
    assert fold == 1 or not per_batch_mod
    folded = lambda a: a.reshape(bsz // fold, fold * seq, a.shape[-1])
    unfolded = lambda a: a.reshape(bsz, seq, a.shape[-1])
    x = _ffn1(folded(x), mod, per_batch_mod, g1, win1, wo1, tm)
    yc, q, k, v, gb, pz = _proj_prep(unfolded(x), mod, per_batch_mod, gm, wc, wq, wz, wab, conv_w, dn_conv_w,
                                     alog_row, dtb_row, GRID_W if on_grid else 1, min(tm, seq))
    o_f, o_b, s_f, s_b = _scan(_chunk_prep(q, k, v, gb), s0_f, s0_b)
    y = _mix_ffn2(x, mod, per_batch_mod, folded(yc), folded(o_f), folded(o_b), folded(pz), gn, wt, wbo,
                  g2, win2, wo2, norm_final, final_norm, tm)
    return unfolded(y), s_f, s_b


def kernel(x_prompt, x_sample, state_dn_fwd, state_dn_bwd, c, c_ctx, w_ada, b_ada, norm_ffn1, w_ffn1_in,
           w_ffn1_out, norm_mix, w_mix_in, conv_w, dn_conv_w, dn_a_log, dn_dt_bias, dn_norm, w_mix_out,
           norm_ffn2, w_ffn2_in, w_ffn2_out, norm_final):
    depth = w_ada.shape[0]
    d = x_prompt.shape[-1]
    n_lat = c.shape[0]
    cwid = conv_w.shape[-1]
    kwid = N_HEADS * HEAD_DIM
    n_gate = 4 * N_HEADS
    row = lambda a: a.reshape(1, -1)

    cvec = jnp.concatenate([c_ctx[None, :], c, jnp.zeros((16 - 1 - n_lat, d), F32)], axis=0)

    xp, xs = x_prompt, x_sample
    new_f, new_b = [], []
    for l in range(depth):
        mod = _modulation(cvec, w_ada[l], b_ada[l]).reshape(16, N_MOD, d)
        mod_ctx, mod_lat = mod[0:1], mod[1:1 + n_lat]

        win1, wo1 = w_ffn1_in[l].astype(BF16), w_ffn1_out[l].astype(BF16)
        win2, wo2 = w_ffn2_in[l].astype(BF16), w_ffn2_out[l].astype(BF16)
        wm = w_mix_in[l].astype(BF16)
        wc = wm[:, :3 * cwid]
        wq = wm[:, 3 * cwid:3 * cwid + 3 * kwid]
        wz = wm[:, 3 * cwid + 3 * kwid:3 * cwid + 4 * kwid]
        wab = jnp.pad(wm[:, 3 * cwid + 4 * kwid:], ((0, 0), (0, HEAD_DIM - n_gate)))
        zpad = jnp.zeros((N_HEADS,), F32)
        alog_row = jnp.concatenate([dn_a_log[l, 0], zpad, dn_a_log[l, 1], zpad,
                                    jnp.zeros((HEAD_DIM - n_gate,), F32)])[None, :]
        dtb_row = jnp.concatenate([dn_dt_bias[l, 0], zpad, dn_dt_bias[l, 1], zpad,
                                   jnp.zeros((HEAD_DIM - n_gate,), F32)])[None, :]
        wo = w_mix_out[l].astype(BF16)
        lw = (row(norm_ffn1[l]), win1, wo1, row(norm_mix[l]), wc, wq, wz, wab, conv_w[l], dn_conv_w[l],
              alog_row, dtb_row, row(dn_norm[l]), wo[:cwid], wo[cwid:], row(norm_ffn2[l]), win2, wo2)
        last = l == depth - 1
        xp, sf, sb = _layer(xp, mod_ctx, False, None, None, False, lw, row(norm_final), last,
                            TOKEN_TILE, TOKEN_TILE // xp.shape[1])
        new_f.append(sf)
        new_b.append(sb)
        xs, _, _ = _layer(xs, mod_lat, True, state_dn_fwd[:, l], state_dn_bwd[:, l], True, lw,
                          row(norm_final), last, TOKEN_TILE, 1)
    return xp, xs, jnp.stack(new_f, axis=1), jnp.stack(new_b, axis=1)
```

```python
import functools

import jax
import jax.numpy as jnp
from jax import lax
from jax.experimental import pallas as pl
from jax.experimental.pallas import tpu as pltpu

F32 = jnp.float32
BF16 = jnp.bfloat16

EPS = 1e-6
CHUNK = 64
GRID_W = 64
N_HEADS = 4
HEAD_DIM = 128
N_PAIRS = N_HEADS // 2
N_DIRS = 2
N_MOD = 9
SUBLANES = 8
FF_CHUNK = 256
TOKEN_TILE = 512
PREP_CHUNKS = 8
SCAN_ROWS = 8
VMEM_LIMIT = 56 * 1024 * 1024


def _cparams(*sem):
    return pltpu.CompilerParams(dimension_semantics=sem, vmem_limit_bytes=VMEM_LIMIT)


def _bdot(a, b):
    return jnp.dot(a.astype(BF16), b.astype(BF16), preferred_element_type=F32)


def _split_bf16(a):
    hi = a.astype(BF16)
    lo = (a - hi.astype(F32)).astype(BF16)
    return hi, lo


def _dot_x3(a, b):
    a_hi, a_lo = _split_bf16(a)
    b_hi, b_lo = _split_bf16(b)
    d = lambda x, y: jnp.dot(x, y, preferred_element_type=F32)
    return d(a_hi, b_hi) + (d(a_hi, b_lo) + d(a_lo, b_hi))


def _sigmoid(x):
    return 1.0 / (1.0 + jnp.exp(-x))


def _silu(x):
    return x * _sigmoid(x)


def _rms(x):
    return x * lax.rsqrt(jnp.mean(x * x, axis=-1, keepdims=True) + EPS)


def _mod_kernel(c_ref, w_ref, b_ref, o_ref):
    s = _silu(c_ref[...])
    o_ref[...] = _dot_x3(s, w_ref[...]) + b_ref[...]


def _modulation(cvec, w_ada, b_ada):
    rows, d = cvec.shape
    n = w_ada.shape[1]
    tn = 1536
    return pl.pallas_call(
        _mod_kernel,
        grid=(n // tn,),
        in_specs=[pl.BlockSpec((rows, d), lambda i: (0, 0)),
                  pl.BlockSpec((d, tn), lambda i: (0, i)),
                  pl.BlockSpec((1, tn), lambda i: (0, i))],
        out_specs=pl.BlockSpec((rows, tn), lambda i: (0, i)),
        out_shape=jax.ShapeDtypeStruct((rows, n), F32),
        compiler_params=_cparams("arbitrary"),
        name="modulation",
    )(cvec, w_ada, b_ada.reshape(1, n))


def _swiglu_residual(x, x_keep_ref, mod_ref, mod_row, g_ref, win_ref, wo_ref, h_ref, acc_ref):
    sh = mod_ref[0, mod_row:mod_row + 1, :]
    sc = mod_ref[0, mod_row + 1:mod_row + 2, :]
    gt = mod_ref[0, mod_row + 2:mod_row + 3, :]
    h_ref[...] = (_rms(x) * g_ref[...] * (1.0 + sc) + sh).astype(BF16)
    acc_ref[...] = jnp.zeros_like(acc_ref)
    ff = wo_ref.shape[0]
    for c in range(ff // FF_CHUNK):
        lo, hi = c * FF_CHUNK, (c + 1) * FF_CHUNK
        h = h_ref[...]
        a = jnp.dot(h, win_ref[:, lo:hi], preferred_element_type=F32)
        b = jnp.dot(h, win_ref[:, ff + lo:ff + hi], preferred_element_type=F32)
        act = (_silu(a) * b).astype(BF16)
        acc_ref[...] += jnp.dot(act, wo_ref[lo:hi, :], preferred_element_type=F32)
    return x_keep_ref[...] + 0.5 * gt * acc_ref[...]


def _ffn1_kernel(x_ref, mod_ref, g_ref, win_ref, wo_ref, o_ref, h_ref, acc_ref):
    o_ref[0] = _swiglu_residual(x_ref[0], x_ref.at[0], mod_ref, 0, g_ref, win_ref, wo_ref, h_ref, acc_ref)


def _mix_ffn2_kernel(x_ref, mod_ref, yc_ref, of_ref, ob_ref, z_ref, gn_ref, wt_ref, wbm_ref,
                     g_ref, win_ref, wo_ref, gfin_ref, o_ref, h_ref, acc_ref, x2_ref, *, final_norm):
    o = of_ref[0] + ob_ref[0]
    gate = _silu(z_ref[0])
    gn = gn_ref[...]
    ys = []
    for h in range(N_HEADS):
        sl = slice(h * HEAD_DIM, (h + 1) * HEAD_DIM)
        ys.append(_rms(o[:, sl]) * gn * gate[:, sl])
    y_dn = jnp.concatenate(ys, axis=1)
    m = _bdot(yc_ref[0], wt_ref[...]) + _bdot(y_dn, wbm_ref[...])
    x2_ref[...] = x_ref[0] + mod_ref[0, 5:6, :] * m
    y = _swiglu_residual(x2_ref[...], x2_ref, mod_ref, 6, g_ref, win_ref, wo_ref, h_ref, acc_ref)
    if final_norm:
        y = _rms(y) * gfin_ref[...]
    o_ref[0] = y


def _token_specs(d, tm, per_batch_mod):
    mod_map = (lambda b, i: (b, 0, 0)) if per_batch_mod else (lambda b, i: (0, 0, 0))
    return pl.BlockSpec((1, tm, d), lambda b, i: (b, i, 0)), pl.BlockSpec((1, N_MOD, d), mod_map)


def _ffn_weight_specs(win, wo):
    assert win.shape[1] == 2 * wo.shape[0] and wo.shape[0] % FF_CHUNK == 0
    const2 = lambda b, i: (0, 0)
    return [pl.BlockSpec(win.shape, const2, pipeline_mode=pl.Buffered(1)),
            pl.BlockSpec(wo.shape, const2, pipeline_mode=pl.Buffered(1))]


def _ffn1(x, mod, per_batch_mod, g, win, wo, tm):
    bsz, seq, d = x.shape
    const2 = lambda b, i: (0, 0)
    x_spec, mod_spec = _token_specs(d, tm, per_batch_mod)
    return pl.pallas_call(
        _ffn1_kernel,
        grid=(bsz, seq // tm),
        in_specs=[x_spec, mod_spec, pl.BlockSpec((1, d), const2)] + _ffn_weight_specs(win, wo),
        out_specs=x_spec,
        out_shape=jax.ShapeDtypeStruct(x.shape, F32),
        scratch_shapes=[pltpu.VMEM((tm, d), BF16), pltpu.VMEM((tm, d), F32)],
        compiler_params=_cparams("parallel", "arbitrary"),
        name="ffn1",
    )(x, mod, g, win, wo)


def _mix_ffn2(x, mod, per_batch_mod, yc, o_f, o_b, z, gn, wt, wbm, g, win, wo, gfin, final_norm, tm):
    bsz, seq, d = x.shape
    half = yc.shape[-1]
    const2 = lambda b, i: (0, 0)
    x_spec, mod_spec = _token_specs(d, tm, per_batch_mod)
    half_spec = pl.BlockSpec((1, tm, half), lambda b, i: (b, i, 0))
    return pl.pallas_call(
        functools.partial(_mix_ffn2_kernel, final_norm=final_norm),
        grid=(bsz, seq // tm),
        in_specs=[x_spec, mod_spec] + [half_spec] * 4
                 + [pl.BlockSpec((1, HEAD_DIM), const2),
                    pl.BlockSpec((half, d), const2, pipeline_mode=pl.Buffered(1)),
                    pl.BlockSpec((half, d), const2, pipeline_mode=pl.Buffered(1)),
                    pl.BlockSpec((1, d), const2)]
                 + _ffn_weight_specs(win, wo) + [pl.BlockSpec((1, d), const2)],
        out_specs=x_spec,
        out_shape=jax.ShapeDtypeStruct(x.shape, F32),
        scratch_shapes=[pltpu.VMEM((tm, d), BF16), pltpu.VMEM((tm, d), F32), pltpu.VMEM((tm, d), F32)],
        compiler_params=_cparams("parallel", "arbitrary"),
        name="mix_out_ffn2",
    )(x, mod, yc, o_f, o_b, z, gn, wt, wbm, g, win, wo, gfin)


def _neighbour_rows(ext, halo, tm, dist, first, last):
    prev = ext[halo - dist:halo - dist + tm]
    nxt = ext[halo + dist:halo + dist + tm]
    row = lax.broadcasted_iota(jnp.int32, prev.shape, 0)
    prev = jnp.where(first & (row < dist), 0.0, prev)
    nxt = jnp.where(last & (row >= tm - dist), 0.0, nxt)
    return prev, nxt


def _proj_prep_kernel(x_ref, xp_ref, xn_ref, mod_ref, g_ref, wc_ref, wq_ref, wz_ref, wab_ref,
                      cw_ref, dw_ref, alog_ref, dtb_ref,
                      yc_ref, q_ref, k_ref, v_ref, gb_ref, pz_ref, *, conv_dist, n_tiles):
    i = pl.program_id(1)
    first = i == 0
    last = i == n_tiles - 1
    tm = x_ref.shape[1]
    halo = xp_ref.shape[1]
    cwid = yc_ref.shape[-1]
    kwid = q_ref.shape[-1]
    sh = mod_ref[0, 3:4, :]
    sc = mod_ref[0, 4:5, :]
    norm = lambda xv: (_rms(xv) * g_ref[...] * (1.0 + sc) + sh).astype(BF16)
    h = norm(x_ref[0])
    h_ext = jnp.concatenate([norm(xp_ref[0]), h, norm(xn_ref[0])], axis=0)
    dot = lambda a, b: jnp.dot(a, b, preferred_element_type=F32)

    qh = SUBLANES
    pq_ext = dot(h_ext[halo - qh:halo + tm + qh], wq_ref[...])
    pch_ext = dot(h_ext, wc_ref[:, cwid:])
    pq = pq_ext[qh:qh + tm]
    qp, qn = _neighbour_rows(pq_ext, qh, tm, 1, first, last)
    dw = dw_ref[...]
    qkv = _silu(qp * dw[0:1] + pq * dw[1:2] + qn * dw[2:3])
    for hd in range(N_HEADS):
        sl = slice(hd * HEAD_DIM, (hd + 1) * HEAD_DIM)
        qv = qkv[:, sl]
        kv = qkv[:, kwid + hd * HEAD_DIM: kwid + (hd + 1) * HEAD_DIM]
        q_ref[0, :, sl] = qv * (lax.rsqrt(jnp.sum(qv * qv, axis=-1, keepdims=True) + EPS) * HEAD_DIM ** -0.5)
        k_ref[0, :, sl] = kv * lax.rsqrt(jnp.sum(kv * kv, axis=-1, keepdims=True) + EPS)
    v_ref[0] = qkv[:, 2 * kwid:]

    pcb = dot(h, wc_ref[:, :cwid])
    pz_ref[0] = dot(h, wz_ref[...])
    ab = dot(h, wab_ref[...])
    u_ext = pch_ext[:, :cwid] * pch_ext[:, cwid:]
    u = u_ext[halo:halo + tm]
    up, un = _neighbour_rows(u_ext, halo, tm, conv_dist, first, last)
    cw = cw_ref[...]
    yc_ref[0] = pcb * (up * cw[0:1] + u * cw[1:2] + un * cw[2:3])

    lane = lax.broadcasted_iota(jnp.int32, ab.shape, 1)
    is_g = (lane < N_HEADS) | ((lane >= 2 * N_HEADS) & (lane < 3 * N_HEADS))
    xg = ab + dtb_ref[...]
    softplus = jnp.maximum(xg, 0.0) + jnp.log1p(jnp.exp(-jnp.abs(xg)))
    gates = jnp.where(is_g, -jnp.exp(alog_ref[...]) * softplus, _sigmoid(ab))
    r = lax.broadcasted_iota(jnp.int32, (CHUNK, CHUNK), 0)
    c = lax.broadcasted_iota(jnp.int32, (CHUNK, CHUNK), 1)
    tril = (r >= c).astype(BF16)
    triu = (r <= c).astype(BF16)
    g_hi = gates.astype(BF16)
    rem = gates - g_hi.astype(F32)
    g_mid = rem.astype(BF16)
    g_lo = (rem - g_mid.astype(F32)).astype(BF16)
    lane_c = lax.broadcasted_iota(jnp.int32, (CHUNK, ab.shape[1]), 1)
    is_gf_c = lane_c < N_HEADS
    is_gb_c = (lane_c >= 2 * N_HEADS) & (lane_c < 3 * N_HEADS)
    for t in range(tm // CHUNK):
        rs = slice(t * CHUNK, (t + 1) * CHUNK)
        parts = (g_hi[rs], g_mid[rs], g_lo[rs])
        pre = dot(tril, parts[0]) + (dot(tril, parts[1]) + dot(tril, parts[2]))
        suf = dot(triu, parts[0]) + (dot(triu, parts[1]) + dot(triu, parts[2]))
        gb_ref[0, rs, :] = jnp.where(is_gf_c, pre, jnp.where(is_gb_c, suf, gates[rs]))


def _proj_prep(x, mod, per_batch_mod, g, wc, wq, wz, wab, conv_w, dn_conv_w, alog_row, dtb_row, conv_dist, tm):
    bsz, seq, d = x.shape
    cwid, kwid = wc.shape[1] // 3, wq.shape[1] // 3
    n_tiles = seq // tm
    halo = max(conv_dist, SUBLANES)
    assert tm % halo == 0 and halo % SUBLANES == 0
    const2 = lambda b, i: (0, 0)
    tok = lambda b, i: (b, i, 0)
    x_spec, mod_spec = _token_specs(d, tm, per_batch_mod)
    prev_map = lambda b, i: (b, jnp.maximum(i * (tm // halo) - 1, 0), 0)
    next_map = lambda b, i: (b, jnp.minimum((i + 1) * (tm // halo), seq // halo - 1), 0)
    widths = (cwid, kwid, kwid, kwid, wab.shape[1], wz.shape[1])
    return pl.pallas_call(
        functools.partial(_proj_prep_kernel, conv_dist=conv_dist, n_tiles=n_tiles),
        grid=(bsz, n_tiles),
        in_specs=[x_spec, pl.BlockSpec((1, halo, d), prev_map), pl.BlockSpec((1, halo, d), next_map),
                  mod_spec, pl.BlockSpec((1, d), const2)]
                 + [pl.BlockSpec(w.shape, const2, pipeline_mode=pl.Buffered(1)) for w in (wc, wq, wz, wab)]
                 + [pl.BlockSpec(a.shape, const2) for a in (conv_w, dn_conv_w, alog_row, dtb_row)],
        out_specs=[pl.BlockSpec((1, tm, w), tok) for w in widths],
        out_shape=[jax.ShapeDtypeStruct((bsz, seq, w), F32) for w in widths],
        compiler_params=_cparams("parallel", "arbitrary"),
        name="mix_in_proj_prep",
    )(x, x, x, mod, g, wc, wq, wz, wab, conv_w, dn_conv_w, alog_row, dtb_row)


def _pair_block_diag(y16, keep_left, keep_right):
    return jnp.concatenate([y16 * keep_left, y16 * keep_right], axis=0)


INV_BASE = 8


def _pair_matmul_x3(lhs_list, rhs, keep_left, keep_right):
    d = lambda x, y: jnp.dot(x, y, preferred_element_type=F32)
    r_hi, r_lo = _split_bf16(rhs)
    r_hi = _pair_block_diag(r_hi, keep_left, keep_right)
    r_lo = _pair_block_diag(r_lo, keep_left, keep_right)
    parts = [_split_bf16(x) for x in lhs_list]
    l1 = jnp.concatenate([jnp.concatenate([hi, lo], axis=1) for hi, lo in parts], axis=0)
    l2 = jnp.concatenate([hi for hi, _ in parts], axis=0)
    res = d(l1, jnp.concatenate([r_hi, r_hi], axis=0)) + d(l2, r_lo)
    return [res[i * CHUNK:(i + 1) * CHUNK] for i in range(len(lhs_list))]


def _pair_matmul(lhs, rhs, keep_left, keep_right):
    return jnp.dot(lhs.astype(BF16), _pair_block_diag(rhs.astype(BF16), keep_left, keep_right),
                   preferred_element_type=F32)


def _pair_inverses(a_list, row, col, eye_pair, keep_left, keep_right):
    mm = lambda lhs_list, rhs: _pair_matmul_x3(lhs_list, rhs, keep_left, keep_right)
    mm1 = lambda lhs, rhs: _pair_matmul(lhs, rhs, keep_left, keep_right)
    shift = INV_BASE.bit_length() - 1
    same_base = (row >> shift) == (col >> shift)
    ps = [jnp.where(same_base, -a, 0.0) for a in a_list]
    ts = [eye_pair + p for p in ps]
    n_sq = shift - 1
    for i in range(n_sq + 1):
        new_ps, new_ts = [], []
        for p, t in zip(ps, ts):
            if i == 0:
                new_ps.append(mm([p], p)[0])
                new_ts.append(t)
            elif i == n_sq:
                new_ps.append(p)
                new_ts.append(t + mm([t], p)[0])
            else:
                p2, tp = mm([p, t], p)
                new_ps.append(p2)
                new_ts.append(t + tp)
        ps, ts = new_ps, new_ts
    size = INV_BASE
    while size < CHUNK:
        s1 = size.bit_length() - 1
        off = ((row >> (s1 + 1)) == (col >> (s1 + 1))) & ((row >> s1) != (col >> s1))
        es = [jnp.where(off, a, 0.0) for a in a_list]
        xs = [mm1(e, t) for e, t in zip(es, ts)]
        ts = [t - mm1(t, x) for t, x in zip(ts, xs)]
        size *= 2
    return ts


def _chunk_prep_kernel(q_ref, k_ref, v_ref, g_ref,
                       uf_ref, ub_ref, wf_ref, wb_ref, qgf_ref, qgb_ref, kgf_ref, kgb_ref,
                       qkf_ref, qkb_ref, glf_ref, glb_ref):
    u_refs, w_refs, qg_refs = (uf_ref, ub_ref), (wf_ref, wb_ref), (qgf_ref, qgb_ref)
    kg_refs, qk_refs, gl_refs = (kgf_ref, kgb_ref), (qkf_ref, qkb_ref), (glf_ref, glb_ref)
    row = lax.broadcasted_iota(jnp.int32, (CHUNK, 2 * CHUNK), 0)
    lane = lax.broadcasted_iota(jnp.int32, (CHUNK, 2 * CHUNK), 1)
    col = lane & (CHUNK - 1)
    left = lane < CHUNK
    keep_left = jnp.where(left, 1.0, 0.0).astype(BF16)
    keep_right = jnp.where(left, 0.0, 1.0).astype(BF16)
    eye_pair = jnp.where(row == col, 1.0, 0.0)
    zeros16 = jnp.zeros((CHUNK, HEAD_DIM), BF16)
    dot = lambda x, y: jnp.dot(x, y, preferred_element_type=F32)

    chains, a_list = [], []
    for c in range(q_ref.shape[1] // CHUNK):
        rs = slice(c * CHUNK, (c + 1) * CHUNK)
        gates = g_ref[0, rs, :]
        for p in range(N_PAIRS):
            sl = slice(2 * p * HEAD_DIM, 2 * (p + 1) * HEAD_DIM)
            q2, k2, v2 = q_ref[0, rs, sl], k_ref[0, rs, sl], v_ref[0, rs, sl]
            k16 = k2.astype(BF16)
            k_bd = jnp.concatenate([jnp.concatenate([k16[:, :HEAD_DIM], zeros16], axis=1),
                                    jnp.concatenate([zeros16, k16[:, HEAD_DIM:]], axis=1)], axis=0)
            kq = jnp.concatenate([k16, q2.astype(BF16)], axis=0)
            gram = lax.dot_general(kq, k_bd, (((1,), (1,)), ((), ())), preferred_element_type=F32)
            for d in range(N_DIRS):
                gcol, bcol = 2 * d * N_HEADS, (2 * d + 1) * N_HEADS
                gc = [gates[:, gcol + 2 * p + e:gcol + 2 * p + e + 1] for e in range(2)]
                beta = [gates[:, bcol + 2 * p + e:bcol + 2 * p + e + 1] for e in range(2)]
                gc_pair = jnp.where(left, gc[0], gc[1])
                beta_pair = jnp.where(left, beta[0], beta[1])
                gr_pair = jnp.sum(jnp.where(row == col, gc_pair, 0.0), axis=0, keepdims=True)
                if d == 0:
                    incl, strict, last = row >= col, row > col, CHUNK - 1
                else:
                    incl, strict, last = row <= col, row < col, 0
                decay = jnp.where(incl, jnp.exp(jnp.where(incl, gc_pair - gr_pair, 0.0)), 0.0)
                a_list.append(jnp.where(strict, beta_pair * gram[:CHUNK] * decay, 0.0))
                qk_refs[d][0, rs, p * HEAD_DIM:(p + 1) * HEAD_DIM] = (
                    jnp.where(incl, gram[CHUNK:] * decay, 0.0).astype(BF16))
                chains.append((c, rs, p, d, q2, k2, v2, gc, beta, last))

    t_list = _pair_inverses(a_list, row, col, eye_pair, keep_left, keep_right)

    for (c, rs, p, d, q2, k2, v2, gc, beta, last), t in zip(chains, t_list):
        t16 = t.astype(BF16)
        for e in range(2):
            h = 2 * p + e
            sl = slice(e * HEAD_DIM, (e + 1) * HEAD_DIM)
            hs = slice(h * HEAD_DIM, (h + 1) * HEAD_DIM)
            eg = jnp.exp(gc[e])
            g_last = gc[e][last:last + 1, :]
            rhs = jnp.concatenate([v2[:, sl] * beta[e], k2[:, sl] * (beta[e] * eg)], axis=1).astype(BF16)
            zpad = jnp.zeros_like(rhs)
            uw = dot(t16, jnp.concatenate([rhs, zpad] if e == 0 else [zpad, rhs], axis=0))
            u_refs[d][0, rs, hs] = uw[:, :HEAD_DIM]
            w_refs[d][0, rs, hs] = uw[:, HEAD_DIM:].astype(BF16)
            qg_refs[d][0, rs, hs] = (q2[:, sl] * eg).astype(BF16)
            kg_refs[d][0, rs, hs] = (k2[:, sl] * jnp.exp(g_last - gc[e])).astype(BF16)
            gl_refs[d][0, c, h:h + 1, :] = jnp.broadcast_to(jnp.exp(g_last), (1, HEAD_DIM))


def _chunk_prep(q, k, v, gb):
    bsz, seq, kwid = q.shape
    n = seq // CHUNK
    step_chunks = min(PREP_CHUNKS, n)
    rows = step_chunks * CHUNK
    tok = lambda b, i: (b, i, 0)
    f32_full = jax.ShapeDtypeStruct(q.shape, F32)
    bf_full = jax.ShapeDtypeStruct(q.shape, BF16)
    bf_half = jax.ShapeDtypeStruct((bsz, seq, kwid // 2), BF16)
    gl = jax.ShapeDtypeStruct((bsz, n, N_HEADS, HEAD_DIM), F32)
    outs = [f32_full] * 2 + [bf_full] * 6 + [bf_half] * 2 + [gl] * 2
    out_specs = ([pl.BlockSpec((1, rows, kwid), tok)] * 8 + [pl.BlockSpec((1, rows, kwid // 2), tok)] * 2
                 + [pl.BlockSpec((1, step_chunks, N_HEADS, HEAD_DIM), lambda b, i: (b, i, 0, 0))] * 2)
    return pl.pallas_call(
        _chunk_prep_kernel,
        grid=(bsz, seq // rows),
        in_specs=[pl.BlockSpec((1, rows, kwid), tok)] * 3 + [pl.BlockSpec((1, rows, gb.shape[-1]), tok)],
        out_specs=out_specs,
        out_shape=outs,
        compiler_params=_cparams("parallel", "parallel"),
        name="delta_chunk_prep",
    )(q, k, v, gb)


def _scan_kernel(uf_ref, wf_ref, qgf_ref, kgf_ref, qkf_ref, glf_ref,
                 ub_ref, wb_ref, qgb_ref, kgb_ref, qkb_ref, glb_ref, s0f_ref, s0b_ref,
                 of_ref, ob_ref, sff_ref, sfb_ref, s_ref, *, n_chunks, has_s0):
    j = pl.program_id(1)
    n_hd = N_DIRS * N_HEADS

    @pl.when(j == 0)
    def _():
        for r in range(SCAN_ROWS):
            if has_s0:
                s_ref[r * n_hd:r * n_hd + N_HEADS] = s0f_ref[r]
                s_ref[r * n_hd + N_HEADS:(r + 1) * n_hd] = s0b_ref[r]
            else:
                s_ref[r * n_hd:(r + 1) * n_hd] = jnp.zeros((n_hd, HEAD_DIM, HEAD_DIM), F32)

    dot = lambda x, y: jnp.dot(x, y, preferred_element_type=F32)
    zeros16 = jnp.zeros((CHUNK, HEAD_DIM), BF16)
    ins = ((uf_ref, wf_ref, qgf_ref, kgf_ref, qkf_ref, glf_ref), (ub_ref, wb_ref, qgb_ref, kgb_ref, qkb_ref, glb_ref))
    outs = (of_ref, ob_ref)
    hds = [(r, d, h) for r in range(SCAN_ROWS) for d in range(N_DIRS) for h in range(N_HEADS)]
    hsl = lambda h: slice(h * HEAD_DIM, (h + 1) * HEAD_DIM)
    sidx = lambda r, d, h: r * n_hd + d * N_HEADS + h

    s_old, ws_qs = [], []
    for r, d, h in hds:
        s = s_ref[sidx(r, d, h)]
        s_old.append(s)
        lhs = jnp.concatenate([ins[d][1][r, :, hsl(h)], ins[d][2][r, :, hsl(h)]], axis=0)
        ws_qs.append(dot(lhs, s.astype(BF16)))
    v_new = []
    for i, (r, d, h) in enumerate(hds):
        vn = (ins[d][0][r, :, hsl(h)] - ws_qs[i][:CHUNK]).astype(BF16)
        v_new.append(vn)
        v_pad = jnp.concatenate([vn, zeros16] if h % 2 == 0 else [zeros16, vn], axis=0)
        qk_pair = ins[d][4][r, :, hsl(h // 2)]
        outs[d][r, :, hsl(h)] = ws_qs[i][CHUNK:] + dot(qk_pair, v_pad)
    for i, (r, d, h) in enumerate(hds):
        kg = ins[d][3][r, :, hsl(h)]
        upd = lax.dot_general(kg, v_new[i], (((0,), (0,)), ((), ())), preferred_element_type=F32)
        s_ref[sidx(r, d, h)] = s_old[i] * ins[d][5][r, 0, h:h + 1, :] + upd

    @pl.when(j == n_chunks - 1)
    def _():
        for r in range(SCAN_ROWS):
            sff_ref[r] = s_ref[r * n_hd:r * n_hd + N_HEADS]
            sfb_ref[r] = s_ref[r * n_hd + N_HEADS:(r + 1) * n_hd]


def _scan(prep, s0_f, s0_b):
    (u_f, u_b, w_f, w_b, qg_f, qg_b, kg_f, kg_b, qk_f, qk_b, gl_f, gl_b) = prep
    bsz, seq, kwid = u_f.shape
    assert bsz % SCAN_ROWS == 0
    n = seq // CHUNK
    has_s0 = s0_f is not None
    sblk = (SCAN_ROWS, N_HEADS, HEAD_DIM, HEAD_DIM)
    if not has_s0:
        s0_f = s0_b = jnp.zeros(sblk, F32)
    fwd = lambda b, j: (b, j, 0)
    bwd = lambda b, j: (b, n - 1 - j, 0)
    fwd4 = lambda b, j: (b, j, 0, 0)
    bwd4 = lambda b, j: (b, n - 1 - j, 0, 0)
    smap = (lambda b, j: (b, 0, 0, 0)) if has_s0 else (lambda b, j: (0, 0, 0, 0))

    def dir_specs(tok, tok4):
        return ([pl.BlockSpec((SCAN_ROWS, CHUNK, kwid), tok)] * 4
                + [pl.BlockSpec((SCAN_ROWS, CHUNK, kwid // 2), tok),
                   pl.BlockSpec((SCAN_ROWS, 1, N_HEADS, HEAD_DIM), tok4)])

    return pl.pallas_call(
        functools.partial(_scan_kernel, n_chunks=n, has_s0=has_s0),
        grid=(bsz // SCAN_ROWS, n),
        in_specs=dir_specs(fwd, fwd4) + dir_specs(bwd, bwd4) + [pl.BlockSpec(sblk, smap)] * 2,
        out_specs=[pl.BlockSpec((SCAN_ROWS, CHUNK, kwid), fwd),
                   pl.BlockSpec((SCAN_ROWS, CHUNK, kwid), bwd),
                   pl.BlockSpec(sblk, lambda b, j: (b, 0, 0, 0)),
                   pl.BlockSpec(sblk, lambda b, j: (b, 0, 0, 0))],
        out_shape=[jax.ShapeDtypeStruct(u_f.shape, F32)] * 2
                  + [jax.ShapeDtypeStruct((bsz,) + sblk[1:], F32)] * 2,
        scratch_shapes=[pltpu.VMEM((SCAN_ROWS * N_DIRS * N_HEADS, HEAD_DIM, HEAD_DIM), F32)],
        compiler_params=_cparams("parallel", "arbitrary"),
        name="delta_scan",
    )(u_f, w_f, qg_f, kg_f, qk_f, gl_f, u_b, w_b, qg_b, kg_b, qk_b, gl_b, s0_f, s0_b)


def _layer(x, mod, per_batch_mod, s0_f, s0_b, on_grid, lw, norm_final, final_norm, tm, fold):
    (g1, win1, wo1, gm, wc, wq, wz, wab, conv_w, dn_conv_w, alog_row, dtb_row, gn, wt, wbo,
     g2, win2, wo2) = lw
    bsz, seq, d = x.shape
    assert fold == 1 or not per_batch_mod
    folded = lambda a: a.reshape(bsz // fold, fold * seq, a.shape[-1])
    unfolded = lambda a: a.reshape(bsz, seq, a.shape[-1])
    x = _ffn1(folded(x), mod, per_batch_mod, g1, win1, wo1, tm)
    yc, q, k, v, gb, pz = _proj_prep(unfolded(x), mod, per_batch_mod, gm, wc, wq, wz, wab, conv_w, dn_conv_w,
                                     alog_row, dtb_row, GRID_W if on_grid else 1, min(tm, seq))
    o_f, o_b, s_f, s_b = _scan(_chunk_prep(q, k, v, gb), s0_f, s0_b)
    y = _mix_ffn2(x, mod, per_batch_mod, folded(yc), folded(o_f), folded(o_b), folded(pz), gn, wt, wbo,
                  g2, win2, wo2, norm_final, final_norm, tm)
    return unfolded(y), s_f, s_b


def kernel(x_prompt, x_sample, state_dn_fwd, state_dn_bwd, c, c_ctx, w_ada, b_ada, norm_ffn1, w_ffn1_in,
           w_ffn1_out, norm_mix, w_mix_in, conv_w, dn_conv_w, dn_a_log, dn_dt_bias, dn_norm, w_mix_out,
           norm_ffn2, w_ffn2_in, w_ffn2_out, norm_final):
    depth = w_ada.shape[0]
    d = x_prompt.shape[-1]
    n_lat = c.shape[0]
    cwid = conv_w.shape[-1]
    kwid = N_HEADS * HEAD_DIM
    n_gate = 4 * N_HEADS
    row = lambda a: a.reshape(1, -1)

    cvec = jnp.concatenate([c_ctx[None, :], c, jnp.zeros((16 - 1 - n_lat, d), F32)], axis=0)

    xp, xs = x_prompt, x_sample
    new_f, new_b = [], []
    for l in range(depth):
        mod = _modulation(cvec, w_ada[l], b_ada[l]).reshape(16, N_MOD, d)
        mod_ctx, mod_lat = mod[0:1], mod[1:1 + n_lat]

        win1, wo1 = w_ffn1_in[l].astype(BF16), w_ffn1_out[l].astype(BF16)
        win2, wo2 = w_ffn2_in[l].astype(BF16), w_ffn2_out[l].astype(BF16)
        wm = w_mix_in[l].astype(BF16)
        wc = wm[:, :3 * cwid]
        wq = wm[:, 3 * cwid:3 * cwid + 3 * kwid]
        wz = wm[:, 3 * cwid + 3 * kwid:3 * cwid + 4 * kwid]
        wab = jnp.pad(wm[:, 3 * cwid + 4 * kwid:], ((0, 0), (0, HEAD_DIM - n_gate)))
        zpad = jnp.zeros((N_HEADS,), F32)
        alog_row = jnp.concatenate([dn_a_log[l, 0], zpad, dn_a_log[l, 1], zpad,
                                    jnp.zeros((HEAD_DIM - n_gate,), F32)])[None, :]
        dtb_row = jnp.concatenate([dn_dt_bias[l, 0], zpad, dn_dt_bias[l, 1], zpad,
                                   jnp.zeros((HEAD_DIM - n_gate,), F32)])[None, :]
        wo = w_mix_out[l].astype(BF16)
        lw = (row(norm_ffn1[l]), win1, wo1, row(norm_mix[l]), wc, wq, wz, wab, conv_w[l], dn_conv_w[l],
              alog_row, dtb_row, row(dn_norm[l]), wo[:cwid], wo[cwid:], row(norm_ffn2[l]), win2, wo2)
        last = l == depth - 1
        xp, sf, sb = _layer(xp, mod_ctx, False, None, None, False, lw, row(norm_final), last,
                            TOKEN_TILE, TOKEN_TILE // xp.shape[1])
        new_f.append(sf)
        new_b.append(sb)
        xs, _, _ = _layer(xs, mod_lat, True, state_dn_fwd[:, l], state_dn_bwd[:, l], True, lw,
                          row(norm_final), last, TOKEN_TILE, 1)
    return xp, xs, jnp.stack(new_f, axis=1), jnp.stack(new_b, axis=1)
```

```python
import functools

import jax
import jax.numpy as jnp
from jax import lax
from jax.experimental import pallas as pl
from jax.experimental.pallas import tpu as pltpu

F32 = jnp.float32
BF16 = jnp.bfloat16

EPS = 1e-6
CHUNK = 64
GRID_W = 64
N_HEADS = 4
HEAD_DIM = 128
N_PAIRS = N_HEADS // 2
N_DIRS = 2
N_MOD = 9
SUBLANES = 8
FF_CHUNK = 256
TOKEN_TILE = 512
PREP_CHUNKS = 8
SCAN_ROWS = 8
VMEM_LIMIT = 56 * 1024 * 1024


def _cparams(*sem):
    return pltpu.CompilerParams(dimension_semantics=sem, vmem_limit_bytes=VMEM_LIMIT)


def _bdot(a, b):
    return jnp.dot(a.astype(BF16), b.astype(BF16), preferred_element_type=F32)


def _split_bf16(a):
    hi = a.astype(BF16)
    lo = (a - hi.astype(F32)).astype(BF16)
    return hi, lo


def _dot_x3(a, b):
    a_hi, a_lo = _split_bf16(a)
    b_hi, b_lo = _split_bf16(b)
    d = lambda x, y: jnp.dot(x, y, preferred_element_type=F32)
    return d(a_hi, b_hi) + (d(a_hi, b_lo) + d(a_lo, b_hi))


def _sigmoid(x):
    return 1.0 / (1.0 + jnp.exp(-x))


def _silu(x):
    return x * _sigmoid(x)


def _rms(x):
    return x * lax.rsqrt(jnp.mean(x * x, axis=-1, keepdims=True) + EPS)


def _mod_kernel(c_ref, w_ref, b_ref, o_ref):
    s = _silu(c_ref[...])
    o_ref[...] = _dot_x3(s, w_ref[...]) + b_ref[...]


def _modulation(cvec, w_ada, b_ada):
    rows, d = cvec.shape
    n = w_ada.shape[1]
    tn = 1536
    return pl.pallas_call(
        _mod_kernel,
        grid=(n // tn,),
        in_specs=[pl.BlockSpec((rows, d), lambda i: (0, 0)),
                  pl.BlockSpec((d, tn), lambda i: (0, i)),
                  pl.BlockSpec((1, tn), lambda i: (0, i))],
        out_specs=pl.BlockSpec((rows, tn), lambda i: (0, i)),
        out_shape=jax.ShapeDtypeStruct((rows, n), F32),
        compiler_params=_cparams("arbitrary"),
        name="modulation",
    )(cvec, w_ada, b_ada.reshape(1, n))


def _swiglu_residual(x, x_keep_ref, mod_ref, mod_row, g_ref, win_ref, wo_ref, h_ref, acc_ref):
    sh = mod_ref[0, mod_row:mod_row + 1, :]
    sc = mod_ref[0, mod_row + 1:mod_row + 2, :]
    gt = mod_ref[0, mod_row + 2:mod_row + 3, :]
    h_ref[...] = (_rms(x) * g_ref[...] * (1.0 + sc) + sh).astype(BF16)
    acc_ref[...] = jnp.zeros_like(acc_ref)
    ff = wo_ref.shape[0]
    for c in range(ff // FF_CHUNK):
        lo, hi = c * FF_CHUNK, (c + 1) * FF_CHUNK
        h = h_ref[...]
        a = jnp.dot(h, win_ref[:, lo:hi], preferred_element_type=F32)
        b = jnp.dot(h, win_ref[:, ff + lo:ff + hi], preferred_element_type=F32)
        act = (_silu(a) * b).astype(BF16)
        acc_ref[...] += jnp.dot(act, wo_ref[lo:hi, :], preferred_element_type=F32)
    return x_keep_ref[...] + 0.5 * gt * acc_ref[...]


def _ffn1_kernel(x_ref, mod_ref, g_ref, win_ref, wo_ref, o_ref, h_ref, acc_ref):
    o_ref[0] = _swiglu_residual(x_ref[0], x_ref.at[0], mod_ref, 0, g_ref, win_ref, wo_ref, h_ref, acc_ref)


def _mix_ffn2_kernel(x_ref, mod_ref, yc_ref, of_ref, ob_ref, z_ref, gn_ref, wt_ref, wbm_ref,
                     g_ref, win_ref, wo_ref, gfin_ref, o_ref, h_ref, acc_ref, x2_ref, *, final_norm):
    o = of_ref[0] + ob_ref[0]
    gate = _silu(z_ref[0])
    gn = gn_ref[...]
    ys = []
    for h in range(N_HEADS):
        sl = slice(h * HEAD_DIM, (h + 1) * HEAD_DIM)
        ys.append(_rms(o[:, sl]) * gn * gate[:, sl])
    y_dn = jnp.concatenate(ys, axis=1)
    m = _bdot(yc_ref[0], wt_ref[...]) + _bdot(y_dn, wbm_ref[...])
    x2_ref[...] = x_ref[0] + mod_ref[0, 5:6, :] * m
    y = _swiglu_residual(x2_ref[...], x2_ref, mod_ref, 6, g_ref, win_ref, wo_ref, h_ref, acc_ref)
    if final_norm:
        y = _rms(y) * gfin_ref[...]
    o_ref[0] = y


def _token_specs(d, tm, per_batch_mod):
    mod_map = (lambda b, i: (b, 0, 0)) if per_batch_mod else (lambda b, i: (0, 0, 0))
    return pl.BlockSpec((1, tm, d), lambda b, i: (b, i, 0)), pl.BlockSpec((1, N_MOD, d), mod_map)


def _ffn_weight_specs(win, wo):
    assert win.shape[1] == 2 * wo.shape[0] and wo.shape[0] % FF_CHUNK == 0
    const2 = lambda b, i: (0, 0)
    return [pl.BlockSpec(win.shape, const2, pipeline_mode=pl.Buffered(1)),
            pl.BlockSpec(wo.shape, const2, pipeline_mode=pl.Buffered(1))]


def _ffn1(x, mod, per_batch_mod, g, win, wo, tm):
    bsz, seq, d = x.shape
    const2 = lambda b, i: (0, 0)
    x_spec, mod_spec = _token_specs(d, tm, per_batch_mod)
    return pl.pallas_call(
        _ffn1_kernel,
        grid=(bsz, seq // tm),
        in_specs=[x_spec, mod_spec, pl.BlockSpec((1, d), const2)] + _ffn_weight_specs(win, wo),
        out_specs=x_spec,
        out_shape=jax.ShapeDtypeStruct(x.shape, F32),
        scratch_shapes=[pltpu.VMEM((tm, d), BF16), pltpu.VMEM((tm, d), F32)],
        compiler_params=_cparams("parallel", "arbitrary"),
        name="ffn1",
    )(x, mod, g, win, wo)


def _mix_ffn2(x, mod, per_batch_mod, yc, o_f, o_b, z, gn, wt, wbm, g, win, wo, gfin, final_norm, tm):
    bsz, seq, d = x.shape
    half = yc.shape[-1]
    const2 = lambda b, i: (0, 0)
    x_spec, mod_spec = _token_specs(d, tm, per_batch_mod)
    half_spec = pl.BlockSpec((1, tm, half), lambda b, i: (b, i, 0))
    return pl.pallas_call(
        functools.partial(_mix_ffn2_kernel, final_norm=final_norm),
        grid=(bsz, seq // tm),
        in_specs=[x_spec, mod_spec] + [half_spec] * 4
                 + [pl.BlockSpec((1, HEAD_DIM), const2),
                    pl.BlockSpec((half, d), const2, pipeline_mode=pl.Buffered(1)),
                    pl.BlockSpec((half, d), const2, pipeline_mode=pl.Buffered(1)),
                    pl.BlockSpec((1, d), const2)]
                 + _ffn_weight_specs(win, wo) + [pl.BlockSpec((1, d), const2)],
        out_specs=x_spec,
        out_shape=jax.ShapeDtypeStruct(x.shape, F32),
        scratch_shapes=[pltpu.VMEM((tm, d), BF16), pltpu.VMEM((tm, d), F32), pltpu.VMEM((tm, d), F32)],
        compiler_params=_cparams("parallel", "arbitrary"),
        name="mix_out_ffn2",
    )(x, mod, yc, o_f, o_b, z, gn, wt, wbm, g, win, wo, gfin)


def _neighbour_rows(ext, halo, tm, dist, first, last):
    prev = ext[halo - dist:halo - dist + tm]
    nxt = ext[halo + dist:halo + dist + tm]
    row = lax.broadcasted_iota(jnp.int32, prev.shape, 0)
    prev = jnp.where(first & (row < dist), 0.0, prev)
    nxt = jnp.where(last & (row >= tm - dist), 0.0, nxt)
    return prev, nxt


def _proj_prep_kernel(x_ref, xp_ref, xn_ref, mod_ref, g_ref, wc_ref, wq_ref, wz_ref, wab_ref,
                      cw_ref, dw_ref, alog_ref, dtb_ref,
                      yc_ref, q_ref, k_ref, v_ref, gb_ref, pz_ref, *, conv_dist, n_tiles):
    i = pl.program_id(1)
    first = i == 0
    last = i == n_tiles - 1
    tm = x_ref.shape[1]
    halo = xp_ref.shape[1]
    cwid = yc_ref.shape[-1]
    kwid = q_ref.shape[-1]
    sh = mod_ref[0, 3:4, :]
    sc = mod_ref[0, 4:5, :]
    norm = lambda xv: (_rms(xv) * g_ref[...] * (1.0 + sc) + sh).astype(BF16)
    h = norm(x_ref[0])
    h_ext = jnp.concatenate([norm(xp_ref[0]), h, norm(xn_ref[0])], axis=0)
    dot = lambda a, b: jnp.dot(a, b, preferred_element_type=F32)

    qh = SUBLANES
    pq_ext = dot(h_ext[halo - qh:halo + tm + qh], wq_ref[...])
    pch_ext = dot(h_ext, wc_ref[:, cwid:])
    pq = pq_ext[qh:qh + tm]
    qp, qn = _neighbour_rows(pq_ext, qh, tm, 1, first, last)
    dw = dw_ref[...]
    qkv = _silu(qp * dw[0:1] + pq * dw[1:2] + qn * dw[2:3])
    for hd in range(N_HEADS):
        sl = slice(hd * HEAD_DIM, (hd + 1) * HEAD_DIM)
        qv = qkv[:, sl]
        kv = qkv[:, kwid + hd * HEAD_DIM: kwid + (hd + 1) * HEAD_DIM]
        q_ref[0, :, sl] = qv * (lax.rsqrt(jnp.sum(qv * qv, axis=-1, keepdims=True) + EPS) * HEAD_DIM ** -0.5)
        k_ref[0, :, sl] = kv * lax.rsqrt(jnp.sum(kv * kv, axis=-1, keepdims=True) + EPS)
    v_ref[0] = qkv[:, 2 * kwid:]

    pcb = dot(h, wc_ref[:, :cwid])
    pz_ref[0] = dot(h, wz_ref[...])
    ab = dot(h, wab_ref[...])
    u_ext = pch_ext[:, :cwid] * pch_ext[:, cwid:]
    u = u_ext[halo:halo + tm]
    up, un = _neighbour_rows(u_ext, halo, tm, conv_dist, first, last)
    cw = cw_ref[...]
    yc_ref[0] = pcb * (up * cw[0:1] + u * cw[1:2] + un * cw[2:3])

    lane = lax.broadcasted_iota(jnp.int32, ab.shape, 1)
    is_g = (lane < N_HEADS) | ((lane >= 2 * N_HEADS) & (lane < 3 * N_HEADS))
    xg = ab + dtb_ref[...]
    softplus = jnp.maximum(xg, 0.0) + jnp.log1p(jnp.exp(-jnp.abs(xg)))
    gates = jnp.where(is_g, -jnp.exp(alog_ref[...]) * softplus, _sigmoid(ab))
    r = lax.broadcasted_iota(jnp.int32, (CHUNK, CHUNK), 0)
    c = lax.broadcasted_iota(jnp.int32, (CHUNK, CHUNK), 1)
    tril = (r >= c).astype(BF16)
    triu = (r <= c).astype(BF16)
    g_hi = gates.astype(BF16)
    rem = gates - g_hi.astype(F32)
    g_mid = rem.astype(BF16)
    g_lo = (rem - g_mid.astype(F32)).astype(BF16)
    lane_c = lax.broadcasted_iota(jnp.int32, (CHUNK, ab.shape[1]), 1)
    is_gf_c = lane_c < N_HEADS
    is_gb_c = (lane_c >= 2 * N_HEADS) & (lane_c < 3 * N_HEADS)
    for t in range(tm // CHUNK):
        rs = slice(t * CHUNK, (t + 1) * CHUNK)
        parts = (g_hi[rs], g_mid[rs], g_lo[rs])
        pre = dot(tril, parts[0]) + (dot(tril, parts[1]) + dot(tril, parts[2]))
        suf = dot(triu, parts[0]) + (dot(triu, parts[1]) + dot(triu, parts[2]))
        gb_ref[0, rs, :] = jnp.where(is_gf_c, pre, jnp.where(is_gb_c, suf, gates[rs]))


def _proj_prep(x, mod, per_batch_mod, g, wc, wq, wz, wab, conv_w, dn_conv_w, alog_row, dtb_row, conv_dist, tm):
    bsz, seq, d = x.shape
    cwid, kwid = wc.shape[1] // 3, wq.shape[1] // 3
    n_tiles = seq // tm
    halo = max(conv_dist, SUBLANES)
    assert tm % halo == 0 and halo % SUBLANES == 0
    const2 = lambda b, i: (0, 0)
    tok = lambda b, i: (b, i, 0)
    x_spec, mod_spec = _token_specs(d, tm, per_batch_mod)
    prev_map = lambda b, i: (b, jnp.maximum(i * (tm // halo) - 1, 0), 0)
    next_map = lambda b, i: (b, jnp.minimum((i + 1) * (tm // halo), seq // halo - 1), 0)
    widths = (cwid, kwid, kwid, kwid, wab.shape[1], wz.shape[1])
    return pl.pallas_call(
        functools.partial(_proj_prep_kernel, conv_dist=conv_dist, n_tiles=n_tiles),
        grid=(bsz, n_tiles),
        in_specs=[x_spec, pl.BlockSpec((1, halo, d), prev_map), pl.BlockSpec((1, halo, d), next_map),
                  mod_spec, pl.BlockSpec((1, d), const2)]
                 + [pl.BlockSpec(w.shape, const2, pipeline_mode=pl.Buffered(1)) for w in (wc, wq, wz, wab)]
                 + [pl.BlockSpec(a.shape, const2) for a in (conv_w, dn_conv_w, alog_row, dtb_row)],
        out_specs=[pl.BlockSpec((1, tm, w), tok) for w in widths],
        out_shape=[jax.ShapeDtypeStruct((bsz, seq, w), F32) for w in widths],
        compiler_params=_cparams("parallel", "arbitrary"),
        name="mix_in_proj_prep",
    )(x, x, x, mod, g, wc, wq, wz, wab, conv_w, dn_conv_w, alog_row, dtb_row)


def _pair_block_diag(y16, keep_left, keep_right):
    return jnp.concatenate([y16 * keep_left, y16 * keep_right], axis=0)


INV_BASE = 4


def _pair_matmul_x3(lhs_list, rhs, keep_left, keep_right):
    d = lambda x, y: jnp.dot(x, y, preferred_element_type=F32)
    r_hi, r_lo = _split_bf16(rhs)
    r_hi = _pair_block_diag(r_hi, keep_left, keep_right)
    r_lo = _pair_block_diag(r_lo, keep_left, keep_right)
    parts = [_split_bf16(x) for x in lhs_list]
    l1 = jnp.concatenate([jnp.concatenate([hi, lo], axis=1) for hi, lo in parts], axis=0)
    l2 = jnp.concatenate([hi for hi, _ in parts], axis=0)
    res = d(l1, jnp.concatenate([r_hi, r_hi], axis=0)) + d(l2, r_lo)
    return [res[i * CHUNK:(i + 1) * CHUNK] for i in range(len(lhs_list))]


def _pair_matmul(lhs, rhs, keep_left, keep_right):
    return jnp.dot(lhs.astype(BF16), _pair_block_diag(rhs.astype(BF16), keep_left, keep_right),
                   preferred_element_type=F32)


def _pair_inverses(a_list, row, col, eye_pair, keep_left, keep_right):
    mm = lambda lhs_list, rhs: _pair_matmul_x3(lhs_list, rhs, keep_left, keep_right)
    mm1 = lambda lhs, rhs: _pair_matmul(lhs, rhs, keep_left, keep_right)
    shift = INV_BASE.bit_length() - 1
    same_base = (row >> shift) == (col >> shift)
    ps = [jnp.where(same_base, -a, 0.0) for a in a_list]
    ts = [eye_pair + p for p in ps]
    n_sq = shift - 1
    for i in range(n_sq + 1):
        new_ps, new_ts = [], []
        for p, t in zip(ps, ts):
            if i == 0:
                new_ps.append(mm([p], p)[0])
                new_ts.append(t)
            elif i == n_sq:
                new_ps.append(p)
                new_ts.append(t + mm([t], p)[0])
            else:
                p2, tp = mm([p, t], p)
                new_ps.append(p2)
                new_ts.append(t + tp)
        ps, ts = new_ps, new_ts
    size = INV_BASE
    while size < CHUNK:
        s1 = size.bit_length() - 1
        off = ((row >> (s1 + 1)) == (col >> (s1 + 1))) & ((row >> s1) != (col >> s1))
        es = [jnp.where(off, a, 0.0) for a in a_list]
        xs = [mm1(e, t) for e, t in zip(es, ts)]
        ts = [t - mm1(t, x) for t, x in zip(ts, xs)]
        size *= 2
    return ts


def _chunk_prep_kernel(q_ref, k_ref, v_ref, g_ref,
                       uf_ref, ub_ref, wf_ref, wb_ref, qgf_ref, qgb_ref, kgf_ref, kgb_ref,
                       qkf_ref, qkb_ref, glf_ref, glb_ref):
    u_refs, w_refs, qg_refs = (uf_ref, ub_ref), (wf_ref, wb_ref), (qgf_ref, qgb_ref)
    kg_refs, qk_refs, gl_refs = (kgf_ref, kgb_ref), (qkf_ref, qkb_ref), (glf_ref, glb_ref)
    row = lax.broadcasted_iota(jnp.int32, (CHUNK, 2 * CHUNK), 0)
    lane = lax.broadcasted_iota(jnp.int32, (CHUNK, 2 * CHUNK), 1)
    col = lane & (CHUNK - 1)
    left = lane < CHUNK
    keep_left = jnp.where(left, 1.0, 0.0).astype(BF16)
    keep_right = jnp.where(left, 0.0, 1.0).astype(BF16)
    eye_pair = jnp.where(row == col, 1.0, 0.0)
    zeros16 = jnp.zeros((CHUNK, HEAD_DIM), BF16)
    dot = lambda x, y: jnp.dot(x, y, preferred_element_type=F32)

    chains, a_list = [], []
    for c in range(q_ref.shape[1] // CHUNK):
        rs = slice(c * CHUNK, (c + 1) * CHUNK)
        gates = g_ref[0, rs, :]
        for p in range(N_PAIRS):
            sl = slice(2 * p * HEAD_DIM, 2 * (p + 1) * HEAD_DIM)
            q2, k2, v2 = q_ref[0, rs, sl], k_ref[0, rs, sl], v_ref[0, rs, sl]
            k16 = k2.astype(BF16)
            k_bd = jnp.concatenate([jnp.concatenate([k16[:, :HEAD_DIM], zeros16], axis=1),
                                    jnp.concatenate([zeros16, k16[:, HEAD_DIM:]], axis=1)], axis=0)
            kq = jnp.concatenate([k16, q2.astype(BF16)], axis=0)
            gram = lax.dot_general(kq, k_bd, (((1,), (1,)), ((), ())), preferred_element_type=F32)
            for d in range(N_DIRS):
                gcol, bcol = 2 * d * N_HEADS, (2 * d + 1) * N_HEADS
                gc = [gates[:, gcol + 2 * p + e:gcol + 2 * p + e + 1] for e in range(2)]
                beta = [gates[:, bcol + 2 * p + e:bcol + 2 * p + e + 1] for e in range(2)]
                gc_pair = jnp.where(left, gc[0], gc[1])
                beta_pair = jnp.where(left, beta[0], beta[1])
                gr_pair = jnp.sum(jnp.where(row == col, gc_pair, 0.0), axis=0, keepdims=True)
                if d == 0:
                    incl, strict, last = row >= col, row > col, CHUNK - 1
                else:
                    incl, strict, last = row <= col, row < col, 0
                decay = jnp.where(incl, jnp.exp(jnp.where(incl, gc_pair - gr_pair, 0.0)), 0.0)
                a_list.append(jnp.where(strict, beta_pair * gram[:CHUNK] * decay, 0.0))
                qk_refs[d][0, rs, p * HEAD_DIM:(p + 1) * HEAD_DIM] = (
                    jnp.where(incl, gram[CHUNK:] * decay, 0.0).astype(BF16))
                chains.append((c, rs, p, d, q2, k2, v2, gc, beta, last))

    t_list = _pair_inverses(a_list, row, col, eye_pair, keep_left, keep_right)

    for (c, rs, p, d, q2, k2, v2, gc, beta, last), t in zip(chains, t_list):
        t16 = t.astype(BF16)
        for e in range(2):
            h = 2 * p + e
            sl = slice(e * HEAD_DIM, (e + 1) * HEAD_DIM)
            hs = slice(h * HEAD_DIM, (h + 1) * HEAD_DIM)
            eg = jnp.exp(gc[e])
            g_last = gc[e][last:last + 1, :]
            rhs = jnp.concatenate([v2[:, sl] * beta[e], k2[:, sl] * (beta[e] * eg)], axis=1).astype(BF16)
            zpad = jnp.zeros_like(rhs)
            uw = dot(t16, jnp.concatenate([rhs, zpad] if e == 0 else [zpad, rhs], axis=0))
            u_refs[d][0, rs, hs] = uw[:, :HEAD_DIM]
            w_refs[d][0, rs, hs] = uw[:, HEAD_DIM:].astype(BF16)
            qg_refs[d][0, rs, hs] = (q2[:, sl] * eg).astype(BF16)
            kg_refs[d][0, rs, hs] = (k2[:, sl] * jnp.exp(g_last - gc[e])).astype(BF16)
            gl_refs[d][0, c, h:h + 1, :] = jnp.broadcast_to(jnp.exp(g_last), (1, HEAD_DIM))


def _chunk_prep(q, k, v, gb):
    bsz, seq, kwid = q.shape
    n = seq // CHUNK
    step_chunks = min(PREP_CHUNKS, n)
    rows = step_chunks * CHUNK
    tok = lambda b, i: (b, i, 0)
    f32_full = jax.ShapeDtypeStruct(q.shape, F32)
    bf_full = jax.ShapeDtypeStruct(q.shape, BF16)
    bf_half = jax.ShapeDtypeStruct((bsz, seq, kwid // 2), BF16)
    gl = jax.ShapeDtypeStruct((bsz, n, N_HEADS, HEAD_DIM), F32)
    outs = [f32_full] * 2 + [bf_full] * 6 + [bf_half] * 2 + [gl] * 2
    out_specs = ([pl.BlockSpec((1, rows, kwid), tok)] * 8 + [pl.BlockSpec((1, rows, kwid // 2), tok)] * 2
                 + [pl.BlockSpec((1, step_chunks, N_HEADS, HEAD_DIM), lambda b, i: (b, i, 0, 0))] * 2)
    return pl.pallas_call(
        _chunk_prep_kernel,
        grid=(bsz, seq // rows),
        in_specs=[pl.BlockSpec((1, rows, kwid), tok)] * 3 + [pl.BlockSpec((1, rows, gb.shape[-1]), tok)],
        out_specs=out_specs,
        out_shape=outs,
        compiler_params=_cparams("parallel", "parallel"),
        name="delta_chunk_prep",
    )(q, k, v, gb)


def _scan_kernel(uf_ref, wf_ref, qgf_ref, kgf_ref, qkf_ref, glf_ref,
                 ub_ref, wb_ref, qgb_ref, kgb_ref, qkb_ref, glb_ref, s0f_ref, s0b_ref,
                 of_ref, ob_ref, sff_ref, sfb_ref, s_ref, *, n_chunks, has_s0):
    j = pl.program_id(1)
    n_hd = N_DIRS * N_HEADS

    @pl.when(j == 0)
    def _():
        for r in range(SCAN_ROWS):
            if has_s0:
                s_ref[r * n_hd:r * n_hd + N_HEADS] = s0f_ref[r]
                s_ref[r * n_hd + N_HEADS:(r + 1) * n_hd] = s0b_ref[r]
            else:
                s_ref[r * n_hd:(r + 1) * n_hd] = jnp.zeros((n_hd, HEAD_DIM, HEAD_DIM), F32)

    dot = lambda x, y: jnp.dot(x, y, preferred_element_type=F32)
    zeros16 = jnp.zeros((CHUNK, HEAD_DIM), BF16)
    ins = ((uf_ref, wf_ref, qgf_ref, kgf_ref, qkf_ref, glf_ref), (ub_ref, wb_ref, qgb_ref, kgb_ref, qkb_ref, glb_ref))
    outs = (of_ref, ob_ref)
    hds = [(r, d, h) for r in range(SCAN_ROWS) for d in range(N_DIRS) for h in range(N_HEADS)]
    hsl = lambda h: slice(h * HEAD_DIM, (h + 1) * HEAD_DIM)
    sidx = lambda r, d, h: r * n_hd + d * N_HEADS + h

    s_old, ws_qs = [], []
    for r, d, h in hds:
        s = s_ref[sidx(r, d, h)]
        s_old.append(s)
        lhs = jnp.concatenate([ins[d][1][r, :, hsl(h)], ins[d][2][r, :, hsl(h)]], axis=0)
        ws_qs.append(dot(lhs, s.astype(BF16)))
    v_new = []
    for i, (r, d, h) in enumerate(hds):
        vn = (ins[d][0][r, :, hsl(h)] - ws_qs[i][:CHUNK]).astype(BF16)
        v_new.append(vn)
        v_pad = jnp.concatenate([vn, zeros16] if h % 2 == 0 else [zeros16, vn], axis=0)
        qk_pair = ins[d][4][r, :, hsl(h // 2)]
        outs[d][r, :, hsl(h)] = ws_qs[i][CHUNK:] + dot(qk_pair, v_pad)
    for i, (r, d, h) in enumerate(hds):
        kg = ins[d][3][r, :, hsl(h)]
        upd = lax.dot_general(kg, v_new[i], (((0,), (0,)), ((), ())), preferred_element_type=F32)
        s_ref[sidx(r, d, h)] = s_old[i] * ins[d][5][r, 0, h:h + 1, :] + upd

    @pl.when(j == n_chunks - 1)
    def _():
        for r in range(SCAN_ROWS):
            sff_ref[r] = s_ref[r * n_hd:r * n_hd + N_HEADS]
            sfb_ref[r] = s_ref[r * n_hd + N_HEADS:(r + 1) * n_hd]


def _scan(prep, s0_f, s0_b):
    (u_f, u_b, w_f, w_b, qg_f, qg_b, kg_f, kg_b, qk_f, qk_b, gl_f, gl_b) = prep
    bsz, seq, kwid = u_f.shape
    assert bsz % SCAN_ROWS == 0
    n = seq // CHUNK
    has_s0 = s0_f is not None
    sblk = (SCAN_ROWS, N_HEADS, HEAD_DIM, HEAD_DIM)
    if not has_s0:
        s0_f = s0_b = jnp.zeros(sblk, F32)
    fwd = lambda b, j: (b, j, 0)
    bwd = lambda b, j: (b, n - 1 - j, 0)
    fwd4 = lambda b, j: (b, j, 0, 0)
    bwd4 = lambda b, j: (b, n - 1 - j, 0, 0)
    smap = (lambda b, j: (b, 0, 0, 0)) if has_s0 else (lambda b, j: (0, 0, 0, 0))

    def dir_specs(tok, tok4):
        return ([pl.BlockSpec((SCAN_ROWS, CHUNK, kwid), tok)] * 4
                + [pl.BlockSpec((SCAN_ROWS, CHUNK, kwid // 2), tok),
                   pl.BlockSpec((SCAN_ROWS, 1, N_HEADS, HEAD_DIM), tok4)])

    return pl.pallas_call(
        functools.partial(_scan_kernel, n_chunks=n, has_s0=has_s0),
        grid=(bsz // SCAN_ROWS, n),
        in_specs=dir_specs(fwd, fwd4) + dir_specs(bwd, bwd4) + [pl.BlockSpec(sblk, smap)] * 2,
        out_specs=[pl.BlockSpec((SCAN_ROWS, CHUNK, kwid), fwd),
                   pl.BlockSpec((SCAN_ROWS, CHUNK, kwid), bwd),
                   pl.BlockSpec(sblk, lambda b, j: (b, 0, 0, 0)),
                   pl.BlockSpec(sblk, lambda b, j: (b, 0, 0, 0))],
        out_shape=[jax.ShapeDtypeStruct(u_f.shape, F32)] * 2
                  + [jax.ShapeDtypeStruct((bsz,) + sblk[1:], F32)] * 2,
        scratch_shapes=[pltpu.VMEM((SCAN_ROWS * N_DIRS * N_HEADS, HEAD_DIM, HEAD_DIM), F32)],
        compiler_params=_cparams("parallel", "arbitrary"),
        name="delta_scan",
    )(u_f, w_f, qg_f, kg_f, qk_f, gl_f, u_b, w_b, qg_b, kg_b, qk_b, gl_b, s0_f, s0_b)


def _layer(x, mod, per_batch_mod, s0_f, s0_b, on_grid, lw, norm_final, final_norm, tm, fold):
    (g1, win1, wo1, gm, wc, wq, wz, wab, conv_w, dn_conv_w, alog_row, dtb_row, gn, wt, wbo,
     g2, win2, wo2) = lw
    bsz, seq, d = x.shape
    assert fold == 1 or not per_batch_mod
    folded = lambda a: a.reshape(bsz // fold, fold * seq, a.shape[-1])
    unfolded = lambda a: a.reshape(bsz, seq, a.shape[-1])
    x = _ffn1(folded(x), mod, per_batch_mod, g1, win1, wo1, tm)
    yc, q, k, v, gb, pz = _proj_prep(unfolded(x), mod, per_batch_mod, gm, wc, wq, wz, wab, conv_w, dn_conv_w,
                                     alog_row, dtb_row, GRID_W if on_grid else 1, min(tm, seq))
    o_f, o_b, s_f, s_b = _scan(_chunk_prep(q, k, v, gb), s0_f, s0_b)
    y = _mix_ffn2(x, mod, per_batch_mod, folded(yc), folded(o_f), folded(o_b), folded(pz), gn, wt, wbo,
                  g2, win2, wo2, norm_final, final_norm, tm)
    return unfolded(y), s_f, s_b


def kernel(x_prompt, x_sample, state_dn_fwd, state_dn_bwd, c, c_ctx, w_ada, b_ada, norm_ffn1, w_ffn1_in,
           w_ffn1_out, norm_mix, w_mix_in, conv_w, dn_conv_w, dn_a_log, dn_dt_bias, dn_norm, w_mix_out,
           norm_ffn2, w_ffn2_in, w_ffn2_out, norm_final):
    depth = w_ada.shape[0]
    d = x_prompt.shape[-1]
    n_lat = c.shape[0]
    cwid = conv_w.shape[-1]
    kwid = N_HEADS * HEAD_DIM
    n_gate = 4 * N_HEADS
    row = lambda a: a.reshape(1, -1)

    cvec = jnp.concatenate([c_ctx[None, :], c, jnp.zeros((16 - 1 - n_lat, d), F32)], axis=0)

    xp, xs = x_prompt, x_sample
    new_f, new_b = [], []
    for l in range(depth):
        mod = _modulation(cvec, w_ada[l], b_ada[l]).reshape(16, N_MOD, d)
        mod_ctx, mod_lat = mod[0:1], mod[1:1 + n_lat]

        win1, wo1 = w_ffn1_in[l].astype(BF16), w_ffn1_out[l].astype(BF16)
        win2, wo2 = w_ffn2_in[l].astype(BF16), w_ffn2_out[l].astype(BF16)
        wm = w_mix_in[l].astype(BF16)
        wc = wm[:, :3 * cwid]
        wq = wm[:, 3 * cwid:3 * cwid + 3 * kwid]
        wz = wm[:, 3 * cwid + 3 * kwid:3 * cwid + 4 * kwid]
        wab = jnp.pad(wm[:, 3 * cwid + 4 * kwid:], ((0, 0), (0, HEAD_DIM - n_gate)))
        zpad = jnp.zeros((N_HEADS,), F32)
        alog_row = jnp.concatenate([dn_a_log[l, 0], zpad, dn_a_log[l, 1], zpad,
                                    jnp.zeros((HEAD_DIM - n_gate,), F32)])[None, :]
        dtb_row = jnp.concatenate([dn_dt_bias[l, 0], zpad, dn_dt_bias[l, 1], zpad,
                                   jnp.zeros((HEAD_DIM - n_gate,), F32)])[None, :]
        wo = w_mix_out[l].astype(BF16)
        lw = (row(norm_ffn1[l]), win1, wo1, row(norm_mix[l]), wc, wq, wz, wab, conv_w[l], dn_conv_w[l],
              alog_row, dtb_row, row(dn_norm[l]), wo[:cwid], wo[cwid:], row(norm_ffn2[l]), win2, wo2)
        last = l == depth - 1
        xp, sf, sb = _layer(xp, mod_ctx, False, None, None, False, lw, row(norm_final), last,
                            TOKEN_TILE, TOKEN_TILE // xp.shape[1])
        new_f.append(sf)
        new_b.append(sb)
        xs, _, _ = _layer(xs, mod_lat, True, state_dn_fwd[:, l], state_dn_bwd[:, l], True, lw,
                          row(norm_final), last, TOKEN_TILE, 1)
    return xp, xs, jnp.stack(new_f, axis=1), jnp.stack(new_b, axis=1)
```

```python
import functools

import jax
import jax.numpy as jnp
from jax import lax
from jax.experimental import pallas as pl
from jax.experimental.pallas import tpu as pltpu

F32 = jnp.float32
BF16 = jnp.bfloat16

EPS = 1e-6
CHUNK = 64
GRID_W = 64
N_HEADS = 4
HEAD_DIM = 128
N_PAIRS = N_HEADS // 2
N_DIRS = 2
N_MOD = 9
SUBLANES = 8
FF_CHUNK = 256
TOKEN_TILE = 512
FFN1_TILE = 1024
PREP_CHUNKS = 8
SCAN_ROWS = 8
VMEM_LIMIT = 56 * 1024 * 1024


def _cparams(*sem):
    return pltpu.CompilerParams(dimension_semantics=sem, vmem_limit_bytes=VMEM_LIMIT)


def _bdot(a, b):
    return jnp.dot(a.astype(BF16), b.astype(BF16), preferred_element_type=F32)


def _split_bf16(a):
    hi = a.astype(BF16)
    lo = (a - hi.astype(F32)).astype(BF16)
    return hi, lo


def _dot_x3(a, b):
    a_hi, a_lo = _split_bf16(a)
    b_hi, b_lo = _split_bf16(b)
    d = lambda x, y: jnp.dot(x, y, preferred_element_type=F32)
    return d(a_hi, b_hi) + (d(a_hi, b_lo) + d(a_lo, b_hi))


def _sigmoid(x):
    return 1.0 / (1.0 + jnp.exp(-x))


def _silu(x):
    return x * _sigmoid(x)


def _rms(x):
    return x * lax.rsqrt(jnp.mean(x * x, axis=-1, keepdims=True) + EPS)


def _mod_kernel(c_ref, w_ref, b_ref, o_ref):
    s = _silu(c_ref[...])
    o_ref[...] = _dot_x3(s, w_ref[...]) + b_ref[...]


def _modulation(cvec, w_ada, b_ada):
    rows, d = cvec.shape
    n = w_ada.shape[1]
    tn = 1536
    return pl.pallas_call(
        _mod_kernel,
        grid=(n // tn,),
        in_specs=[pl.BlockSpec((rows, d), lambda i: (0, 0)),
                  pl.BlockSpec((d, tn), lambda i: (0, i)),
                  pl.BlockSpec((1, tn), lambda i: (0, i))],
        out_specs=pl.BlockSpec((rows, tn), lambda i: (0, i)),
        out_shape=jax.ShapeDtypeStruct((rows, n), F32),
        compiler_params=_cparams("arbitrary"),
        name="modulation",
    )(cvec, w_ada, b_ada.reshape(1, n))


def _swiglu_residual(x, x_keep_ref, mod_ref, mod_row, g_ref, win_ref, wo_ref, h_ref, acc_ref):
    sh = mod_ref[0, mod_row:mod_row + 1, :]
    sc = mod_ref[0, mod_row + 1:mod_row + 2, :]
    gt = mod_ref[0, mod_row + 2:mod_row + 3, :]
    h_ref[...] = (_rms(x) * g_ref[...] * (1.0 + sc) + sh).astype(BF16)
    acc_ref[...] = jnp.zeros_like(acc_ref)
    ff = wo_ref.shape[0]
    for c in range(ff // FF_CHUNK):
        lo, hi = c * FF_CHUNK, (c + 1) * FF_CHUNK
        h = h_ref[...]
        a = jnp.dot(h, win_ref[:, lo:hi], preferred_element_type=F32)
        b = jnp.dot(h, win_ref[:, ff + lo:ff + hi], preferred_element_type=F32)
        act = (_silu(a) * b).astype(BF16)
        acc_ref[...] += jnp.dot(act, wo_ref[lo:hi, :], preferred_element_type=F32)
    return x_keep_ref[...] + 0.5 * gt * acc_ref[...]


def _ffn1_kernel(x_ref, mod_ref, g_ref, win_ref, wo_ref, o_ref, h_ref, acc_ref):
    o_ref[0] = _swiglu_residual(x_ref[0], x_ref.at[0], mod_ref, 0, g_ref, win_ref, wo_ref, h_ref, acc_ref)


def _mix_ffn2_kernel(x_ref, mod_ref, yc_ref, of_ref, ob_ref, z_ref, gn_ref, wt_ref, wbm_ref,
                     g_ref, win_ref, wo_ref, gfin_ref, o_ref, h_ref, acc_ref, x2_ref, *, final_norm):
    o = of_ref[0] + ob_ref[0]
    gate = _silu(z_ref[0])
    gn = gn_ref[...]
    ys = []
    for h in range(N_HEADS):
        sl = slice(h * HEAD_DIM, (h + 1) * HEAD_DIM)
        ys.append(_rms(o[:, sl]) * gn * gate[:, sl])
    y_dn = jnp.concatenate(ys, axis=1)
    m = _bdot(yc_ref[0], wt_ref[...]) + _bdot(y_dn, wbm_ref[...])
    x2_ref[...] = x_ref[0] + mod_ref[0, 5:6, :] * m
    y = _swiglu_residual(x2_ref[...], x2_ref, mod_ref, 6, g_ref, win_ref, wo_ref, h_ref, acc_ref)
    if final_norm:
        y = _rms(y) * gfin_ref[...]
    o_ref[0] = y


def _token_specs(d, tm, per_batch_mod):
    mod_map = (lambda b, i: (b, 0, 0)) if per_batch_mod else (lambda b, i: (0, 0, 0))
    return pl.BlockSpec((1, tm, d), lambda b, i: (b, i, 0)), pl.BlockSpec((1, N_MOD, d), mod_map)


def _ffn_weight_specs(win, wo):
    assert win.shape[1] == 2 * wo.shape[0] and wo.shape[0] % FF_CHUNK == 0
    const2 = lambda b, i: (0, 0)
    return [pl.BlockSpec(win.shape, const2, pipeline_mode=pl.Buffered(1)),
            pl.BlockSpec(wo.shape, const2, pipeline_mode=pl.Buffered(1))]


def _ffn1(x, mod, per_batch_mod, g, win, wo, tm):
    bsz, seq, d = x.shape
    const2 = lambda b, i: (0, 0)
    x_spec, mod_spec = _token_specs(d, tm, per_batch_mod)
    return pl.pallas_call(
        _ffn1_kernel,
        grid=(bsz, seq // tm),
        in_specs=[x_spec, mod_spec, pl.BlockSpec((1, d), const2)] + _ffn_weight_specs(win, wo),
        out_specs=x_spec,
        out_shape=jax.ShapeDtypeStruct(x.shape, F32),
        scratch_shapes=[pltpu.VMEM((tm, d), BF16), pltpu.VMEM((tm, d), F32)],
        compiler_params=_cparams("parallel", "arbitrary"),
        name="ffn1",
    )(x, mod, g, win, wo)


def _mix_ffn2(x, mod, per_batch_mod, yc, o_f, o_b, z, gn, wt, wbm, g, win, wo, gfin, final_norm, tm):
    bsz, seq, d = x.shape
    half = yc.shape[-1]
    const2 = lambda b, i: (0, 0)
    x_spec, mod_spec = _token_specs(d, tm, per_batch_mod)
    half_spec = pl.BlockSpec((1, tm, half), lambda b, i: (b, i, 0))
    return pl.pallas_call(
        functools.partial(_mix_ffn2_kernel, final_norm=final_norm),
        grid=(bsz, seq // tm),
        in_specs=[x_spec, mod_spec] + [half_spec] * 4
                 + [pl.BlockSpec((1, HEAD_DIM), const2),
                    pl.BlockSpec((half, d), const2, pipeline_mode=pl.Buffered(1)),
                    pl.BlockSpec((half, d), const2, pipeline_mode=pl.Buffered(1)),
                    pl.BlockSpec((1, d), const2)]
                 + _ffn_weight_specs(win, wo) + [pl.BlockSpec((1, d), const2)],
        out_specs=x_spec,
        out_shape=jax.ShapeDtypeStruct(x.shape, F32),
        scratch_shapes=[pltpu.VMEM((tm, d), BF16), pltpu.VMEM((tm, d), F32), pltpu.VMEM((tm, d), F32)],
        compiler_params=_cparams("parallel", "arbitrary"),
        name="mix_out_ffn2",
    )(x, mod, yc, o_f, o_b, z, gn, wt, wbm, g, win, wo, gfin)


def _neighbour_rows(ext, halo, tm, dist, first, last):
    prev = ext[halo - dist:halo - dist + tm]
    nxt = ext[halo + dist:halo + dist + tm]
    row = lax.broadcasted_iota(jnp.int32, prev.shape, 0)
    prev = jnp.where(first & (row < dist), 0.0, prev)
    nxt = jnp.where(last & (row >= tm - dist), 0.0, nxt)
    return prev, nxt


def _proj_prep_kernel(x_ref, xp_ref, xn_ref, mod_ref, g_ref, wc_ref, wq_ref, wz_ref, wab_ref,
                      cw_ref, dw_ref, alog_ref, dtb_ref,
                      yc_ref, q_ref, k_ref, v_ref, gb_ref, pz_ref, *, conv_dist, n_tiles):
    i = pl.program_id(1)
    first = i == 0
    last = i == n_tiles - 1
    tm = x_ref.shape[1]
    halo = xp_ref.shape[1]
    cwid = yc_ref.shape[-1]
    kwid = q_ref.shape[-1]
    sh = mod_ref[0, 3:4, :]
    sc = mod_ref[0, 4:5, :]
    norm = lambda xv: (_rms(xv) * g_ref[...] * (1.0 + sc) + sh).astype(BF16)
    h = norm(x_ref[0])
    h_ext = jnp.concatenate([norm(xp_ref[0]), h, norm(xn_ref[0])], axis=0)
    dot = lambda a, b: jnp.dot(a, b, preferred_element_type=F32)

    qh = SUBLANES
    pq_ext = dot(h_ext[halo - qh:halo + tm + qh], wq_ref[...])
    pch_ext = dot(h_ext, wc_ref[:, cwid:])
    pq = pq_ext[qh:qh + tm]
    qp, qn = _neighbour_rows(pq_ext, qh, tm, 1, first, last)
    dw = dw_ref[...]
    qkv = _silu(qp * dw[0:1] + pq * dw[1:2] + qn * dw[2:3])
    for hd in range(N_HEADS):
        sl = slice(hd * HEAD_DIM, (hd + 1) * HEAD_DIM)
        qv = qkv[:, sl]
        kv = qkv[:, kwid + hd * HEAD_DIM: kwid + (hd + 1) * HEAD_DIM]
        q_ref[0, :, sl] = qv * (lax.rsqrt(jnp.sum(qv * qv, axis=-1, keepdims=True) + EPS) * HEAD_DIM ** -0.5)
        k_ref[0, :, sl] = kv * lax.rsqrt(jnp.sum(kv * kv, axis=-1, keepdims=True) + EPS)
    v_ref[0] = qkv[:, 2 * kwid:]

    pcb = dot(h, wc_ref[:, :cwid])
    pz_ref[0] = dot(h, wz_ref[...])
    ab = dot(h, wab_ref[...])
    u_ext = pch_ext[:, :cwid] * pch_ext[:, cwid:]
    u = u_ext[halo:halo + tm]
    up, un = _neighbour_rows(u_ext, halo, tm, conv_dist, first, last)
    cw = cw_ref[...]
    yc_ref[0] = pcb * (up * cw[0:1] + u * cw[1:2] + un * cw[2:3])

    lane = lax.broadcasted_iota(jnp.int32, ab.shape, 1)
    is_g = (lane < N_HEADS) | ((lane >= 2 * N_HEADS) & (lane < 3 * N_HEADS))
    xg = ab + dtb_ref[...]
    softplus = jnp.maximum(xg, 0.0) + jnp.log1p(jnp.exp(-jnp.abs(xg)))
    gates = jnp.where(is_g, -jnp.exp(alog_ref[...]) * softplus, _sigmoid(ab))
    r = lax.broadcasted_iota(jnp.int32, (CHUNK, CHUNK), 0)
    c = lax.broadcasted_iota(jnp.int32, (CHUNK, CHUNK), 1)
    tril = (r >= c).astype(BF16)
    triu = (r <= c).astype(BF16)
    g_hi = gates.astype(BF16)
    rem = gates - g_hi.astype(F32)
    g_mid = rem.astype(BF16)
    g_lo = (rem - g_mid.astype(F32)).astype(BF16)
    lane_c = lax.broadcasted_iota(jnp.int32, (CHUNK, ab.shape[1]), 1)
    is_gf_c = lane_c < N_HEADS
    is_gb_c = (lane_c >= 2 * N_HEADS) & (lane_c < 3 * N_HEADS)
    for t in range(tm // CHUNK):
        rs = slice(t * CHUNK, (t + 1) * CHUNK)
        parts = (g_hi[rs], g_mid[rs], g_lo[rs])
        pre = dot(tril, parts[0]) + (dot(tril, parts[1]) + dot(tril, parts[2]))
        suf = dot(triu, parts[0]) + (dot(triu, parts[1]) + dot(triu, parts[2]))
        gb_ref[0, rs, :] = jnp.where(is_gf_c, pre, jnp.where(is_gb_c, suf, gates[rs]))


def _proj_prep(x, mod, per_batch_mod, g, wc, wq, wz, wab, conv_w, dn_conv_w, alog_row, dtb_row, conv_dist, tm):
    bsz, seq, d = x.shape
    cwid, kwid = wc.shape[1] // 3, wq.shape[1] // 3
    n_tiles = seq // tm
    halo = max(conv_dist, SUBLANES)
    assert tm % halo == 0 and halo % SUBLANES == 0
    const2 = lambda b, i: (0, 0)
    tok = lambda b, i: (b, i, 0)
    x_spec, mod_spec = _token_specs(d, tm, per_batch_mod)
    prev_map = lambda b, i: (b, jnp.maximum(i * (tm // halo) - 1, 0), 0)
    next_map = lambda b, i: (b, jnp.minimum((i + 1) * (tm // halo), seq // halo - 1), 0)
    widths = (cwid, kwid, kwid, kwid, wab.shape[1], wz.shape[1])
    return pl.pallas_call(
        functools.partial(_proj_prep_kernel, conv_dist=conv_dist, n_tiles=n_tiles),
        grid=(bsz, n_tiles),
        in_specs=[x_spec, pl.BlockSpec((1, halo, d), prev_map), pl.BlockSpec((1, halo, d), next_map),
                  mod_spec, pl.BlockSpec((1, d), const2)]
                 + [pl.BlockSpec(w.shape, const2, pipeline_mode=pl.Buffered(1)) for w in (wc, wq, wz, wab)]
                 + [pl.BlockSpec(a.shape, const2) for a in (conv_w, dn_conv_w, alog_row, dtb_row)],
        out_specs=[pl.BlockSpec((1, tm, w), tok) for w in widths],
        out_shape=[jax.ShapeDtypeStruct((bsz, seq, w), F32) for w in widths],
        compiler_params=_cparams("parallel", "arbitrary"),
        name="mix_in_proj_prep",
    )(x, x, x, mod, g, wc, wq, wz, wab, conv_w, dn_conv_w, alog_row, dtb_row)


def _pair_block_diag(y16, keep_left, keep_right):
    return jnp.concatenate([y16 * keep_left, y16 * keep_right], axis=0)


INV_BASE = 4


def _pair_matmul_x3(lhs_list, rhs, keep_left, keep_right):
    d = lambda x, y: jnp.dot(x, y, preferred_element_type=F32)
    r_hi, r_lo = _split_bf16(rhs)
    r_hi = _pair_block_diag(r_hi, keep_left, keep_right)
    r_lo = _pair_block_diag(r_lo, keep_left, keep_right)
    parts = [_split_bf16(x) for x in lhs_list]
    l1 = jnp.concatenate([jnp.concatenate([hi, lo], axis=1) for hi, lo in parts], axis=0)
    l2 = jnp.concatenate([hi for hi, _ in parts], axis=0)
    res = d(l1, jnp.concatenate([r_hi, r_hi], axis=0)) + d(l2, r_lo)
    return [res[i * CHUNK:(i + 1) * CHUNK] for i in range(len(lhs_list))]


def _pair_matmul(lhs, rhs, keep_left, keep_right):
    return jnp.dot(lhs.astype(BF16), _pair_block_diag(rhs.astype(BF16), keep_left, keep_right),
                   preferred_element_type=F32)


def _pair_inverses(a_list, row, col, eye_pair, keep_left, keep_right):
    mm = lambda lhs_list, rhs: _pair_matmul_x3(lhs_list, rhs, keep_left, keep_right)
    mm1 = lambda lhs, rhs: _pair_matmul(lhs, rhs, keep_left, keep_right)
    shift = INV_BASE.bit_length() - 1
    same_base = (row >> shift) == (col >> shift)
    ps = [jnp.where(same_base, -a, 0.0) for a in a_list]
    ts = [eye_pair + p for p in ps]
    n_sq = shift - 1
    for i in range(n_sq + 1):
        new_ps, new_ts = [], []
        for p, t in zip(ps, ts):
            if i == 0:
                new_ps.append(mm([p], p)[0])
                new_ts.append(t)
            elif i == n_sq:
                new_ps.append(p)
                new_ts.append(t + mm([t], p)[0])
            else:
                p2, tp = mm([p, t], p)
                new_ps.append(p2)
                new_ts.append(t + tp)
        ps, ts = new_ps, new_ts
    size = INV_BASE
    while size < CHUNK:
        s1 = size.bit_length() - 1
        off = ((row >> (s1 + 1)) == (col >> (s1 + 1))) & ((row >> s1) != (col >> s1))
        es = [jnp.where(off, a, 0.0) for a in a_list]
        xs = [mm1(e, t) for e, t in zip(es, ts)]
        ts = [t - mm1(t, x) for t, x in zip(ts, xs)]
        size *= 2
    return ts


def _chunk_prep_kernel(q_ref, k_ref, v_ref, g_ref,
                       uf_ref, ub_ref, wf_ref, wb_ref, qgf_ref, qgb_ref, kgf_ref, kgb_ref,
                       qkf_ref, qkb_ref, glf_ref, glb_ref):
    u_refs, w_refs, qg_refs = (uf_ref, ub_ref), (wf_ref, wb_ref), (qgf_ref, qgb_ref)
    kg_refs, qk_refs, gl_refs = (kgf_ref, kgb_ref), (qkf_ref, qkb_ref), (glf_ref, glb_ref)
    row = lax.broadcasted_iota(jnp.int32, (CHUNK, 2 * CHUNK), 0)
    lane = lax.broadcasted_iota(jnp.int32, (CHUNK, 2 * CHUNK), 1)
    col = lane & (CHUNK - 1)
    left = lane < CHUNK
    keep_left = jnp.where(left, 1.0, 0.0).astype(BF16)
    keep_right = jnp.where(left, 0.0, 1.0).astype(BF16)
    eye_pair = jnp.where(row == col, 1.0, 0.0)
    zeros16 = jnp.zeros((CHUNK, HEAD_DIM), BF16)
    dot = lambda x, y: jnp.dot(x, y, preferred_element_type=F32)

    chains, a_list = [], []
    for c in range(q_ref.shape[1] // CHUNK):
        rs = slice(c * CHUNK, (c + 1) * CHUNK)
        gates = g_ref[0, rs, :]
        for p in range(N_PAIRS):
            sl = slice(2 * p * HEAD_DIM, 2 * (p + 1) * HEAD_DIM)
            q2, k2, v2 = q_ref[0, rs, sl], k_ref[0, rs, sl], v_ref[0, rs, sl]
            k16 = k2.astype(BF16)
            k_bd = jnp.concatenate([jnp.concatenate([k16[:, :HEAD_DIM], zeros16], axis=1),
                                    jnp.concatenate([zeros16, k16[:, HEAD_DIM:]], axis=1)], axis=0)
            kq = jnp.concatenate([k16, q2.astype(BF16)], axis=0)
            gram = lax.dot_general(kq, k_bd, (((1,), (1,)), ((), ())), preferred_element_type=F32)
            for d in range(N_DIRS):
                gcol, bcol = 2 * d * N_HEADS, (2 * d + 1) * N_HEADS
                gc = [gates[:, gcol + 2 * p + e:gcol + 2 * p + e + 1] for e in range(2)]
                beta = [gates[:, bcol + 2 * p + e:bcol + 2 * p + e + 1] for e in range(2)]
                gc_pair = jnp.where(left, gc[0], gc[1])
                beta_pair = jnp.where(left, beta[0], beta[1])
                gr_pair = jnp.sum(jnp.where(row == col, gc_pair, 0.0), axis=0, keepdims=True)
                if d == 0:
                    incl, strict, last = row >= col, row > col, CHUNK - 1
                else:
                    incl, strict, last = row <= col, row < col, 0
                decay = jnp.where(incl, jnp.exp(jnp.where(incl, gc_pair - gr_pair, 0.0)), 0.0)
                a_list.append(jnp.where(strict, beta_pair * gram[:CHUNK] * decay, 0.0))
                qk_refs[d][0, rs, p * HEAD_DIM:(p + 1) * HEAD_DIM] = (
                    jnp.where(incl, gram[CHUNK:] * decay, 0.0).astype(BF16))
                chains.append((c, rs, p, d, q2, k2, v2, gc, beta, last))

    t_list = _pair_inverses(a_list, row, col, eye_pair, keep_left, keep_right)

    for (c, rs, p, d, q2, k2, v2, gc, beta, last), t in zip(chains, t_list):
        t16 = t.astype(BF16)
        for e in range(2):
            h = 2 * p + e
            sl = slice(e * HEAD_DIM, (e + 1) * HEAD_DIM)
            hs = slice(h * HEAD_DIM, (h + 1) * HEAD_DIM)
            eg = jnp.exp(gc[e])
            g_last = gc[e][last:last + 1, :]
            rhs = jnp.concatenate([v2[:, sl] * beta[e], k2[:, sl] * (beta[e] * eg)], axis=1).astype(BF16)
            zpad = jnp.zeros_like(rhs)
            uw = dot(t16, jnp.concatenate([rhs, zpad] if e == 0 else [zpad, rhs], axis=0))
            u_refs[d][0, rs, hs] = uw[:, :HEAD_DIM]
            w_refs[d][0, rs, hs] = uw[:, HEAD_DIM:].astype(BF16)
            qg_refs[d][0, rs, hs] = (q2[:, sl] * eg).astype(BF16)
            kg_refs[d][0, rs, hs] = (k2[:, sl] * jnp.exp(g_last - gc[e])).astype(BF16)
            gl_refs[d][0, c, h:h + 1, :] = jnp.broadcast_to(jnp.exp(g_last), (1, HEAD_DIM))


def _chunk_prep(q, k, v, gb):
    bsz, seq, kwid = q.shape
    n = seq // CHUNK
    step_chunks = min(PREP_CHUNKS, n)
    rows = step_chunks * CHUNK
    tok = lambda b, i: (b, i, 0)
    f32_full = jax.ShapeDtypeStruct(q.shape, F32)
    bf_full = jax.ShapeDtypeStruct(q.shape, BF16)
    bf_half = jax.ShapeDtypeStruct((bsz, seq, kwid // 2), BF16)
    gl = jax.ShapeDtypeStruct((bsz, n, N_HEADS, HEAD_DIM), F32)
    outs = [f32_full] * 2 + [bf_full] * 6 + [bf_half] * 2 + [gl] * 2
    out_specs = ([pl.BlockSpec((1, rows, kwid), tok)] * 8 + [pl.BlockSpec((1, rows, kwid // 2), tok)] * 2
                 + [pl.BlockSpec((1, step_chunks, N_HEADS, HEAD_DIM), lambda b, i: (b, i, 0, 0))] * 2)
    return pl.pallas_call(
        _chunk_prep_kernel,
        grid=(bsz, seq // rows),
        in_specs=[pl.BlockSpec((1, rows, kwid), tok)] * 3 + [pl.BlockSpec((1, rows, gb.shape[-1]), tok)],
        out_specs=out_specs,
        out_shape=outs,
        compiler_params=_cparams("parallel", "parallel"),
        name="delta_chunk_prep",
    )(q, k, v, gb)


def _scan_kernel(uf_ref, wf_ref, qgf_ref, kgf_ref, qkf_ref, glf_ref,
                 ub_ref, wb_ref, qgb_ref, kgb_ref, qkb_ref, glb_ref, s0f_ref, s0b_ref,
                 of_ref, ob_ref, sff_ref, sfb_ref, s_ref, *, n_chunks, has_s0):
    j = pl.program_id(1)
    n_hd = N_DIRS * N_HEADS

    @pl.when(j == 0)
    def _():
        for r in range(SCAN_ROWS):
            if has_s0:
                s_ref[r * n_hd:r * n_hd + N_HEADS] = s0f_ref[r]
                s_ref[r * n_hd + N_HEADS:(r + 1) * n_hd] = s0b_ref[r]
            else:
                s_ref[r * n_hd:(r + 1) * n_hd] = jnp.zeros((n_hd, HEAD_DIM, HEAD_DIM), F32)

    dot = lambda x, y: jnp.dot(x, y, preferred_element_type=F32)
    zeros16 = jnp.zeros((CHUNK, HEAD_DIM), BF16)
    ins = ((uf_ref, wf_ref, qgf_ref, kgf_ref, qkf_ref, glf_ref), (ub_ref, wb_ref, qgb_ref, kgb_ref, qkb_ref, glb_ref))
    outs = (of_ref, ob_ref)
    hds = [(r, d, h) for r in range(SCAN_ROWS) for d in range(N_DIRS) for h in range(N_HEADS)]
    hsl = lambda h: slice(h * HEAD_DIM, (h + 1) * HEAD_DIM)
    sidx = lambda r, d, h: r * n_hd + d * N_HEADS + h

    s_old, ws_qs = [], []
    for r, d, h in hds:
        s = s_ref[sidx(r, d, h)]
        s_old.append(s)
        lhs = jnp.concatenate([ins[d][1][r, :, hsl(h)], ins[d][2][r, :, hsl(h)]], axis=0)
        ws_qs.append(dot(lhs, s.astype(BF16)))
    v_new = []
    for i, (r, d, h) in enumerate(hds):
        vn = (ins[d][0][r, :, hsl(h)] - ws_qs[i][:CHUNK]).astype(BF16)
        v_new.append(vn)
        v_pad = jnp.concatenate([vn, zeros16] if h % 2 == 0 else [zeros16, vn], axis=0)
        qk_pair = ins[d][4][r, :, hsl(h // 2)]
        outs[d][r, :, hsl(h)] = ws_qs[i][CHUNK:] + dot(qk_pair, v_pad)
    for i, (r, d, h) in enumerate(hds):
        kg = ins[d][3][r, :, hsl(h)]
        upd = lax.dot_general(kg, v_new[i], (((0,), (0,)), ((), ())), preferred_element_type=F32)
        s_ref[sidx(r, d, h)] = s_old[i] * ins[d][5][r, 0, h:h + 1, :] + upd

    @pl.when(j == n_chunks - 1)
    def _():
        for r in range(SCAN_ROWS):
            sff_ref[r] = s_ref[r * n_hd:r * n_hd + N_HEADS]
            sfb_ref[r] = s_ref[r * n_hd + N_HEADS:(r + 1) * n_hd]


def _scan(prep, s0_f, s0_b):
    (u_f, u_b, w_f, w_b, qg_f, qg_b, kg_f, kg_b, qk_f, qk_b, gl_f, gl_b) = prep
    bsz, seq, kwid = u_f.shape
    assert bsz % SCAN_ROWS == 0
    n = seq // CHUNK
    has_s0 = s0_f is not None
    sblk = (SCAN_ROWS, N_HEADS, HEAD_DIM, HEAD_DIM)
    if not has_s0:
        s0_f = s0_b = jnp.zeros(sblk, F32)
    fwd = lambda b, j: (b, j, 0)
    bwd = lambda b, j: (b, n - 1 - j, 0)
    fwd4 = lambda b, j: (b, j, 0, 0)
    bwd4 = lambda b, j: (b, n - 1 - j, 0, 0)
    smap = (lambda b, j: (b, 0, 0, 0)) if has_s0 else (lambda b, j: (0, 0, 0, 0))

    def dir_specs(tok, tok4):
        return ([pl.BlockSpec((SCAN_ROWS, CHUNK, kwid), tok)] * 4
                + [pl.BlockSpec((SCAN_ROWS, CHUNK, kwid // 2), tok),
                   pl.BlockSpec((SCAN_ROWS, 1, N_HEADS, HEAD_DIM), tok4)])

    return pl.pallas_call(
        functools.partial(_scan_kernel, n_chunks=n, has_s0=has_s0),
        grid=(bsz // SCAN_ROWS, n),
        in_specs=dir_specs(fwd, fwd4) + dir_specs(bwd, bwd4) + [pl.BlockSpec(sblk, smap)] * 2,
        out_specs=[pl.BlockSpec((SCAN_ROWS, CHUNK, kwid), fwd),
                   pl.BlockSpec((SCAN_ROWS, CHUNK, kwid), bwd),
                   pl.BlockSpec(sblk, lambda b, j: (b, 0, 0, 0)),
                   pl.BlockSpec(sblk, lambda b, j: (b, 0, 0, 0))],
        out_shape=[jax.ShapeDtypeStruct(u_f.shape, F32)] * 2
                  + [jax.ShapeDtypeStruct((bsz,) + sblk[1:], F32)] * 2,
        scratch_shapes=[pltpu.VMEM((SCAN_ROWS * N_DIRS * N_HEADS, HEAD_DIM, HEAD_DIM), F32)],
        compiler_params=_cparams("parallel", "arbitrary"),
        name="delta_scan",
    )(u_f, w_f, qg_f, kg_f, qk_f, gl_f, u_b, w_b, qg_b, kg_b, qk_b, gl_b, s0_f, s0_b)


def _layer(x, mod, per_batch_mod, s0_f, s0_b, on_grid, lw, norm_final, final_norm, tm, fold):
    (g1, win1, wo1, gm, wc, wq, wz, wab, conv_w, dn_conv_w, alog_row, dtb_row, gn, wt, wbo,
     g2, win2, wo2) = lw
    bsz, seq, d = x.shape
    assert fold == 1 or not per_batch_mod
    folded = lambda a: a.reshape(bsz // fold, fold * seq, a.shape[-1])
    unfolded = lambda a: a.reshape(bsz, seq, a.shape[-1])
    x = _ffn1(folded(x), mod, per_batch_mod, g1, win1, wo1, min(FFN1_TILE, fold * seq))
    yc, q, k, v, gb, pz = _proj_prep(unfolded(x), mod, per_batch_mod, gm, wc, wq, wz, wab, conv_w, dn_conv_w,
                                     alog_row, dtb_row, GRID_W if on_grid else 1, min(tm, seq))
    o_f, o_b, s_f, s_b = _scan(_chunk_prep(q, k, v, gb), s0_f, s0_b)
    y = _mix_ffn2(x, mod, per_batch_mod, folded(yc), folded(o_f), folded(o_b), folded(pz), gn, wt, wbo,
                  g2, win2, wo2, norm_final, final_norm, tm)
    return unfolded(y), s_f, s_b


def kernel(x_prompt, x_sample, state_dn_fwd, state_dn_bwd, c, c_ctx, w_ada, b_ada, norm_ffn1, w_ffn1_in,
           w_ffn1_out, norm_mix, w_mix_in, conv_w, dn_conv_w, dn_a_log, dn_dt_bias, dn_norm, w_mix_out,
           norm_ffn2, w_ffn2_in, w_ffn2_out, norm_final):
    depth = w_ada.shape[0]
    d = x_prompt.shape[-1]
    n_lat = c.shape[0]
    cwid = conv_w.shape[-1]
    kwid = N_HEADS * HEAD_DIM
    n_gate = 4 * N_HEADS
    row = lambda a: a.reshape(1, -1)

    cvec = jnp.concatenate([c_ctx[None, :], c, jnp.zeros((16 - 1 - n_lat, d), F32)], axis=0)

    xp, xs = x_prompt, x_sample
    new_f, new_b = [], []
    for l in range(depth):
        mod = _modulation(cvec, w_ada[l], b_ada[l]).reshape(16, N_MOD, d)
        mod_ctx, mod_lat = mod[0:1], mod[1:1 + n_lat]

        win1, wo1 = w_ffn1_in[l].astype(BF16), w_ffn1_out[l].astype(BF16)
        win2, wo2 = w_ffn2_in[l].astype(BF16), w_ffn2_out[l].astype(BF16)
        wm = w_mix_in[l].astype(BF16)
        wc = wm[:, :3 * cwid]
        wq = wm[:, 3 * cwid:3 * cwid + 3 * kwid]
        wz = wm[:, 3 * cwid + 3 * kwid:3 * cwid + 4 * kwid]
        wab = jnp.pad(wm[:, 3 * cwid + 4 * kwid:], ((0, 0), (0, HEAD_DIM - n_gate)))
        zpad = jnp.zeros((N_HEADS,), F32)
        alog_row = jnp.concatenate([dn_a_log[l, 0], zpad, dn_a_log[l, 1], zpad,
                                    jnp.zeros((HEAD_DIM - n_gate,), F32)])[None, :]
        dtb_row = jnp.concatenate([dn_dt_bias[l, 0], zpad, dn_dt_bias[l, 1], zpad,
                                   jnp.zeros((HEAD_DIM - n_gate,), F32)])[None, :]
        wo = w_mix_out[l].astype(BF16)
        lw = (row(norm_ffn1[l]), win1, wo1, row(norm_mix[l]), wc, wq, wz, wab, conv_w[l], dn_conv_w[l],
              alog_row, dtb_row, row(dn_norm[l]), wo[:cwid], wo[cwid:], row(norm_ffn2[l]), win2, wo2)
        last = l == depth - 1
        xp, sf, sb = _layer(xp, mod_ctx, False, None, None, False, lw, row(norm_final), last,
                            TOKEN_TILE, TOKEN_TILE // xp.shape[1])
        new_f.append(sf)
        new_b.append(sb)
        xs, _, _ = _layer(xs, mod_lat, True, state_dn_fwd[:, l], state_dn_bwd[:, l], True, lw,
                          row(norm_final), last, TOKEN_TILE, 1)
    return xp, xs, jnp.stack(new_f, axis=1), jnp.stack(new_b, axis=1)
```
